```python
import math
import jax
import jax.numpy as jnp
from jax import lax
import numpy as np

D_MODEL = 1024
BATCH = 4
SEQ = 4096
DEPTH = 4

GRID_W = 64
CTX_LEN = 256
N_GROUPS = 4
GROUP_W = D_MODEL // N_GROUPS
GROUP_HEADS = 4
HEAD_DIM = GROUP_W // GROUP_HEADS
NA_WIN_H = 8
NA_WIN_W = 16
DN_CONV_K = 5
DN_CHUNK = 64
DIFF_DIM = HEAD_DIM // 2
Q_BLOCK = 128
FT_DIM = GROUP_W // GROUP_HEADS
N_EXPERTS = 16
EC_FACTOR = 2
D_EXPERT = 2 * D_MODEL
ROPE_BASE = 10000.0
RMS_EPS = 1e-6
IN_WIDTHS = (GROUP_W, GROUP_W, GROUP_W, 3 * GROUP_W, 2 * GROUP_HEADS, 2 * GROUP_HEADS, GROUP_W, GROUP_W, GROUP_W, GROUP_W, GROUP_W)
IN_W = 11 * GROUP_W + 4 * GROUP_HEADS
F32 = jnp.float32

kernel_name = 'hybrid_parallel_group_dit_block'


def rmsnorm(x, g):
    xf = x.astype(F32)
    y = xf * lax.rsqrt(jnp.mean(xf * xf, axis=-1, keepdims=True) + RMS_EPS)
    return (y * g.astype(F32)).astype(x.dtype)


def l2norm(x):
    return x * lax.rsqrt(jnp.sum(x * x, axis=-1, keepdims=True) + 1e-6)


def modulate(h, shift, scale):
    return h * (1 + scale) + shift


def split_heads(t, d=HEAD_DIM):
    return t.reshape(*t.shape[:-1], t.shape[-1] // d, d)


def split_proj(p):
    idx, acc = [], 0
    for w in IN_WIDTHS[:-1]:
        acc += w
        idx.append(acc)
    return jnp.split(p, idx, axis=-1)


def axial_rope(n_tok, dim):
    t = jnp.arange(n_tok)
    pos = jnp.stack([t // GRID_W, t % GRID_W], axis=-1).astype(F32)
    n_freq = dim // 4
    inv = ROPE_BASE ** (-jnp.arange(n_freq, dtype=F32) / n_freq)
    ang = (pos[:, :, None] * inv).reshape(n_tok, 2 * n_freq)
    return jnp.cos(ang), jnp.sin(ang)


def apply_rope(x, cos, sin):
    xf = x.astype(F32).reshape(*x.shape[:-1], -1, 2)
    x1, x2 = xf[..., 0], xf[..., 1]
    out = jnp.stack([x1 * cos - x2 * sin, x1 * sin + x2 * cos], axis=-1)
    return out.reshape(x.shape).astype(x.dtype)


def softmax_attention(q, k, v):
    s = jnp.einsum('bqhd,bkhd->bhqk', q, k).astype(F32) * q.shape[-1] ** -0.5
    p = jax.nn.softmax(s, axis=-1)
    return jnp.einsum('bhqk,bkhd->bqhd', p, v.astype(F32))


def neighbourhood_attention(q, k, v, k_ctx, v_ctx, rpb):
    B, S, H, dh = q.shape
    rows = S // GRID_W
    wh = min(NA_WIN_H, rows)
    r = jnp.arange(rows)
    row_idx = jnp.clip(r - wh // 2, 0, rows - wh)[:, None] + jnp.arange(wh)[None, :]
    cq = jnp.arange(GRID_W)
    c0 = jnp.clip(cq - NA_WIN_W // 2, 0, GRID_W - NA_WIN_W)
    in_win = (cq[None, :] >= c0[:, None]) & (cq[None, :] < c0[:, None] + NA_WIN_W)
    dy = row_idx - r[:, None] + NA_WIN_H - 1
    dx = jnp.clip(cq[None, :] - cq[:, None], 1 - NA_WIN_W, NA_WIN_W - 1) + NA_WIN_W - 1
    bias = rpb[:, dy[:, None, :, None], dx[None, :, None, :]].astype(F32)
    qg = q.reshape(B, rows, GRID_W, H, dh)
    k_blk = k.reshape(B, rows, GRID_W, H, dh)[:, row_idx]
    v_blk = v.reshape(B, rows, GRID_W, H, dh)[:, row_idx]
    scale = dh ** -0.5
    s_win = jnp.einsum('brqhd,brwkhd->bhrqwk', qg, k_blk).astype(F32) * scale + bias
    s_win = jnp.where(in_win[:, None, :], s_win, -jnp.inf)
    s_ctx = jnp.einsum('brqhd,blhd->bhrql', qg, k_ctx).astype(F32) * scale
    n_win = wh * GRID_W
    p = jax.nn.softmax(jnp.concatenate([s_win.reshape(B, H, rows, GRID_W, n_win), s_ctx], axis=-1), axis=-1)
    p_win = p[..., :n_win].reshape(B, H, rows, GRID_W, wh, GRID_W)
    p_ctx = p[..., n_win:]
    out = jnp.einsum('bhrqwk,brwkhd->brqhd', p_win, v_blk.astype(F32)) + jnp.einsum('bhrql,blhd->brqhd', p_ctx, v_ctx.astype(F32))
    return out.reshape(B, S, H, dh)


def short_conv(u, w):
    return lax.conv_general_dilated(u, w[:, None, :].astype(u.dtype), window_strides=(1,), padding=[(DN_CONV_K // 2, DN_CONV_K // 2)], dimension_numbers=('NWC', 'WIO', 'NWC'), feature_group_count=u.shape[-1])


def chunk_gated_delta(q, k, v, g, beta, s0):
    B, T, H, dk = q.shape
    n, C = T // DN_CHUNK, DN_CHUNK

    def chunks(t):
        return jnp.swapaxes(t.reshape(B, n, C, H, *t.shape[3:]), 2, 3)
    qc, kc, vc, bc = chunks(q), chunks(k), chunks(v), chunks(beta)
    gc = jnp.cumsum(chunks(g), axis=-1)
    incl = jnp.tril(jnp.ones((C, C), bool))
    strict = jnp.tril(jnp.ones((C, C), bool), -1)
    decay = jnp.exp(jnp.where(incl, gc[..., :, None] - gc[..., None, :], -jnp.inf))
    kb = kc * bc[..., None]
    lower = jnp.where(strict, jnp.einsum('bnhid,bnhjd->bnhij', kb, kc) * decay, 0.0)
    eye = jnp.eye(C, dtype=F32)
    t_inv = lax.linalg.triangular_solve(eye + lower, jnp.broadcast_to(eye, lower.shape), left_side=True, lower=True, unit_diagonal=True)
    u = t_inv @ (vc * bc[..., None])
    w = t_inv @ (kb * jnp.exp(gc)[..., None])
    qk = jnp.where(incl, jnp.einsum('bnhid,bnhjd->bnhij', qc, kc) * decay, 0.0)
    q_dec = qc * jnp.exp(gc)[..., None]
    k_dec = kc * jnp.exp(gc[..., -1:] - gc)[..., None]
    g_last = jnp.exp(gc[..., -1])

    def step(s, xs):
        u_i, w_i, qk_i, qd_i, kd_i, gl_i = xs
        v_new = u_i - jnp.einsum('bhcd,bhde->bhce', w_i, s)
        o_i = jnp.einsum('bhcd,bhde->bhce', qd_i, s) + jnp.einsum('bhcj,bhje->bhce', qk_i, v_new)
        s = s * gl_i[..., None, None] + jnp.einsum('bhcd,bhce->bhde', kd_i, v_new)
        return s, o_i
    xs = tuple(jnp.swapaxes(t, 0, 1) for t in (u, w, qk, q_dec, k_dec, g_last))
    s_fin, o = lax.scan(step, s0, xs)
    o = jnp.swapaxes(jnp.swapaxes(o, 0, 1), 2, 3).reshape(B, T, H, -1)
    return o, s_fin


def gated_deltanet(lat, ctx, conv_w, a_log, dt_bias, norm_g):
    def prep(qkv, a, b):
        B, T, _ = qkv.shape
        qkv = jax.nn.silu(short_conv(qkv, conv_w)).astype(F32)
        q, k, v = [split_heads(t) for t in jnp.split(qkv, 3, axis=-1)]
        q = l2norm(q) * HEAD_DIM ** -0.5
        k = l2norm(k)
        a = a.astype(F32).reshape(B, T, 2, GROUP_HEADS)
        b = b.astype(F32).reshape(B, T, 2, GROUP_HEADS)
        g = -jnp.exp(a_log.astype(F32)) * jax.nn.softplus(a + dt_bias.astype(F32))
        return q, k, v, g, jax.nn.sigmoid(b)
    ql, kl, vl, gl, bl = prep(lat[0], lat[1], lat[2])
    qc, kc, vc, gc, bc = prep(ctx[0], ctx[1], ctx[2])
    s0 = jnp.zeros((ql.shape[0], GROUP_HEADS, HEAD_DIM, HEAD_DIM), F32)
    flip = lambda t: jnp.flip(t, axis=1)
    oc_f, sc_f = chunk_gated_delta(qc, kc, vc, gc[:, :, 0], bc[:, :, 0], s0)
    ol_f, _ = chunk_gated_delta(ql, kl, vl, gl[:, :, 0], bl[:, :, 0], sc_f)
    oc_b, sc_b = chunk_gated_delta(flip(qc), flip(kc), flip(vc), flip(gc[:, :, 1]), flip(bc[:, :, 1]), s0)
    ol_b, _ = chunk_gated_delta(flip(ql), flip(kl), flip(vl), flip(gl[:, :, 1]), flip(bl[:, :, 1]), sc_b)

    def finish(o, gate):
        y = rmsnorm(o, norm_g) * jax.nn.silu(split_heads(gate.astype(F32)))
        return y.reshape(gate.shape).astype(gate.dtype)
    return finish(ol_f + flip(ol_b), lat[3]), finish(oc_f + flip(oc_b), ctx[3])


def diff_core(q, k, v, lam):
    s = jnp.einsum('bqhmd,bkhmd->bhmqk', q, k).astype(F32) * q.shape[-1] ** -0.5
    p = jax.nn.softmax(s, axis=-1)
    a = p[:, :, 0] - lam * p[:, :, 1]
    return jnp.einsum('bhqk,bkhd->bqhd', a, v.astype(F32))


def fourier_mix(u, w):
    B, T, _ = u.shape
    uf = u.astype(F32).reshape(B, T, GROUP_HEADS, FT_DIM)
    y = jnp.fft.fft2(uf, axes=(1, 3), norm='ortho').real.reshape(B, T, GROUP_W)
    return (y @ w.astype(F32)).astype(u.dtype)


def expert_choice_ffn(h, w_router, w_gate, w_up, w_down):
    B, N, D = h.shape
    cap = EC_FACTOR * N // N_EXPERTS
    aff = jax.nn.softmax((h @ w_router).astype(F32), axis=-1)
    gate, idx = lax.top_k(jnp.swapaxes(aff, 1, 2), cap)
    xg = jax.vmap(lambda hb, ib: hb[ib])(h, idx)
    hid = jax.nn.silu(jnp.einsum('becd,edf->becf', xg, w_gate)) * jnp.einsum('becd,edf->becf', xg, w_up)
    y = jnp.einsum('becf,efd->becd', hid, w_down) * gate[..., None].astype(h.dtype)
    return jax.vmap(lambda yb, ib: jnp.zeros((N, D), yb.dtype).at[ib.reshape(-1)].add(yb.reshape(-1, D)))(y, idx)


def token_mixers(h_lat, h_ctx, w_in, rpb, conv_w, a_log, dt_bias, dn_g, df_lam, lam_init, df_g, ft_w, cos, sin, ctx_out):
    B, S, _ = h_lat.shape
    L = h_ctx.shape[1]
    (na_q, na_k, na_v, dn_qkv, dn_a, dn_b, dn_gate, df_q, df_k, df_v, ft_u) = split_proj(h_lat @ w_in)
    (cna_q, cna_k, cna_v, cdn_qkv, cdn_a, cdn_b, cdn_gate, cdf_q, cdf_k, cdf_v, cft_u) = split_proj(h_ctx @ w_in)
    diff_heads = lambda t: t.reshape(*t.shape[:-1], GROUP_HEADS, 2, DIFF_DIM)
    dt = h_lat.dtype
    o_na = neighbourhood_attention(split_heads(na_q), split_heads(na_k), split_heads(na_v), split_heads(cna_k), split_heads(cna_v), rpb)
    o_dn, c_dn = gated_deltanet((dn_qkv, dn_a, dn_b, dn_gate), (cdn_qkv, cdn_a, cdn_b, cdn_gate), conv_w, a_log, dt_bias, dn_g)
    lq1, lk1, lq2, lk2 = df_lam.astype(F32)
    lam = jnp.exp(jnp.sum(lq1 * lk1)) - jnp.exp(jnp.sum(lq2 * lk2)) + lam_init
    q_d = apply_rope(diff_heads(df_q), cos, sin)
    k_d = apply_rope(diff_heads(df_k), cos, sin)
    kc_d, vc_d = diff_heads(cdf_k), split_heads(cdf_v)
    k_all = jnp.concatenate([kc_d, k_d], axis=1)
    v_all = jnp.concatenate([vc_d, split_heads(df_v)], axis=1)
    q_blocks = jnp.swapaxes(q_d.reshape(B, S // Q_BLOCK, Q_BLOCK, GROUP_HEADS, 2, DIFF_DIM), 0, 1)
    o_df = lax.map(lambda qb: diff_core(qb, k_all, v_all, lam), q_blocks)
    o_df = jnp.swapaxes(o_df, 0, 1).reshape(B, S, GROUP_HEADS, HEAD_DIM)
    diff_out = lambda o: rmsnorm(o, df_g) * (1.0 - lam_init)
    o_ft = fourier_mix(ft_u, ft_w)
    mix_lat = jnp.concatenate([o_na.reshape(B, S, GROUP_W).astype(dt), o_dn.astype(dt), diff_out(o_df).reshape(B, S, GROUP_W).astype(dt), o_ft.astype(dt)], axis=-1)
    if not ctx_out:
        return mix_lat, None
    c_na = softmax_attention(split_heads(cna_q), split_heads(cna_k), split_heads(cna_v))
    c_df = diff_out(diff_core(diff_heads(cdf_q), kc_d, vc_d, lam))
    c_ft = fourier_mix(cft_u, ft_w)
    mix_ctx = jnp.concatenate([c_na.reshape(B, L, GROUP_W).astype(dt), c_dn.astype(dt), c_df.reshape(B, L, GROUP_W).astype(dt), c_ft.astype(dt)], axis=-1)
    return mix_lat, mix_ctx


def setup_inputs(seed: int = 0) -> dict:
    key = jax.random.key(seed)
    ks = jax.random.split(key, 24)
    nrm = lambda k, shape, s: jax.random.normal(k, shape, F32) * s
    H = GROUP_HEADS
    a_init = jax.random.uniform(ks[9], (DEPTH, 2, H), F32, 1.0, 16.0)
    dt_init = jnp.exp(jax.random.uniform(ks[10], (DEPTH, 2, H), F32, math.log(1e-3), math.log(1e-1)))
    return {
        'x': nrm(ks[0], (BATCH, SEQ, D_MODEL), 1.0),
        'c': nrm(ks[1], (BATCH, D_MODEL), 1.0),
        'ctx': nrm(ks[2], (BATCH, CTX_LEN, D_MODEL), 1.0),
        'c_ctx': nrm(ks[3], (D_MODEL,), 1.0),
        'w_mod': nrm(ks[4], (DEPTH, D_MODEL, 6 * D_MODEL), 0.5 * D_MODEL ** -0.5),
        'b_mod': nrm(ks[5], (DEPTH, 6 * D_MODEL), 0.01),
        'norm1_g': 1.0 + nrm(ks[6], (DEPTH, D_MODEL), 0.02),
        'w_in': nrm(ks[7], (DEPTH, D_MODEL, IN_W), D_MODEL ** -0.5),
        'na_rpb': nrm(ks[8], (DEPTH, H, 2 * NA_WIN_H - 1, 2 * NA_WIN_W - 1), 0.02),
        'dn_conv_w': nrm(ks[11], (DEPTH, DN_CONV_K, 3 * GROUP_W), DN_CONV_K ** -0.5),
        'dn_a_log': jnp.log(a_init),
        'dn_dt_bias': dt_init + jnp.log(-jnp.expm1(-dt_init)),
        'dn_norm_g': 1.0 + nrm(ks[12], (DEPTH, HEAD_DIM), 0.02),
        'df_lambda': nrm(ks[13], (DEPTH, 4, DIFF_DIM), 0.1),
        'df_norm_g': 1.0 + nrm(ks[14], (DEPTH, HEAD_DIM), 0.02),
        'ft_w': nrm(ks[15], (DEPTH, GROUP_W, GROUP_W), GROUP_W ** -0.5),
        'w_out': nrm(ks[16], (DEPTH, D_MODEL, D_MODEL), D_MODEL ** -0.5),
        'norm2_g': 1.0 + nrm(ks[17], (DEPTH, D_MODEL), 0.02),
        'w_router': nrm(ks[18], (DEPTH, D_MODEL, N_EXPERTS), D_MODEL ** -0.5),
        'w_gate': nrm(ks[19], (DEPTH, N_EXPERTS, D_MODEL, D_EXPERT), D_MODEL ** -0.5),
        'w_up': nrm(ks[20], (DEPTH, N_EXPERTS, D_MODEL, D_EXPERT), D_MODEL ** -0.5),
        'w_down': nrm(ks[21], (DEPTH, N_EXPERTS, D_EXPERT, D_MODEL), D_EXPERT ** -0.5),
        'final_norm_g': 1.0 + nrm(ks[22], (D_MODEL,), 0.02),
    }


def reference(x, c, ctx, c_ctx, w_mod, b_mod, norm1_g, w_in, na_rpb, dn_conv_w, dn_a_log, dn_dt_bias, dn_norm_g, df_lambda, df_norm_g, ft_w, w_out, norm2_g, w_router, w_gate, w_up, w_down, final_norm_g):
    S = x.shape[1]
    cos, sin = axial_rope(S, DIFF_DIM)
    cos, sin = cos[:, None, None, :], sin[:, None, None, :]
    s_lat = jax.nn.silu(c)
    s_ctx = jax.nn.silu(c_ctx)
    x_lat, x_ctx = x, ctx
    for l in range(DEPTH):
        ctx_out = l < DEPTH - 1
        lam_init = 0.8 - 0.6 * math.exp(-0.3 * l)
        m_lat = jnp.split((s_lat @ w_mod[l] + b_mod[l])[:, None, :], 6, axis=-1)
        m_ctx = jnp.split(s_ctx @ w_mod[l] + b_mod[l], 6, axis=-1)
        h_lat = modulate(rmsnorm(x_lat, norm1_g[l]), m_lat[0], m_lat[1])
        h_ctx = modulate(rmsnorm(x_ctx, norm1_g[l]), m_ctx[0], m_ctx[1])
        mix_lat, mix_ctx = token_mixers(h_lat, h_ctx, w_in[l], na_rpb[l], dn_conv_w[l], dn_a_log[l], dn_dt_bias[l], dn_norm_g[l], df_lambda[l], lam_init, df_norm_g[l], ft_w[l], cos, sin, ctx_out)
        x_lat = x_lat + m_lat[2] * (mix_lat @ w_out[l])
        h_lat = modulate(rmsnorm(x_lat, norm2_g[l]), m_lat[3], m_lat[4])
        x_lat = x_lat + m_lat[5] * expert_choice_ffn(h_lat, w_router[l], w_gate[l], w_up[l], w_down[l])
        if ctx_out:
            x_ctx = x_ctx + m_ctx[2] * (mix_ctx @ w_out[l])
            h_ctx = modulate(rmsnorm(x_ctx, norm2_g[l]), m_ctx[3], m_ctx[4])
            x_ctx = x_ctx + m_ctx[5] * expert_choice_ffn(h_ctx, w_router[l], w_gate[l], w_up[l], w_down[l])
    return rmsnorm(x_lat, final_norm_g)
```

```python
import functools
import math

import numpy as np
import jax
import jax.numpy as jnp
from jax import lax
from jax.experimental import pallas as pl
from jax.experimental.pallas import tpu as pltpu

F32 = jnp.float32
BF16 = jnp.bfloat16

D_MODEL = 1024
DEPTH = 4
GRID_W = 64
GROUP_W = 256
GROUP_HEADS = 4
HEAD_DIM = 64
NA_WIN_H = 8
NA_WIN_W = 16
DN_CONV_K = 5
DN_CHUNK = 64
DIFF_DIM = 32
FT_DIM = 64
N_EXPERTS = 16
EC_FACTOR = 2
D_EXPERT = 2 * D_MODEL
ROPE_BASE = 10000.0
RMS_EPS = 1e-6
NEG = -1e30

ROW_TILE = 256
N_BF = 6 * GROUP_W
N_F32 = 4 * GROUP_W + 128
VMEM_LIMIT = 56 * 1024 * 1024


def _cparams(*sem):
    return pltpu.CompilerParams(dimension_semantics=sem, vmem_limit_bytes=VMEM_LIMIT)


def _split(x):
    hi = x.astype(BF16)
    lo = (x - hi.astype(F32)).astype(BF16)
    return hi, lo


def _dot(a, b):
    return jnp.dot(a, b, preferred_element_type=F32)


def _dot_nt(a, b):
    return lax.dot_general(a, b, (((1,), (1,)), ((), ())), preferred_element_type=F32)


def _dot_tn(a, b):
    return lax.dot_general(a, b, (((0,), (0,)), ((), ())), preferred_element_type=F32)


def _dot3(a, b):
    ah, al = _split(a)
    bh, bl = _split(b)
    return _dot(ah, bh) + (_dot(al, bh) + _dot(ah, bl))


def _dot3_const(ah, al, b):
    bh, bl = _split(b)
    return _dot(ah, bh) + (_dot(al, bh) + _dot(ah, bl))


def _silu(x):
    return x * jax.nn.sigmoid(x)


def _mod_kernel(s_ref, w_ref, b_ref, o_ref):
    s = _silu(s_ref[...])
    o_ref[...] = _dot(s.astype(BF16), w_ref[...].astype(BF16)) + b_ref[...]


def _modulation(cc, w_mod, b_mod):
    depth = w_mod.shape[0]
    nt = 6 * D_MODEL // 1024
    return pl.pallas_call(
        _mod_kernel,
        grid=(depth, nt),
        in_specs=[
            pl.BlockSpec((8, D_MODEL), lambda l, j: (0, 0)),
            pl.BlockSpec((None, D_MODEL, 1024), lambda l, j: (l, 0, j)),
            pl.BlockSpec((None, 1, 1024), lambda l, j: (l, 0, j)),
        ],
        out_specs=pl.BlockSpec((None, 8, 1024), lambda l, j: (l, 0, j)),
        out_shape=jax.ShapeDtypeStruct((depth, 8, 6 * D_MODEL), F32),
        compiler_params=_cparams("parallel", "parallel"),
        name="modulation",
    )(cc, w_mod, b_mod.reshape(depth, 1, 6 * D_MODEL))


def _norm_mod(x, g, shift, scale):
    y = x * lax.rsqrt(jnp.mean(x * x, axis=-1, keepdims=True) + RMS_EPS) * g
    return y * (1.0 + scale) + shift


def _in_kernel(x_ref, g_ref, m_ref, w_ref, cos_ref, sin_ref, obf_ref, of_ref, ou_ref):
    h = _norm_mod(x_ref[...], g_ref[...], m_ref[0], m_ref[1])
    p = _dot(h.astype(BF16), w_ref[...])
    gw = GROUP_W
    lane = lax.broadcasted_iota(jnp.int32, (1, gw), 1)
    first_half = (lane % DIFF_DIM) < (DIFF_DIM // 2)
    cos, sin = cos_ref[...], sin_ref[...]

    def rope(t):
        sw = jnp.where(first_half, pltpu.roll(t, gw - DIFF_DIM // 2, 1), pltpu.roll(t, DIFF_DIM // 2, 1))
        return t * cos + sw * sin

    obf_ref[:, 0:gw] = (p[:, 0:gw] * HEAD_DIM ** -0.5).astype(BF16)
    obf_ref[:, gw:3 * gw] = p[:, gw:3 * gw].astype(BF16)
    obf_ref[:, 3 * gw:4 * gw] = (rope(p[:, 3 * gw:4 * gw]) * DIFF_DIM ** -0.5).astype(BF16)
    obf_ref[:, 4 * gw:5 * gw] = rope(p[:, 4 * gw:5 * gw]).astype(BF16)
    obf_ref[:, 5 * gw:6 * gw] = p[:, 5 * gw:6 * gw].astype(BF16)
    of_ref[:, 0:4 * gw] = p[:, 6 * gw:10 * gw]
    of_ref[:, 4 * gw:4 * gw + 128] = p[:, 11 * gw:11 * gw + 128]
    ou_ref[...] = p[:, 10 * gw:11 * gw]


def _in_proj(x, g, mods, w_perm, cos_t, sin_t, n_batch, n_tok):
    nt = n_tok // ROW_TILE
    ctx_tile = nt - 1
    n_out = w_perm.shape[1]
    row = lambda b, i: (b, i, 0)
    return pl.pallas_call(
        _in_kernel,
        grid=(n_batch, nt),
        in_specs=[
            pl.BlockSpec((None, ROW_TILE, D_MODEL), row),
            pl.BlockSpec((1, D_MODEL), lambda b, i: (0, 0)),
            pl.BlockSpec((None, 6, 1, D_MODEL), lambda b, i: (jnp.where(i == ctx_tile, n_batch, b), 0, 0, 0)),
            pl.BlockSpec((D_MODEL, n_out), lambda b, i: (0, 0)),
            pl.BlockSpec((ROW_TILE, GROUP_W), lambda b, i: (i, 0)),
            pl.BlockSpec((ROW_TILE, GROUP_W), lambda b, i: (i, 0)),
        ],
        out_specs=[
            pl.BlockSpec((None, ROW_TILE, N_BF), row),
            pl.BlockSpec((None, ROW_TILE, N_F32), row),
            pl.BlockSpec((None, ROW_TILE, GROUP_W), row),
        ],
        out_shape=[
            jax.ShapeDtypeStruct((n_batch, n_tok, N_BF), BF16),
            jax.ShapeDtypeStruct((n_batch, n_tok, N_F32), F32),
            jax.ShapeDtypeStruct((n_batch, n_tok, GROUP_W), F32),
        ],
        compiler_params=_cparams("parallel", "parallel"),
        name="in_proj",
    )(x, g.reshape(1, D_MODEL), mods, w_perm, cos_t, sin_t)


def _na_kernel(q_ref, k_ref, v_ref, bias_ref, o_ref, *, n_lat, n_ctx):
    r = pl.program_id(1)
    rows = n_lat // GRID_W
    wh = min(NA_WIN_H, rows)
    base = jnp.where(r < rows, jnp.clip(r - wh // 2, 0, rows - wh), 0)
    start = pl.multiple_of(base * GRID_W, GRID_W)
    n_win = wh * GRID_W
    q = q_ref[...]
    kw = k_ref[pl.ds(start, n_win), :]
    vw = v_ref[pl.ds(start, n_win), :]
    kc = k_ref[n_lat:n_lat + n_ctx, :]
    vc = v_ref[n_lat:n_lat + n_ctx, :]
    outs = []
    for h in range(GROUP_HEADS):
        sl = slice(h * HEAD_DIM, (h + 1) * HEAD_DIM)
        sw = _dot_nt(q[:, sl], kw[:, sl]) + bias_ref[h]
        sc = _dot_nt(q[:, sl], kc[:, sl])
        m = jnp.maximum(jnp.max(sw, axis=-1, keepdims=True), jnp.max(sc, axis=-1, keepdims=True))
        pw = jnp.exp(sw - m)
        pc = jnp.exp(sc - m)
        l = jnp.sum(pw, axis=-1, keepdims=True) + jnp.sum(pc, axis=-1, keepdims=True)
        o = _dot(pw.astype(BF16), vw[:, sl]) + _dot(pc.astype(BF16), vc[:, sl])
        outs.append(o / l)
    o_ref[...] = jnp.concatenate(outs, axis=-1).astype(BF16)


def _na_attention(pbf, bias_tab, n_batch, n_lat, n_ctx):
    n_tok = n_lat + n_ctx
    rows = n_lat // GRID_W
    wh = min(NA_WIN_H, rows)
    n_steps = n_tok // GRID_W

    def bias_idx(b, r):
        lat = jnp.clip(r - wh // 2, 0, rows - wh) - r + NA_WIN_H - 1
        return (jnp.where(r < rows, lat, NA_WIN_H), 0, 0, 0)

    return pl.pallas_call(
        functools.partial(_na_kernel, n_lat=n_lat, n_ctx=n_ctx),
        grid=(n_batch, n_steps),
        in_specs=[
            pl.BlockSpec((None, GRID_W, GROUP_W), lambda b, r: (b, r, 0)),
            pl.BlockSpec((None, n_tok, GROUP_W), lambda b, r: (b, 0, 1)),
            pl.BlockSpec((None, n_tok, GROUP_W), lambda b, r: (b, 0, 2)),
            pl.BlockSpec((None, GROUP_HEADS, GRID_W, wh * GRID_W), bias_idx),
        ],
        out_specs=pl.BlockSpec((None, GRID_W, GROUP_W), lambda b, r: (b, r, 0)),
        out_shape=jax.ShapeDtypeStruct((n_batch, n_tok, GROUP_W), BF16),
        compiler_params=_cparams("parallel", "arbitrary"),
        name="na_attention",
    )(pbf, pbf, pbf, bias_tab)


def _na_bias_table(rpb):
    wh = NA_WIN_H
    cq = np.arange(GRID_W)
    c0 = np.clip(cq - NA_WIN_W // 2, 0, GRID_W - NA_WIN_W)
    in_win = (cq[None, :] >= c0[:, None]) & (cq[None, :] < c0[:, None] + NA_WIN_W)
    dx = np.clip(cq[None, :] - cq[:, None], 1 - NA_WIN_W, NA_WIN_W - 1) + NA_WIN_W - 1
    dy = np.arange(wh)[:, None] + np.arange(wh)[None, :]
    tab = rpb.astype(F32)[:, dy[:, None, :, None], dx[None, :, None, :]]
    tab = jnp.where(in_win[None, None, :, None, :], tab, NEG)
    tab = jnp.transpose(tab, (1, 0, 2, 3, 4)).reshape(wh, GROUP_HEADS, GRID_W, wh * GRID_W)
    return jnp.concatenate([tab, jnp.full((1,) + tab.shape[1:], NEG, F32)], axis=0)


DF_KV = 256


def _df_kernel(lam_ref, q_ref, k_ref, v_ref, g_ref, o_ref, s1_ref, s2_ref, *, n_tok, lam_init):
    i = pl.program_id(1)
    n_chunks = n_tok // DF_KV
    ctx_tile = n_chunks - 1
    lo = jnp.where(i == ctx_tile, ctx_tile, 0)
    lq1, lk1, lq2, lk2 = lam_ref[0:1, :], lam_ref[1:2, :], lam_ref[2:3, :], lam_ref[3:4, :]
    lam = (jnp.exp(jnp.sum(lq1 * lk1, axis=-1, keepdims=True))
           - jnp.exp(jnp.sum(lq2 * lk2, axis=-1, keepdims=True)) + lam_init)
    q = q_ref[...]
    tq = q.shape[0]
    half = DF_KV // 2
    outs = []
    for h in range(GROUP_HEADS):
        c0 = h * HEAD_DIM
        q1 = q[:, c0:c0 + DIFF_DIM]
        q2 = q[:, c0 + DIFF_DIM:c0 + 2 * DIFF_DIM]

        def scores(c, carry):
            m1, m2 = carry
            off = pl.multiple_of(c * DF_KV, DF_KV)
            kc = k_ref[pl.ds(off, DF_KV), :]
            s1 = _dot_nt(q1, kc[:, c0:c0 + DIFF_DIM])
            s2 = _dot_nt(q2, kc[:, c0 + DIFF_DIM:c0 + 2 * DIFF_DIM])
            s1_ref[c] = s1
            s2_ref[c] = s2
            m1 = jnp.maximum(m1, jnp.maximum(s1[:, :half], s1[:, half:]))
            m2 = jnp.maximum(m2, jnp.maximum(s2[:, :half], s2[:, half:]))
            return m1, m2

        minit = jnp.full((tq, half), NEG, F32)
        m1, m2 = lax.fori_loop(lo, n_chunks, scores, (minit, minit))
        m1 = jnp.max(m1, axis=-1, keepdims=True)
        m2 = jnp.max(m2, axis=-1, keepdims=True)

        def expsum(c, carry):
            l1, l2 = carry
            e1 = jnp.exp(s1_ref[c] - m1)
            e2 = jnp.exp(s2_ref[c] - m2)
            s1_ref[c] = e1
            s2_ref[c] = e2
            return l1 + (e1[:, :half] + e1[:, half:]), l2 + (e2[:, :half] + e2[:, half:])

        zinit = jnp.zeros((tq, half), F32)
        l1, l2 = lax.fori_loop(lo, n_chunks, expsum, (zinit, zinit))
        r1 = 1.0 / jnp.sum(l1, axis=-1, keepdims=True)
        r2 = lam / jnp.sum(l2, axis=-1, keepdims=True)

        def pv(c, acc):
            off = pl.multiple_of(c * DF_KV, DF_KV)
            a = (s1_ref[c] * r1 - s2_ref[c] * r2).astype(BF16)
            return acc + _dot(a, v_ref[pl.ds(off, DF_KV), :][:, c0:c0 + HEAD_DIM])

        o = lax.fori_loop(lo, n_chunks, pv, jnp.zeros((tq, HEAD_DIM), F32))
        o = o * lax.rsqrt(jnp.mean(o * o, axis=-1, keepdims=True) + RMS_EPS)
        outs.append(o)
    o_ref[...] = (jnp.concatenate(outs, axis=-1) * g_ref[...] * (1.0 - lam_init)).astype(BF16)


def _df_attention(pbf, df_lambda, g_tiled, lam_init, n_batch, n_tok):
    nt = n_tok // ROW_TILE
    n_chunks = n_tok // DF_KV
    return pl.pallas_call(
        functools.partial(_df_kernel, n_tok=n_tok, lam_init=lam_init),
        grid=(n_batch, nt),
        in_specs=[
            pl.BlockSpec((4, DIFF_DIM), lambda b, i: (0, 0)),
            pl.BlockSpec((None, ROW_TILE, GROUP_W), lambda b, i: (b, i, 3)),
            pl.BlockSpec((None, n_tok, GROUP_W), lambda b, i: (b, 0, 4)),
            pl.BlockSpec((None, n_tok, GROUP_W), lambda b, i: (b, 0, 5)),
            pl.BlockSpec((1, GROUP_W), lambda b, i: (0, 0)),
        ],
        out_specs=pl.BlockSpec((None, ROW_TILE, GROUP_W), lambda b, i: (b, i, 0)),
        out_shape=jax.ShapeDtypeStruct((n_batch, n_tok, GROUP_W), BF16),
        scratch_shapes=[pltpu.VMEM((n_chunks, ROW_TILE, DF_KV), F32),
                        pltpu.VMEM((n_chunks, ROW_TILE, DF_KV), F32)],
        compiler_params=_cparams("parallel", "arbitrary"),
        name="df_attention",
    )(df_lambda, pbf, pbf, pbf, g_tiled)


CONV_PAD = 8


def _dn_prep_kernel(u_ref, w_ref, o_ref, pad_ref, *, n_lat, n_ctx):
    j = pl.program_id(1)
    zeros = jnp.zeros((CONV_PAD, GROUP_W), F32)
    lat0 = CONV_PAD
    ctx0 = 2 * CONV_PAD + n_lat
    pad_ref[0:CONV_PAD, :] = zeros
    pad_ref[lat0 + n_lat:ctx0, :] = zeros
    pad_ref[ctx0 + n_ctx:ctx0 + n_ctx + CONV_PAD, :] = zeros
    pad_ref[lat0:lat0 + n_lat, :] = u_ref[0:n_lat, :]
    pad_ref[ctx0:ctx0 + n_ctx, :] = u_ref[n_lat:n_lat + n_ctx, :]
    qscale = jnp.where(j == 0, HEAD_DIM ** -0.5, 1.0)
    half = DN_CONV_K // 2
    for (src0, dst0, n) in ((lat0, 0, n_lat), (ctx0, n_lat, n_ctx)):
        for t0 in range(0, n, ROW_TILE):
            acc = jnp.zeros((ROW_TILE, GROUP_W), F32)
            for tap in range(DN_CONV_K):
                a = src0 + t0 + tap - half
                acc = acc + pad_ref[a:a + ROW_TILE, :] * w_ref[tap:tap + 1, :]
            y = _silu(acc)
            parts = []
            for h in range(GROUP_HEADS):
                yh = y[:, h * HEAD_DIM:(h + 1) * HEAD_DIM]
                nrm = lax.rsqrt(jnp.sum(yh * yh, axis=-1, keepdims=True) + 1e-6) * qscale
                parts.append(yh * jnp.where(j == 2, 1.0, nrm))
            o_ref[dst0 + t0:dst0 + t0 + ROW_TILE, :] = jnp.concatenate(parts, axis=-1)


def _dn_prep(pf, conv_w, n_batch, n_lat, n_ctx):
    n_tok = n_lat + n_ctx
    return pl.pallas_call(
        functools.partial(_dn_prep_kernel, n_lat=n_lat, n_ctx=n_ctx),
        grid=(n_batch, 3),
        in_specs=[
            pl.BlockSpec((None, n_tok, GROUP_W), lambda b, j: (b, 0, j)),
            pl.BlockSpec((DN_CONV_K, GROUP_W), lambda b, j: (0, j)),
        ],
        out_specs=pl.BlockSpec((None, n_tok, GROUP_W), lambda b, j: (b, 0, j)),
        out_shape=jax.ShapeDtypeStruct((n_batch, n_tok, 3 * GROUP_W), F32),
        scratch_shapes=[pltpu.VMEM((n_tok + 3 * CONV_PAD, GROUP_W), F32)],
        compiler_params=_cparams("parallel", "arbitrary"),
        name="dn_prep",
    )(pf, conv_w)


def _tri_inverse(lm, eye):
    p = eye - lm
    sq = lm
    n = 2
    while n < DN_CHUNK:
        sq = _dot3(sq, sq)
        p = p + _dot3(p, sq)
        n *= 2
    return p


def _dn_kernel(qf_ref, kf_ref, vf_ref, abf_ref, qb_ref, kb_ref, vb_ref, abb_ref, av_ref, dt_ref,
               of_ref, ob_ref, s_ref):
    n = pl.program_id(1)

    @pl.when(n == 0)
    def _():
        s_ref[...] = jnp.zeros_like(s_ref)

    c = DN_CHUNK
    ri = lax.broadcasted_iota(jnp.int32, (c, c), 0)
    ci = lax.broadcasted_iota(jnp.int32, (c, c), 1)
    eye = (ri == ci).astype(F32)
    lane = lax.broadcasted_iota(jnp.int32, (c, 128), 1)
    for d, (q_ref, k_ref, v_ref, ab_ref, o_ref) in enumerate(
            ((qf_ref, kf_ref, vf_ref, abf_ref, of_ref), (qb_ref, kb_ref, vb_ref, abb_ref, ob_ref))):
        incl = (ri >= ci) if d == 0 else (ri <= ci)
        strict = (ri > ci) if d == 0 else (ri < ci)
        ab = ab_ref[...]
        x = ab + dt_ref[...]
        softplus = jnp.maximum(x, 0.0) + jnp.log1p(jnp.exp(-jnp.abs(x)))
        g = -jnp.exp(av_ref[...]) * softplus
        g1 = g.astype(BF16)
        g2 = (g - g1.astype(F32)).astype(BF16)
        g3 = (g - g1.astype(F32) - g2.astype(F32)).astype(BF16)
        tri = incl.astype(BF16)
        gcs = _dot(tri, g1) + (_dot(tri, g2) + _dot(tri, g3))
        beta_all = jax.nn.sigmoid(ab)
        q, k, v = q_ref[...], k_ref[...], v_ref[...]
        last = c - 1 if d == 0 else 0
        outs = []
        for h in range(GROUP_HEADS):
            col = d * GROUP_HEADS + h
            sl = slice(h * HEAD_DIM, (h + 1) * HEAD_DIM)
            gc = gcs[:, col:col + 1]
            beta = beta_all[:, GROUP_HEADS * 2 + col:GROUP_HEADS * 2 + col + 1]
            gh1 = gc.astype(BF16).astype(F32)
            gh2 = (gc - gh1).astype(BF16).astype(F32)
            gh3 = (gc - gh1 - gh2).astype(BF16).astype(F32)
            p_l = jnp.where(lane == 0, gh1, jnp.where(lane == 1, gh2, gh3))
            p_r = jnp.where(lane == 3, gh1, jnp.where(lane == 4, gh2, gh3))
            ones = jnp.where(lane < 6, 1.0, 0.0)
            left = jnp.where(lane < 3, p_l, ones).astype(BF16)
            right = jnp.where(lane < 3, ones, jnp.where(lane < 6, -p_r, 0.0)).astype(BF16)
            diff = _dot_nt(left, right)
            decay = jnp.exp(jnp.where(incl, diff, NEG))
            qh, kh, vh = q[:, sl], k[:, sl], v[:, sl]
            kbeta = kh * beta
            kk = _dot_nt(kbeta.astype(BF16), kh.astype(BF16))
            lm = jnp.where(strict, kk * decay, 0.0)
            t_inv = _tri_inverse(lm, eye)
            eg = jnp.exp(gc)
            uw = _dot(t_inv.astype(BF16), jnp.concatenate([vh * beta, kbeta * eg], axis=-1).astype(BF16))
            u, w = uw[:, :HEAD_DIM], uw[:, HEAD_DIM:]
            qk = jnp.where(incl, _dot_nt(qh.astype(BF16), kh.astype(BF16)) * decay, 0.0)
            g_last = gc[last:last + 1, :]
            k_dec = kh * jnp.exp(g_last - gc)
            s = s_ref[col]
            ws = _dot(jnp.concatenate([w, qh * eg], axis=0).astype(BF16), s.astype(BF16))
            v_new = u - ws[:c]
            outs.append(ws[c:] + _dot(qk.astype(BF16), v_new.astype(BF16)))
            s_ref[col] = s * jnp.exp(g_last) + _dot_tn(k_dec.astype(BF16), v_new.astype(BF16))
        o_ref[...] = jnp.concatenate(outs, axis=-1)


def _dn_scan(qkv, pf, a_vec, dt_vec, n_batch, n_lat, n_ctx):
    n_tok = n_lat + n_ctx
    nc_lat, nc_ctx = n_lat // DN_CHUNK, n_ctx // DN_CHUNK
    nc = nc_lat + nc_ctx
    ab_blk = 4 * GROUP_W // 128

    def fwd(n):
        return jnp.where(n < nc_ctx, nc_lat + n, n - nc_ctx)

    def bwd(n):
        return nc - 1 - n

    def spec(order, blk, width):
        return pl.BlockSpec((None, DN_CHUNK, width), lambda b, n: (b, order(n), blk))

    lane = pl.BlockSpec((1, 128), lambda b, n: (0, 0))
    out = jax.ShapeDtypeStruct((n_batch, n_tok, GROUP_W), F32)
    return pl.pallas_call(
        _dn_kernel,
        grid=(n_batch, nc),
        in_specs=[spec(fwd, 0, GROUP_W), spec(fwd, 1, GROUP_W), spec(fwd, 2, GROUP_W), spec(fwd, ab_blk, 128),
                  spec(bwd, 0, GROUP_W), spec(bwd, 1, GROUP_W), spec(bwd, 2, GROUP_W), spec(bwd, ab_blk, 128),
                  lane, lane],
        out_specs=[spec(fwd, 0, GROUP_W), spec(bwd, 0, GROUP_W)],
        out_shape=[out, out],
        scratch_shapes=[pltpu.VMEM((2 * GROUP_HEADS, HEAD_DIM, HEAD_DIM), F32)],
        compiler_params=_cparams("parallel", "arbitrary"),
        name="dn_scan",
    )(qkv, qkv, qkv, pf, qkv, qkv, qkv, pf, a_vec, dt_vec)


FT_N2 = 64


def _ft1_kernel(x_ref, fh_ref, fl_ref, o_ref):
    o_ref[...] = _dot3_const(fh_ref[...], fl_ref[...], x_ref[...])


def _ft_stage1(u_rows, f_hi, f_lo, n_batch, n1, tn=2048):
    width = u_rows.shape[2]
    return pl.pallas_call(
        _ft1_kernel,
        grid=(n_batch, width // tn),
        in_specs=[
            pl.BlockSpec((None, n1, tn), lambda b, j: (b, 0, j)),
            pl.BlockSpec((2 * n1, n1), lambda b, j: (0, 0)),
            pl.BlockSpec((2 * n1, n1), lambda b, j: (0, 0)),
        ],
        out_specs=pl.BlockSpec((None, 2 * n1, tn), lambda b, j: (b, 0, j)),
        out_shape=jax.ShapeDtypeStruct((n_batch, 2 * n1, width), F32),
        compiler_params=_cparams("parallel", "parallel"),
        name="ft_stage1",
    )(u_rows, f_hi, f_lo)


def _ft3_kernel(y_ref, gh_ref, gl_ref, ch_ref, cl_ref, sh_ref, sl_ref, w_ref, o_ref, *, pb, n2, norm):
    vr, vi = [], []
    for p in range(pb):
        rhs = jnp.concatenate([y_ref[0, p], y_ref[1, p]], axis=0)
        v = _dot3_const(gh_ref[p], gl_ref[p], rhs)
        vr.append(v[:n2])
        vi.append(v[n2:])
    vr = jnp.concatenate(vr, axis=0)
    vi = jnp.concatenate(vi, axis=0)
    vrh, vrl = _split(vr)
    vih, vil = _split(vi)
    y = (_dot(vrh, ch_ref[...]) + (_dot(vrl, ch_ref[...]) + _dot(vrh, cl_ref[...]))
         + _dot(vih, sh_ref[...]) + (_dot(vil, sh_ref[...]) + _dot(vih, sl_ref[...]))) * norm
    o = _dot(y.astype(BF16), w_ref[...])
    o_ref[...] = o.reshape(pb, n2, GROUP_W).astype(BF16)


def _ft_stage2(y1, g_hi, g_lo, chan, ft_w, n_batch, n1, n2, pb):
    ch, cl, sh, sl = chan
    full = lambda b, i: (0, 0)
    return pl.pallas_call(
        functools.partial(_ft3_kernel, pb=pb, n2=n2, norm=1.0 / math.sqrt(n1 * n2 * FT_DIM)),
        grid=(n_batch, n1 // pb),
        in_specs=[
            pl.BlockSpec((None, 2, pb, n2, GROUP_W), lambda b, i: (b, 0, i, 0, 0)),
            pl.BlockSpec((pb, 2 * n2, 2 * n2), lambda b, i: (i, 0, 0)),
            pl.BlockSpec((pb, 2 * n2, 2 * n2), lambda b, i: (i, 0, 0)),
            pl.BlockSpec((GROUP_W, GROUP_W), full), pl.BlockSpec((GROUP_W, GROUP_W), full),
            pl.BlockSpec((GROUP_W, GROUP_W), full), pl.BlockSpec((GROUP_W, GROUP_W), full),
            pl.BlockSpec((GROUP_W, GROUP_W), full),
        ],
        out_specs=pl.BlockSpec((None, pb, n2, GROUP_W), lambda b, i: (b, i, 0, 0)),
        out_shape=jax.ShapeDtypeStruct((n_batch, n1, n2, GROUP_W), BF16),
        compiler_params=_cparams("parallel", "parallel"),
        name="ft_stage2",
    )(y1, g_hi, g_lo, ch, cl, sh, sl, ft_w)


def _np_split(a):
    hi = jnp.asarray(a, F32).astype(BF16)
    lo = (jnp.asarray(a, F32) - hi.astype(F32)).astype(BF16)
    return hi, lo


@functools.lru_cache(maxsize=None)
def _ft_tables(n1, n2):
    n = n1 * n2
    p = np.arange(n1)
    ang1 = 2.0 * np.pi * ((p[:, None] * p[None, :]) % n1) / n1
    f1 = np.concatenate([np.cos(ang1), -np.sin(ang1)], axis=0)
    q = np.arange(n2)
    b = np.arange(n2)
    phase = (b[None, None, :] * (p[:, None, None] + n1 * q[None, :, None])) % n
    psi = 2.0 * np.pi * phase / n
    gc, gs = np.cos(psi), np.sin(psi)
    g = np.concatenate([np.concatenate([gc, gs], axis=2), np.concatenate([-gs, gc], axis=2)], axis=1)
    c = np.arange(FT_DIM)
    angc = 2.0 * np.pi * ((c[:, None] * c[None, :]) % FT_DIM) / FT_DIM
    eye = np.eye(GROUP_W // FT_DIM)
    cc, sc = np.kron(eye, np.cos(angc)), np.kron(eye, np.sin(angc))
    return f1.astype(np.float32), g.astype(np.float32), cc.astype(np.float32), sc.astype(np.float32)


def _out_kernel(x_ref, m_ref, na_ref, of_ref, ob_ref, gate_ref, dng_ref, df_ref, ft_ref, w_ref, g2_ref, wr_ref,
                xo_ref, h_ref, aff_ref):
    o = of_ref[...] + ob_ref[...]
    gate = gate_ref[...]
    parts = []
    for h in range(GROUP_HEADS):
        sl = slice(h * HEAD_DIM, (h + 1) * HEAD_DIM)
        oh = o[:, sl]
        parts.append(oh * lax.rsqrt(jnp.mean(oh * oh, axis=-1, keepdims=True) + RMS_EPS))
    y_dn = jnp.concatenate(parts, axis=-1) * dng_ref[...] * _silu(gate)
    mix = jnp.concatenate([na_ref[...], y_dn.astype(BF16), df_ref[...], ft_ref[...]], axis=-1)
    x = x_ref[...] + m_ref[2] * _dot(mix, w_ref[...])
    xo_ref[...] = x
    h2 = _norm_mod(x, g2_ref[...], m_ref[3], m_ref[4])
    h_ref[...] = h2.astype(BF16)
    logits = _dot3(h2, wr_ref[...])
    lane = lax.broadcasted_iota(jnp.int32, logits.shape, 1)
    logits = jnp.where(lane < N_EXPERTS, logits, NEG)
    e = jnp.exp(logits - jnp.max(logits, axis=-1, keepdims=True))
    aff_ref[...] = e / jnp.sum(e, axis=-1, keepdims=True)


def _out_proj(x, mods, o_na, o_f, o_b, pf, dn_g, o_df, o_ft, w_out, g2, w_router, n_batch, n_tok):
    nt = n_tok // ROW_TILE
    ctx_tile = nt - 1
    row = lambda b, i: (b, i, 0)
    blk = lambda: pl.BlockSpec((None, ROW_TILE, GROUP_W), row)
    vec = lambda n: pl.BlockSpec((1, n), lambda b, i: (0, 0))
    return pl.pallas_call(
        _out_kernel,
        grid=(n_batch, nt),
        in_specs=[
            pl.BlockSpec((None, ROW_TILE, D_MODEL), row),
            pl.BlockSpec((None, 6, 1, D_MODEL), lambda b, i: (jnp.where(i == ctx_tile, n_batch, b), 0, 0, 0)),
            blk(), blk(), blk(),
            pl.BlockSpec((None, ROW_TILE, GROUP_W), lambda b, i: (b, i, 3)),
            vec(GROUP_W), blk(), blk(),
            pl.BlockSpec((D_MODEL, D_MODEL), lambda b, i: (0, 0)),
            vec(D_MODEL),
            pl.BlockSpec((D_MODEL, 128), lambda b, i: (0, 0)),
        ],
        out_specs=[
            pl.BlockSpec((None, ROW_TILE, D_MODEL), row),
            pl.BlockSpec((None, ROW_TILE, D_MODEL), row),
            pl.BlockSpec((None, ROW_TILE, 128), row),
        ],
        out_shape=[
            jax.ShapeDtypeStruct((n_batch, n_tok, D_MODEL), F32),
            jax.ShapeDtypeStruct((n_batch, n_tok, D_MODEL), BF16),
            jax.ShapeDtypeStruct((n_batch, n_tok, 128), F32),
        ],
        compiler_params=_cparams("parallel", "parallel"),
        name="out_proj_router",
    )(x, mods, o_na, o_f, o_b, pf, dn_g, o_df, o_ft, w_out, g2.reshape(1, D_MODEL), w_router)


MOE_TF = 256


def _moe_kernel(x_ref, wg_ref, wu_ref, wd_ref, o_ref):
    f = pl.program_id(1)
    x = x_ref[...]
    a = _dot(x, wg_ref[...].astype(BF16))
    u = _dot(x, wu_ref[...].astype(BF16))
    hid = (_silu(a) * u).astype(BF16)
    y = _dot(hid, wd_ref[...].astype(BF16))

    @pl.when(f == 0)
    def _():
        o_ref[...] = y

    @pl.when(f != 0)
    def _():
        o_ref[...] += y


def _moe_ffn(xg, w_gate, w_up, w_down):
    n_e, m, _ = xg.shape
    return pl.pallas_call(
        _moe_kernel,
        grid=(n_e, D_EXPERT // MOE_TF),
        in_specs=[
            pl.BlockSpec((None, m, D_MODEL), lambda e, f: (e, 0, 0)),
            pl.BlockSpec((None, D_MODEL, MOE_TF), lambda e, f: (e, 0, f)),
            pl.BlockSpec((None, D_MODEL, MOE_TF), lambda e, f: (e, 0, f)),
            pl.BlockSpec((None, MOE_TF, D_MODEL), lambda e, f: (e, f, 0)),
        ],
        out_specs=pl.BlockSpec((None, m, D_MODEL), lambda e, f: (e, 0, 0)),
        out_shape=jax.ShapeDtypeStruct((n_e, m, D_MODEL), F32),
        compiler_params=_cparams("parallel", "arbitrary"),
        name="moe_ffn",
    )(xg, w_gate, w_up, w_down)


def _final_kernel(x_ref, g_ref, o_ref):
    x = x_ref[...]
    o_ref[...] = x * lax.rsqrt(jnp.mean(x * x, axis=-1, keepdims=True) + RMS_EPS) * g_ref[...]


def _final_norm(x, g, n_batch, n_lat):
    row = lambda b, i: (b, i, 0)
    return pl.pallas_call(
        _final_kernel,
        grid=(n_batch, n_lat // ROW_TILE),
        in_specs=[pl.BlockSpec((None, ROW_TILE, D_MODEL), row), pl.BlockSpec((1, D_MODEL), lambda b, i: (0, 0))],
        out_specs=pl.BlockSpec((None, ROW_TILE, D_MODEL), row),
        out_shape=jax.ShapeDtypeStruct((n_batch, n_lat, D_MODEL), F32),
        compiler_params=_cparams("parallel", "parallel"),
        name="final_norm",
    )(x, g.reshape(1, D_MODEL))


def _in_column_order():
    gw = GROUP_W
    na = np.arange(0, 3 * gw)
    dn_qkv = np.arange(3 * gw, 6 * gw)
    dn_a = np.arange(6 * gw, 6 * gw + 2 * GROUP_HEADS)
    dn_b = dn_a + 2 * GROUP_HEADS
    o = 6 * gw + 4 * GROUP_HEADS
    dn_gate = np.arange(o, o + gw)
    df_q = np.arange(o + gw, o + 2 * gw)
    df_k = df_q + gw
    df_v = df_k + gw
    ft_u = df_v + gw
    half = np.concatenate([np.arange(0, DIFF_DIM, 2), np.arange(1, DIFF_DIM, 2)])
    pair = (np.arange(gw) // DIFF_DIM) * DIFF_DIM + np.tile(half, gw // DIFF_DIM)
    return np.concatenate([na, df_q[pair], df_k[pair], df_v, dn_qkv, dn_gate, ft_u, dn_a, dn_b])


@functools.lru_cache(maxsize=None)
def _rope_tables(n_lat, n_ctx):
    t = np.arange(n_lat)
    pos = np.stack([t // GRID_W, t % GRID_W], axis=-1).astype(np.float32)
    n_freq = DIFF_DIM // 4
    inv = (ROPE_BASE ** (-np.arange(n_freq, dtype=np.float32) / n_freq)).astype(np.float32)
    ang = (pos[:, :, None] * inv).reshape(n_lat, 2 * n_freq)
    lane = np.arange(GROUP_W)
    idx = lane % (DIFF_DIM // 2)
    sign = np.where((lane % DIFF_DIM) < DIFF_DIM // 2, -1.0, 1.0)
    cos = np.concatenate([np.cos(ang)[:, idx], np.ones((n_ctx, GROUP_W))], axis=0)
    sin = np.concatenate([np.sin(ang)[:, idx] * sign, np.zeros((n_ctx, GROUP_W))], axis=0)
    return cos.astype(np.float32), sin.astype(np.float32)


def _lane_vec(v, n=128):
    v = v.reshape(-1).astype(F32)
    return jnp.zeros((1, n), F32).at[0, :v.shape[0]].set(v)


def _route(aff, cap):
    gate, idx = lax.top_k(jnp.swapaxes(aff, 1, 2), cap)
    return gate, idx


def kernel(x, c, ctx, c_ctx, w_mod, b_mod, norm1_g, w_in, na_rpb, dn_conv_w, dn_a_log, dn_dt_bias, dn_norm_g, df_lambda, df_norm_g, ft_w, w_out, norm2_g, w_router, w_gate, w_up, w_down, final_norm_g):
    n_batch, n_lat, _ = x.shape
    n_ctx = ctx.shape[1]
    n_tok = n_lat + n_ctx
    depth = w_mod.shape[0]
    assert n_batch + 1 <= 8 and n_ctx == ROW_TILE and n_lat % ROW_TILE == 0

    xs = jnp.concatenate([x, ctx], axis=1)
    cc = jnp.zeros((8, D_MODEL), F32).at[:n_batch].set(c).at[n_batch].set(c_ctx)
    m_all = _modulation(cc, w_mod, b_mod)

    order = _in_column_order()
    cos_t, sin_t = (jnp.asarray(a) for a in _rope_tables(n_lat, n_ctx))
    n1 = n_lat // FT_N2
    f1, g_lat, cc_m, sc_m = _ft_tables(n1, FT_N2)
    _, g_ctx, _, _ = _ft_tables(1, n_ctx)
    f1h, f1l = _np_split(f1)
    glh, gll = _np_split(g_lat)
    gch, gcl = _np_split(g_ctx)
    chan = _np_split(cc_m) + _np_split(sc_m)
    cap_lat = EC_FACTOR * n_lat // N_EXPERTS
    cap_ctx = EC_FACTOR * n_ctx // N_EXPERTS
    pad_cols = N_BF + N_F32 + GROUP_W - order.shape[0]

    for l in range(depth):
        lam_init = 0.8 - 0.6 * math.exp(-0.3 * l)
        mods = m_all[l, :n_batch + 1].reshape(n_batch + 1, 6, 1, D_MODEL)
        w_perm = jnp.pad(w_in[l][:, order], ((0, 0), (0, pad_cols))).astype(BF16)
        pbf, pf, pu = _in_proj(xs, norm1_g[l], mods, w_perm, cos_t, sin_t, n_batch, n_tok)

        o_na = _na_attention(pbf, _na_bias_table(na_rpb[l]), n_batch, n_lat, n_ctx)

        qkv = _dn_prep(pf, dn_conv_w[l], n_batch, n_lat, n_ctx)
        o_f, o_b = _dn_scan(qkv, pf, _lane_vec(dn_a_log[l]), _lane_vec(dn_dt_bias[l]), n_batch, n_lat, n_ctx)

        o_df = _df_attention(pbf, df_lambda[l], jnp.tile(df_norm_g[l], GROUP_HEADS).reshape(1, GROUP_W),
                             lam_init, n_batch, n_tok)

        ftw = ft_w[l].astype(BF16)
        y1 = _ft_stage1(pu.reshape(n_batch, n_tok // FT_N2, FT_N2 * GROUP_W), f1h, f1l, n_batch, n1)
        o_lat = _ft_stage2(y1.reshape(n_batch, 2, n1, FT_N2, GROUP_W), glh, gll, chan, ftw, n_batch, n1, FT_N2, 8)
        o_lat = jnp.swapaxes(o_lat, 1, 2).reshape(n_batch, n_lat, GROUP_W)
        u_ctx = pu[:, n_lat:]
        y1c = jnp.stack([u_ctx, jnp.zeros_like(u_ctx)], axis=1)[:, :, None]
        o_ctx = _ft_stage2(y1c, gch, gcl, chan, ftw, n_batch, 1, n_ctx, 1).reshape(n_batch, n_ctx, GROUP_W)
        o_ft = jnp.concatenate([o_lat, o_ctx], axis=1)

        wr = jnp.pad(w_router[l], ((0, 0), (0, 128 - N_EXPERTS)))
        xs, h2, aff = _out_proj(xs, mods, o_na, o_f, o_b, pf, jnp.tile(dn_norm_g[l], GROUP_HEADS).reshape(1, GROUP_W),
                                o_df, o_ft, w_out[l].astype(BF16), norm2_g[l], wr, n_batch, n_tok)

        gate_l, idx_l = _route(aff[:, :n_lat, :N_EXPERTS], cap_lat)
        gate_c, idx_c = _route(aff[:, n_lat:, :N_EXPERTS], cap_ctx)
        idx = jnp.concatenate([idx_l, idx_c + n_lat], axis=2)
        gate = jnp.concatenate([gate_l, gate_c], axis=2)
        cap = cap_lat + cap_ctx
        xg = jax.vmap(lambda hb, ib: hb[ib])(h2, idx)
        xg = jnp.swapaxes(xg, 0, 1).reshape(N_EXPERTS, n_batch * cap, D_MODEL)
        y = _moe_ffn(xg, w_gate[l], w_up[l], w_down[l]).reshape(N_EXPERTS, n_batch, cap, D_MODEL)
        y = jnp.swapaxes(y, 0, 1) * gate[..., None]
        moe = jax.vmap(lambda yb, ib: jnp.zeros((n_tok, D_MODEL), F32).at[ib.reshape(-1)].add(yb.reshape(-1, D_MODEL)))(y, idx)
        g_mlp = jnp.concatenate([jnp.broadcast_to(m_all[l, :n_batch, None, 5 * D_MODEL:], (n_batch, n_lat, D_MODEL)),
                                 jnp.broadcast_to(m_all[l, n_batch, 5 * D_MODEL:], (n_batch, n_ctx, D_MODEL))], axis=1)
        xs = xs + g_mlp * moe

    return _final_norm(xs, final_norm_g, n_batch, n_lat)
```

```python
import functools
import math

import numpy as np
import jax
import jax.numpy as jnp
from jax import lax
from jax.experimental import pallas as pl
from jax.experimental.pallas import tpu as pltpu

F32 = jnp.float32
BF16 = jnp.bfloat16

D_MODEL = 1024
DEPTH = 4
GRID_W = 64
GROUP_W = 256
GROUP_HEADS = 4
HEAD_DIM = 64
NA_WIN_H = 8
NA_WIN_W = 16
DN_CONV_K = 5
DN_CHUNK = 64
DIFF_DIM = 32
FT_DIM = 64
N_EXPERTS = 16
EC_FACTOR = 2
D_EXPERT = 2 * D_MODEL
ROPE_BASE = 10000.0
RMS_EPS = 1e-6
NEG = -1e30

ROW_TILE = 256
N_BF = 6 * GROUP_W
N_F32 = 4 * GROUP_W + 128
VMEM_LIMIT = 56 * 1024 * 1024


def _cparams(*sem):
    return pltpu.CompilerParams(dimension_semantics=sem, vmem_limit_bytes=VMEM_LIMIT)


def _split(x):
    hi = x.astype(BF16)
    lo = (x - hi.astype(F32)).astype(BF16)
    return hi, lo


def _dot(a, b):
    return jnp.dot(a, b, preferred_element_type=F32)


def _dot_nt(a, b):
    return lax.dot_general(a, b, (((1,), (1,)), ((), ())), preferred_element_type=F32)


def _dot_tn(a, b):
    return lax.dot_general(a, b, (((0,), (0,)), ((), ())), preferred_element_type=F32)


def _dot3(a, b):
    ah, al = _split(a)
    bh, bl = _split(b)
    return _dot(ah, bh) + (_dot(al, bh) + _dot(ah, bl))


def _dot3_const(ah, al, b):
    bh, bl = _split(b)
    return _dot(ah, bh) + (_dot(al, bh) + _dot(ah, bl))


def _silu(x):
    return x * jax.nn.sigmoid(x)


def _mod_kernel(s_ref, w_ref, b_ref, o_ref):
    s = _silu(s_ref[...])
    o_ref[...] = _dot(s.astype(BF16), w_ref[...].astype(BF16)) + b_ref[...]


def _modulation(cc, w_mod, b_mod):
    depth = w_mod.shape[0]
    nt = 6 * D_MODEL // 1024
    return pl.pallas_call(
        _mod_kernel,
        grid=(depth, nt),
        in_specs=[
            pl.BlockSpec((8, D_MODEL), lambda l, j: (0, 0)),
            pl.BlockSpec((None, D_MODEL, 1024), lambda l, j: (l, 0, j)),
            pl.BlockSpec((None, 1, 1024), lambda l, j: (l, 0, j)),
        ],
        out_specs=pl.BlockSpec((None, 8, 1024), lambda l, j: (l, 0, j)),
        out_shape=jax.ShapeDtypeStruct((depth, 8, 6 * D_MODEL), F32),
        compiler_params=_cparams("parallel", "parallel"),
        name="modulation",
    )(cc, w_mod, b_mod.reshape(depth, 1, 6 * D_MODEL))


def _norm_mod(x, g, shift, scale):
    y = x * lax.rsqrt(jnp.mean(x * x, axis=-1, keepdims=True) + RMS_EPS) * g
    return y * (1.0 + scale) + shift


def _in_kernel(x_ref, g_ref, m_ref, w_ref, cos_ref, sin_ref, obf_ref, of_ref, ou_ref):
    h = _norm_mod(x_ref[...], g_ref[...], m_ref[0], m_ref[1])
    p = _dot(h.astype(BF16), w_ref[...])
    gw = GROUP_W
    lane = lax.broadcasted_iota(jnp.int32, (1, gw), 1)
    even = (lane % 2) == 0
    cos, sin = cos_ref[...], sin_ref[...]

    def rope(t):
        sw = jnp.where(even, pltpu.roll(t, gw - 1, 1), pltpu.roll(t, 1, 1))
        return t * cos + sw * sin

    obf_ref[:, 0:gw] = (p[:, 0:gw] * HEAD_DIM ** -0.5).astype(BF16)
    obf_ref[:, gw:3 * gw] = p[:, gw:3 * gw].astype(BF16)
    obf_ref[:, 3 * gw:4 * gw] = (rope(p[:, 3 * gw:4 * gw]) * DIFF_DIM ** -0.5).astype(BF16)
    obf_ref[:, 4 * gw:5 * gw] = rope(p[:, 4 * gw:5 * gw]).astype(BF16)
    obf_ref[:, 5 * gw:6 * gw] = p[:, 5 * gw:6 * gw].astype(BF16)
    of_ref[:, 0:4 * gw] = p[:, 6 * gw:10 * gw]
    of_ref[:, 4 * gw:4 * gw + 128] = p[:, 11 * gw:11 * gw + 128]
    ou_ref[...] = p[:, 10 * gw:11 * gw]


def _in_proj(x, g, mods, w_perm, cos_t, sin_t, n_batch, n_tok):
    nt = n_tok // ROW_TILE
    ctx_tile = nt - 1
    n_out = w_perm.shape[1]
    row = lambda b, i: (b, i, 0)
    return pl.pallas_call(
        _in_kernel,
        grid=(n_batch, nt),
        in_specs=[
            pl.BlockSpec((None, ROW_TILE, D_MODEL), row),
            pl.BlockSpec((1, D_MODEL), lambda b, i: (0, 0)),
            pl.BlockSpec((None, 6, 1, D_MODEL), lambda b, i: (jnp.where(i == ctx_tile, n_batch, b), 0, 0, 0)),
            pl.BlockSpec((D_MODEL, n_out), lambda b, i: (0, 0)),
            pl.BlockSpec((ROW_TILE, GROUP_W), lambda b, i: (i, 0)),
            pl.BlockSpec((ROW_TILE, GROUP_W), lambda b, i: (i, 0)),
        ],
        out_specs=[
            pl.BlockSpec((None, ROW_TILE, N_BF), row),
            pl.BlockSpec((None, ROW_TILE, N_F32), row),
            pl.BlockSpec((None, ROW_TILE, GROUP_W), row),
        ],
        out_shape=[
            jax.ShapeDtypeStruct((n_batch, n_tok, N_BF), BF16),
            jax.ShapeDtypeStruct((n_batch, n_tok, N_F32), F32),
            jax.ShapeDtypeStruct((n_batch, n_tok, GROUP_W), F32),
        ],
        compiler_params=_cparams("parallel", "parallel"),
        name="in_proj",
    )(x, g.reshape(1, D_MODEL), mods, w_perm, cos_t, sin_t)


def _na_kernel(q_ref, k_ref, v_ref, bias_ref, o_ref, *, n_lat, n_ctx):
    r = pl.program_id(1)
    rows = n_lat // GRID_W
    wh = min(NA_WIN_H, rows)
    base = jnp.where(r < rows, jnp.clip(r - wh // 2, 0, rows - wh), 0)
    start = pl.multiple_of(base * GRID_W, GRID_W)
    n_win = wh * GRID_W
    q = q_ref[...]
    kw = k_ref[pl.ds(start, n_win), :]
    vw = v_ref[pl.ds(start, n_win), :]
    kc = k_ref[n_lat:n_lat + n_ctx, :]
    vc = v_ref[n_lat:n_lat + n_ctx, :]
    outs = []
    for h in range(GROUP_HEADS):
        sl = slice(h * HEAD_DIM, (h + 1) * HEAD_DIM)
        sw = _dot_nt(q[:, sl], kw[:, sl]) + bias_ref[h]
        sc = _dot_nt(q[:, sl], kc[:, sl])
        m = jnp.maximum(jnp.max(sw, axis=-1, keepdims=True), jnp.max(sc, axis=-1, keepdims=True))
        pw = jnp.exp(sw - m)
        pc = jnp.exp(sc - m)
        l = jnp.sum(pw, axis=-1, keepdims=True) + jnp.sum(pc, axis=-1, keepdims=True)
        o = _dot(pw.astype(BF16), vw[:, sl]) + _dot(pc.astype(BF16), vc[:, sl])
        outs.append(o / l)
    o_ref[...] = jnp.concatenate(outs, axis=-1).astype(BF16)


def _na_attention(pbf, bias_tab, n_batch, n_lat, n_ctx):
    n_tok = n_lat + n_ctx
    rows = n_lat // GRID_W
    wh = min(NA_WIN_H, rows)
    n_steps = n_tok // GRID_W

    def bias_idx(b, r):
        lat = jnp.clip(r - wh // 2, 0, rows - wh) - r + NA_WIN_H - 1
        return (jnp.where(r < rows, lat, NA_WIN_H), 0, 0, 0)

    return pl.pallas_call(
        functools.partial(_na_kernel, n_lat=n_lat, n_ctx=n_ctx),
        grid=(n_batch, n_steps),
        in_specs=[
            pl.BlockSpec((None, GRID_W, GROUP_W), lambda b, r: (b, r, 0)),
            pl.BlockSpec((None, n_tok, GROUP_W), lambda b, r: (b, 0, 1)),
            pl.BlockSpec((None, n_tok, GROUP_W), lambda b, r: (b, 0, 2)),
            pl.BlockSpec((None, GROUP_HEADS, GRID_W, wh * GRID_W), bias_idx),
        ],
        out_specs=pl.BlockSpec((None, GRID_W, GROUP_W), lambda b, r: (b, r, 0)),
        out_shape=jax.ShapeDtypeStruct((n_batch, n_tok, GROUP_W), BF16),
        compiler_params=_cparams("parallel", "arbitrary"),
        name="na_attention",
    )(pbf, pbf, pbf, bias_tab)


def _na_bias_table(rpb):
    wh = NA_WIN_H
    cq = np.arange(GRID_W)
    c0 = np.clip(cq - NA_WIN_W // 2, 0, GRID_W - NA_WIN_W)
    in_win = (cq[None, :] >= c0[:, None]) & (cq[None, :] < c0[:, None] + NA_WIN_W)
    dx = np.clip(cq[None, :] - cq[:, None], 1 - NA_WIN_W, NA_WIN_W - 1) + NA_WIN_W - 1
    dy = np.arange(wh)[:, None] + np.arange(wh)[None, :]
    tab = rpb.astype(F32)[:, dy[:, None, :, None], dx[None, :, None, :]]
    tab = jnp.where(in_win[None, None, :, None, :], tab, NEG)
    tab = jnp.transpose(tab, (1, 0, 2, 3, 4)).reshape(wh, GROUP_HEADS, GRID_W, wh * GRID_W)
    return jnp.concatenate([tab, jnp.full((1,) + tab.shape[1:], NEG, F32)], axis=0)


DF_KV = 512


def _df_kernel(lam_ref, q_ref, k_ref, v_ref, g_ref, o_ref, s_ref, *, chunks, lam_init):
    lq1, lk1, lq2, lk2 = lam_ref[0:1, :], lam_ref[1:2, :], lam_ref[2:3, :], lam_ref[3:4, :]
    lam = (jnp.exp(jnp.sum(lq1 * lk1, axis=-1, keepdims=True))
           - jnp.exp(jnp.sum(lq2 * lk2, axis=-1, keepdims=True)) + lam_init)
    q = q_ref[...]
    tq, gw = q.shape
    lane = lax.broadcasted_iota(jnp.int32, (tq, gw), 1)
    row2 = lax.broadcasted_iota(jnp.int32, (2 * tq, 1), 0)
    origin = chunks[0][0]
    zero = jnp.zeros_like(q)

    def head(h, out):
        qm = jnp.concatenate([jnp.where(lane // DIFF_DIM == 2 * h, q, zero),
                              jnp.where(lane // DIFF_DIM == 2 * h + 1, q, zero)], axis=0)
        m = jnp.full((2 * tq, 128), NEG, F32)
        for st, sz in chunks:
            s = _dot_nt(qm, k_ref[st:st + sz, :])
            s_ref[:, st - origin:st - origin + sz] = s
            for t in range(sz // 128):
                m = jnp.maximum(m, s[:, t * 128:(t + 1) * 128])
        m = jnp.max(m, axis=-1, keepdims=True)
        l = jnp.zeros((2 * tq, 128), F32)
        for st, sz in chunks:
            e = jnp.exp(s_ref[:, st - origin:st - origin + sz] - m)
            s_ref[:, st - origin:st - origin + sz] = e
            for t in range(sz // 128):
                l = l + e[:, t * 128:(t + 1) * 128]
        r = jnp.where(row2 < tq, 1.0, lam) / jnp.sum(l, axis=-1, keepdims=True)
        acc = jnp.zeros((tq, gw), F32)
        for st, sz in chunks:
            sl = slice(st - origin, st - origin + sz)
            a = (s_ref[:tq, sl] * r[:tq] - s_ref[tq:, sl] * r[tq:]).astype(BF16)
            acc = acc + _dot(a, v_ref[st:st + sz, :])
        return out + jnp.where(lane // HEAD_DIM == h, acc, 0.0)

    o = lax.fori_loop(0, GROUP_HEADS, head, jnp.zeros((tq, gw), F32))
    sq = o * o
    scale = jnp.zeros((tq, gw), F32)
    for h in range(GROUP_HEADS):
        in_head = lane // HEAD_DIM == h
        ms = jnp.sum(jnp.where(in_head, sq, 0.0), axis=-1, keepdims=True) * (1.0 / HEAD_DIM)
        scale = jnp.where(in_head, lax.rsqrt(ms + RMS_EPS), scale)
    o_ref[...] = (o * scale * g_ref[...] * (1.0 - lam_init)).astype(BF16)


def _df_call(pbf, df_lambda, g_tiled, lam_init, n_batch, n_tok, first_tile, n_tiles, chunks, name):
    n_keys = sum(sz for _, sz in chunks)
    return pl.pallas_call(
        functools.partial(_df_kernel, chunks=chunks, lam_init=lam_init),
        grid=(n_batch, n_tiles),
        in_specs=[
            pl.BlockSpec((4, DIFF_DIM), lambda b, i: (0, 0)),
            pl.BlockSpec((None, ROW_TILE, GROUP_W), lambda b, i: (b, first_tile + i, 3)),
            pl.BlockSpec((None, n_tok, GROUP_W), lambda b, i: (b, 0, 4)),
            pl.BlockSpec((None, n_tok, GROUP_W), lambda b, i: (b, 0, 5)),
            pl.BlockSpec((1, GROUP_W), lambda b, i: (0, 0)),
        ],
        out_specs=pl.BlockSpec((None, ROW_TILE, GROUP_W), lambda b, i: (b, i, 0)),
        out_shape=jax.ShapeDtypeStruct((n_batch, n_tiles * ROW_TILE, GROUP_W), BF16),
        scratch_shapes=[pltpu.VMEM((2 * ROW_TILE, n_keys), F32)],
        compiler_params=_cparams("parallel", "arbitrary"),
        name=name,
    )(df_lambda, pbf, pbf, pbf, g_tiled)


def _df_attention(pbf, df_lambda, g_tiled, lam_init, n_batch, n_lat, n_ctx):
    n_tok = n_lat + n_ctx
    lat_chunks = tuple((st, min(DF_KV, n_tok - st)) for st in range(0, n_tok, DF_KV))
    o_lat = _df_call(pbf, df_lambda, g_tiled, lam_init, n_batch, n_tok, 0, n_lat // ROW_TILE, lat_chunks, "df_attention")
    o_ctx = _df_call(pbf, df_lambda, g_tiled, lam_init, n_batch, n_tok, n_lat // ROW_TILE, n_ctx // ROW_TILE,
                     ((n_lat, n_ctx),), "df_attention_ctx")
    return jnp.concatenate([o_lat, o_ctx], axis=1)


CONV_PAD = 8


def _dn_prep_kernel(u_ref, w_ref, o_ref, pad_ref, *, n_lat, n_ctx):
    j = pl.program_id(1)
    zeros = jnp.zeros((CONV_PAD, GROUP_W), F32)
    lat0 = CONV_PAD
    ctx0 = 2 * CONV_PAD + n_lat
    pad_ref[0:CONV_PAD, :] = zeros
    pad_ref[lat0 + n_lat:ctx0, :] = zeros
    pad_ref[ctx0 + n_ctx:ctx0 + n_ctx + CONV_PAD, :] = zeros
    pad_ref[lat0:lat0 + n_lat, :] = u_ref[0:n_lat, :]
    pad_ref[ctx0:ctx0 + n_ctx, :] = u_ref[n_lat:n_lat + n_ctx, :]
    qscale = jnp.where(j == 0, HEAD_DIM ** -0.5, 1.0)
    half = DN_CONV_K // 2
    for (src0, dst0, n) in ((lat0, 0, n_lat), (ctx0, n_lat, n_ctx)):
        for t0 in range(0, n, ROW_TILE):
            acc = jnp.zeros((ROW_TILE, GROUP_W), F32)
            for tap in range(DN_CONV_K):
                a = src0 + t0 + tap - half
                acc = acc + pad_ref[a:a + ROW_TILE, :] * w_ref[tap:tap + 1, :]
            y = _silu(acc)
            parts = []
            for h in range(GROUP_HEADS):
                yh = y[:, h * HEAD_DIM:(h + 1) * HEAD_DIM]
                nrm = lax.rsqrt(jnp.sum(yh * yh, axis=-1, keepdims=True) + 1e-6) * qscale
                parts.append(yh * jnp.where(j == 2, 1.0, nrm))
            o_ref[dst0 + t0:dst0 + t0 + ROW_TILE, :] = jnp.concatenate(parts, axis=-1)


def _dn_prep(pf, conv_w, n_batch, n_lat, n_ctx):
    n_tok = n_lat + n_ctx
    return pl.pallas_call(
        functools.partial(_dn_prep_kernel, n_lat=n_lat, n_ctx=n_ctx),
        grid=(n_batch, 3),
        in_specs=[
            pl.BlockSpec((None, n_tok, GROUP_W), lambda b, j: (b, 0, j)),
            pl.BlockSpec((DN_CONV_K, GROUP_W), lambda b, j: (0, j)),
        ],
        out_specs=pl.BlockSpec((None, n_tok, GROUP_W), lambda b, j: (b, 0, j)),
        out_shape=jax.ShapeDtypeStruct((n_batch, n_tok, 3 * GROUP_W), F32),
        scratch_shapes=[pltpu.VMEM((n_tok + 3 * CONV_PAD, GROUP_W), F32)],
        compiler_params=_cparams("parallel", "arbitrary"),
        name="dn_prep",
    )(pf, conv_w)


DN_BLOCK = 4 * DN_CHUNK


def _split3(x):
    x1 = x.astype(BF16)
    r = x - x1.astype(F32)
    x2 = r.astype(BF16)
    return x1, x2, (r - x2.astype(F32)).astype(BF16)


def _block_diag(x, masks):
    zero = jnp.zeros_like(x)
    return jnp.concatenate([jnp.where(m, x, zero) for m in masks], axis=0)


def _mm3_heads(lhs, rhs, masks):
    lh, ll = _split(lhs)
    rh, rl = _split(rhs)
    bh, bl = _block_diag(rh, masks), _block_diag(rl, masks)
    return _dot(jnp.concatenate([lh, ll, lh], axis=1), jnp.concatenate([bh, bh, bl], axis=0))


def _dn_kernel(qf_ref, kf_ref, vf_ref, abf_ref, qb_ref, kb_ref, vb_ref, abb_ref, av_ref, dt_ref, exp_ref,
               of_ref, ob_ref, s_ref):
    @pl.when(pl.program_id(1) == 0)
    def _():
        s_ref[...] = jnp.zeros_like(s_ref)

    c, nb, gw = DN_CHUNK, DN_BLOCK, GROUP_W
    n_sub = nb // c
    lane4 = lax.broadcasted_iota(jnp.int32, (c, gw), 1)
    masks = [(lane4 // HEAD_DIM) == h for h in range(GROUP_HEADS)]
    ri = lax.broadcasted_iota(jnp.int32, (c, gw), 0)
    cj = lane4 % HEAD_DIM
    eye = (ri == cj).astype(F32)
    bi = lax.broadcasted_iota(jnp.int32, (nb, nb), 0)
    bj = lax.broadcasted_iota(jnp.int32, (nb, nb), 1)
    same_chunk = (bi // c) == (bj // c)
    lane1 = lax.broadcasted_iota(jnp.int32, (nb, 128), 1)
    dirs = ((qf_ref, kf_ref, vf_ref, abf_ref), (qb_ref, kb_ref, vb_ref, abb_ref))

    chains = []
    for d, (q_ref, k_ref, v_ref, ab_ref) in enumerate(dirs):
        incl = (ri >= cj) if d == 0 else (ri <= cj)
        strict = (ri > cj) if d == 0 else (ri < cj)
        ab = ab_ref[...]
        x = ab + dt_ref[...]
        softplus = jnp.maximum(x, 0.0) + jnp.log1p(jnp.exp(-jnp.abs(x)))
        g = -jnp.exp(av_ref[...]) * softplus
        tri = (same_chunk & ((bi >= bj) if d == 0 else (bi <= bj))).astype(BF16)
        g1, g2, g3 = _split3(g)
        gcs = _dot(tri, g1) + (_dot(tri, g2) + _dot(tri, g3))
        x1, x2, x3 = _split3(jnp.where(lane1 < 2 * GROUP_HEADS, gcs, jax.nn.sigmoid(ab)))
        e = exp_ref[d]
        spread = _dot(x1, e) + (_dot(x2, e) + _dot(x3, e))
        q_all, k_all, v_all = q_ref[...], k_ref[...], v_ref[...]
        last = c - 1 if d == 0 else 0
        for j in range(n_sub):
            rows = slice(j * c, (j + 1) * c)
            gc, beta = spread[rows, :gw], spread[rows, gw:]
            q, k, v = q_all[rows], k_all[rows], v_all[rows]
            gc_row = jnp.sum(gc * eye, axis=0, keepdims=True)
            decay = jnp.exp(jnp.where(incl, gc - gc_row, NEG))
            kbeta = k * beta
            kq = _dot_nt(jnp.concatenate([kbeta, q], axis=0).astype(BF16), _block_diag(k.astype(BF16), masks))
            eg = jnp.exp(gc)
            g_last = gc[last:last + 1, :]
            chains.append(dict(
                lm=jnp.where(strict, kq[:c] * decay, 0.0),
                qk=jnp.where(incl, kq[c:] * decay, 0.0).astype(BF16),
                rhs=jnp.concatenate([_block_diag((v * beta).astype(BF16), masks),
                                     _block_diag((kbeta * eg).astype(BF16), masks)], axis=1),
                q_dec=q * eg,
                k_dec=(k * jnp.exp(g_last - gc)).astype(BF16),
                g_end=jnp.exp(g_last)))

    for ch in chains:
        ch["p"] = eye - ch["lm"]
        ch["sq"] = _mm3_heads(ch["lm"], ch["lm"], masks)
    n = 4
    while n < c:
        for ch in chains:
            r = _mm3_heads(jnp.concatenate([ch["sq"], ch["p"]], axis=0), ch["sq"], masks)
            ch["sq"], ch["p"] = r[:c], ch["p"] + r[c:]
        n *= 2
    for ch in chains:
        t_inv = ch["p"] + _mm3_heads(ch["p"], ch["sq"], masks)
        uw = _dot(t_inv.astype(BF16), ch["rhs"])
        ch["u"] = uw[:, :gw]
        ch["wq"] = jnp.concatenate([uw[:, gw:], ch["q_dec"]], axis=0).astype(BF16)

    states = [s_ref[0], s_ref[1]]
    outs = [[None] * n_sub, [None] * n_sub]
    for step in range(n_sub):
        for d in range(2):
            j = step if d == 0 else n_sub - 1 - step
            ch = chains[d * n_sub + j]
            s = states[d]
            ws = _dot(ch["wq"], _block_diag(s.astype(BF16), masks))
            vb = (ch["u"] - ws[:c]).astype(BF16)
            outs[d][j] = ws[c:] + _dot(ch["qk"], _block_diag(vb, masks))
            full = _dot_tn(ch["k_dec"], vb)
            upd = jnp.where(masks[0], full[0:c], 0.0)
            for h in range(1, GROUP_HEADS):
                upd = upd + jnp.where(masks[h], full[h * c:(h + 1) * c], 0.0)
            states[d] = s * ch["g_end"] + upd
    of_ref[...] = jnp.concatenate(outs[0], axis=0)
    ob_ref[...] = jnp.concatenate(outs[1], axis=0)
    s_ref[...] = jnp.stack(states, axis=0)


def _dn_scan(qkv, pf, a_vec, dt_vec, n_batch, n_lat, n_ctx):
    n_tok = n_lat + n_ctx
    assert n_ctx == DN_BLOCK and n_lat % DN_BLOCK == 0
    nblk = n_tok // DN_BLOCK
    ab_blk = 4 * GROUP_W // 128

    def fwd(n):
        return jnp.where(n == 0, nblk - 1, n - 1)

    def bwd(n):
        return jnp.where(n == 0, nblk - 1, nblk - 1 - n)

    def spec(order, blk, width):
        return pl.BlockSpec((None, DN_BLOCK, width), lambda b, n: (b, order(n), blk))

    spread = np.zeros((2, 128, 2 * GROUP_W), np.float32)
    for d in range(2):
        for h in range(GROUP_HEADS):
            spread[d, GROUP_HEADS * d + h, h * HEAD_DIM:(h + 1) * HEAD_DIM] = 1.0
            spread[d, 2 * GROUP_HEADS + GROUP_HEADS * d + h, GROUP_W + h * HEAD_DIM:GROUP_W + (h + 1) * HEAD_DIM] = 1.0

    lane = pl.BlockSpec((1, 128), lambda b, n: (0, 0))
    out = jax.ShapeDtypeStruct((n_batch, n_tok, GROUP_W), F32)
    return pl.pallas_call(
        _dn_kernel,
        grid=(n_batch, nblk),
        in_specs=[spec(fwd, 0, GROUP_W), spec(fwd, 1, GROUP_W), spec(fwd, 2, GROUP_W), spec(fwd, ab_blk, 128),
                  spec(bwd, 0, GROUP_W), spec(bwd, 1, GROUP_W), spec(bwd, 2, GROUP_W), spec(bwd, ab_blk, 128),
                  lane, lane, pl.BlockSpec((2, 128, 2 * GROUP_W), lambda b, n: (0, 0, 0))],
        out_specs=[spec(fwd, 0, GROUP_W), spec(bwd, 0, GROUP_W)],
        out_shape=[out, out],
        scratch_shapes=[pltpu.VMEM((2, HEAD_DIM, GROUP_W), F32)],
        compiler_params=_cparams("parallel", "arbitrary"),
        name="dn_scan",
    )(qkv, qkv, qkv, pf, qkv, qkv, qkv, pf, a_vec, dt_vec, jnp.asarray(spread, BF16))


FT_N2 = 64


def _ft1_kernel(x_ref, fh_ref, fl_ref, o_ref):
    o_ref[...] = _dot3_const(fh_ref[...], fl_ref[...], x_ref[...])


def _ft_stage1(u_rows, f_hi, f_lo, n_batch, n1, tn=2048):
    width = u_rows.shape[2]
    return pl.pallas_call(
        _ft1_kernel,
        grid=(n_batch, width // tn),
        in_specs=[
            pl.BlockSpec((None, n1, tn), lambda b, j: (b, 0, j)),
            pl.BlockSpec((2 * n1, n1), lambda b, j: (0, 0)),
            pl.BlockSpec((2 * n1, n1), lambda b, j: (0, 0)),
        ],
        out_specs=pl.BlockSpec((None, 2 * n1, tn), lambda b, j: (b, 0, j)),
        out_shape=jax.ShapeDtypeStruct((n_batch, 2 * n1, width), F32),
        compiler_params=_cparams("parallel", "parallel"),
        name="ft_stage1",
    )(u_rows, f_hi, f_lo)


def _ft3_kernel(y_ref, gh_ref, gl_ref, ch_ref, cl_ref, sh_ref, sl_ref, w_ref, o_ref, *, pb, n2, norm):
    vr, vi = [], []
    for p in range(pb):
        rhs = jnp.concatenate([y_ref[0, p], y_ref[1, p]], axis=0)
        v = _dot3_const(gh_ref[p], gl_ref[p], rhs)
        vr.append(v[:n2])
        vi.append(v[n2:])
    vr = jnp.concatenate(vr, axis=0)
    vi = jnp.concatenate(vi, axis=0)
    vrh, vrl = _split(vr)
    vih, vil = _split(vi)
    y = (_dot(vrh, ch_ref[...]) + (_dot(vrl, ch_ref[...]) + _dot(vrh, cl_ref[...]))
         + _dot(vih, sh_ref[...]) + (_dot(vil, sh_ref[...]) + _dot(vih, sl_ref[...]))) * norm
    o = _dot(y.astype(BF16), w_ref[...])
    o_ref[...] = o.reshape(pb, n2, GROUP_W).astype(BF16)


def _ft_stage2(y1, g_hi, g_lo, chan, ft_w, n_batch, n1, n2, pb):
    ch, cl, sh, sl = chan
    full = lambda b, i: (0, 0)
    return pl.pallas_call(
        functools.partial(_ft3_kernel, pb=pb, n2=n2, norm=1.0 / math.sqrt(n1 * n2 * FT_DIM)),
        grid=(n_batch, n1 // pb),
        in_specs=[
            pl.BlockSpec((None, 2, pb, n2, GROUP_W), lambda b, i: (b, 0, i, 0, 0)),
            pl.BlockSpec((pb, 2 * n2, 2 * n2), lambda b, i: (i, 0, 0)),
            pl.BlockSpec((pb, 2 * n2, 2 * n2), lambda b, i: (i, 0, 0)),
            pl.BlockSpec((GROUP_W, GROUP_W), full), pl.BlockSpec((GROUP_W, GROUP_W), full),
            pl.BlockSpec((GROUP_W, GROUP_W), full), pl.BlockSpec((GROUP_W, GROUP_W), full),
            pl.BlockSpec((GROUP_W, GROUP_W), full),
        ],
        out_specs=pl.BlockSpec((None, pb, n2, GROUP_W), lambda b, i: (b, i, 0, 0)),
        out_shape=jax.ShapeDtypeStruct((n_batch, n1, n2, GROUP_W), BF16),
        compiler_params=_cparams("parallel", "parallel"),
        name="ft_stage2",
    )(y1, g_hi, g_lo, ch, cl, sh, sl, ft_w)


def _np_split(a):
    hi = jnp.asarray(a, F32).astype(BF16)
    lo = (jnp.asarray(a, F32) - hi.astype(F32)).astype(BF16)
    return hi, lo


@functools.lru_cache(maxsize=None)
def _ft_tables(n1, n2):
    n = n1 * n2
    p = np.arange(n1)
    ang1 = 2.0 * np.pi * ((p[:, None] * p[None, :]) % n1) / n1
    f1 = np.concatenate([np.cos(ang1), -np.sin(ang1)], axis=0)
    q = np.arange(n2)
    b = np.arange(n2)
    phase = (b[None, None, :] * (p[:, None, None] + n1 * q[None, :, None])) % n
    psi = 2.0 * np.pi * phase / n
    gc, gs = np.cos(psi), np.sin(psi)
    g = np.concatenate([np.concatenate([gc, gs], axis=2), np.concatenate([-gs, gc], axis=2)], axis=1)
    c = np.arange(FT_DIM)
    angc = 2.0 * np.pi * ((c[:, None] * c[None, :]) % FT_DIM) / FT_DIM
    eye = np.eye(GROUP_W // FT_DIM)
    cc, sc = np.kron(eye, np.cos(angc)), np.kron(eye, np.sin(angc))
    return f1.astype(np.float32), g.astype(np.float32), cc.astype(np.float32), sc.astype(np.float32)


def _out_kernel(x_ref, m_ref, na_ref, of_ref, ob_ref, gate_ref, dng_ref, df_ref, ft_ref, w_ref, g2_ref, wr_ref,
                xo_ref, h_ref, aff_ref):
    o = of_ref[...] + ob_ref[...]
    gate = gate_ref[...]
    parts = []
    for h in range(GROUP_HEADS):
        sl = slice(h * HEAD_DIM, (h + 1) * HEAD_DIM)
        oh = o[:, sl]
        parts.append(oh * lax.rsqrt(jnp.mean(oh * oh, axis=-1, keepdims=True) + RMS_EPS))
    y_dn = jnp.concatenate(parts, axis=-1) * dng_ref[...] * _silu(gate)
    mix = jnp.concatenate([na_ref[...], y_dn.astype(BF16), df_ref[...], ft_ref[...]], axis=-1)
    x = x_ref[...] + m_ref[2] * _dot(mix, w_ref[...])
    xo_ref[...] = x
    h2 = _norm_mod(x, g2_ref[...], m_ref[3], m_ref[4])
    h_ref[...] = h2.astype(BF16)
    logits = _dot3(h2, wr_ref[...])
    lane = lax.broadcasted_iota(jnp.int32, logits.shape, 1)
    logits = jnp.where(lane < N_EXPERTS, logits, NEG)
    e = jnp.exp(logits - jnp.max(logits, axis=-1, keepdims=True))
    aff_ref[...] = e / jnp.sum(e, axis=-1, keepdims=True)


def _out_proj(x, mods, o_na, o_f, o_b, pf, dn_g, o_df, o_ft, w_out, g2, w_router, n_batch, n_tok):
    nt = n_tok // ROW_TILE
    ctx_tile = nt - 1
    row = lambda b, i: (b, i, 0)
    blk = lambda: pl.BlockSpec((None, ROW_TILE, GROUP_W), row)
    vec = lambda n: pl.BlockSpec((1, n), lambda b, i: (0, 0))
    return pl.pallas_call(
        _out_kernel,
        grid=(n_batch, nt),
        in_specs=[
            pl.BlockSpec((None, ROW_TILE, D_MODEL), row),
            pl.BlockSpec((None, 6, 1, D_MODEL), lambda b, i: (jnp.where(i == ctx_tile, n_batch, b), 0, 0, 0)),
            blk(), blk(), blk(),
            pl.BlockSpec((None, ROW_TILE, GROUP_W), lambda b, i: (b, i, 3)),
            vec(GROUP_W), blk(), blk(),
            pl.BlockSpec((D_MODEL, D_MODEL), lambda b, i: (0, 0)),
            vec(D_MODEL),
            pl.BlockSpec((D_MODEL, 128), lambda b, i: (0, 0)),
        ],
        out_specs=[
            pl.BlockSpec((None, ROW_TILE, D_MODEL), row),
            pl.BlockSpec((None, ROW_TILE, D_MODEL), row),
            pl.BlockSpec((None, ROW_TILE, 128), row),
        ],
        out_shape=[
            jax.ShapeDtypeStruct((n_batch, n_tok, D_MODEL), F32),
            jax.ShapeDtypeStruct((n_batch, n_tok, D_MODEL), BF16),
            jax.ShapeDtypeStruct((n_batch, n_tok, 128), F32),
        ],
        compiler_params=_cparams("parallel", "parallel"),
        name="out_proj_router",
    )(x, mods, o_na, o_f, o_b, pf, dn_g, o_df, o_ft, w_out, g2.reshape(1, D_MODEL), w_router)


MOE_TF = 256


def _moe_kernel(x_ref, wg_ref, wu_ref, wd_ref, o_ref):
    f = pl.program_id(1)
    x = x_ref[...]
    a = _dot(x, wg_ref[...].astype(BF16))
    u = _dot(x, wu_ref[...].astype(BF16))
    hid = (_silu(a) * u).astype(BF16)
    y = _dot(hid, wd_ref[...].astype(BF16))

    @pl.when(f == 0)
    def _():
        o_ref[...] = y

    @pl.when(f != 0)
    def _():
        o_ref[...] += y


def _moe_ffn(xg, w_gate, w_up, w_down):
    n_e, m, _ = xg.shape
    return pl.pallas_call(
        _moe_kernel,
        grid=(n_e, D_EXPERT // MOE_TF),
        in_specs=[
            pl.BlockSpec((None, m, D_MODEL), lambda e, f: (e, 0, 0)),
            pl.BlockSpec((None, D_MODEL, MOE_TF), lambda e, f: (e, 0, f)),
            pl.BlockSpec((None, D_MODEL, MOE_TF), lambda e, f: (e, 0, f)),
            pl.BlockSpec((None, MOE_TF, D_MODEL), lambda e, f: (e, f, 0)),
        ],
        out_specs=pl.BlockSpec((None, m, D_MODEL), lambda e, f: (e, 0, 0)),
        out_shape=jax.ShapeDtypeStruct((n_e, m, D_MODEL), F32),
        compiler_params=_cparams("parallel", "arbitrary"),
        name="moe_ffn",
    )(xg, w_gate, w_up, w_down)


def _final_kernel(x_ref, g_ref, o_ref):
    x = x_ref[...]
    o_ref[...] = x * lax.rsqrt(jnp.mean(x * x, axis=-1, keepdims=True) + RMS_EPS) * g_ref[...]


def _final_norm(x, g, n_batch, n_lat):
    row = lambda b, i: (b, i, 0)
    return pl.pallas_call(
        _final_kernel,
        grid=(n_batch, n_lat // ROW_TILE),
        in_specs=[pl.BlockSpec((None, ROW_TILE, D_MODEL), row), pl.BlockSpec((1, D_MODEL), lambda b, i: (0, 0))],
        out_specs=pl.BlockSpec((None, ROW_TILE, D_MODEL), row),
        out_shape=jax.ShapeDtypeStruct((n_batch, n_lat, D_MODEL), F32),
        compiler_params=_cparams("parallel", "parallel"),
        name="final_norm",
    )(x, g.reshape(1, D_MODEL))


def _reorder_in_columns(w):
    gw = GROUP_W
    o = 6 * gw + 4 * GROUP_HEADS
    pad = jnp.zeros((w.shape[0], 128 - 4 * GROUP_HEADS), w.dtype)
    parts = [w[:, 0:3 * gw], w[:, o + gw:o + 4 * gw], w[:, 3 * gw:6 * gw], w[:, o:o + gw],
             w[:, o + 4 * gw:o + 5 * gw], w[:, 6 * gw:o], pad]
    return jnp.concatenate(parts, axis=1).astype(BF16)


@functools.lru_cache(maxsize=None)
def _rope_tables(n_lat, n_ctx):
    t = np.arange(n_lat)
    pos = np.stack([t // GRID_W, t % GRID_W], axis=-1).astype(np.float32)
    n_freq = DIFF_DIM // 4
    inv = (ROPE_BASE ** (-np.arange(n_freq, dtype=np.float32) / n_freq)).astype(np.float32)
    ang = (pos[:, :, None] * inv).reshape(n_lat, 2 * n_freq)
    lane = np.arange(GROUP_W)
    idx = (lane % DIFF_DIM) // 2
    sign = np.where(lane % 2 == 0, -1.0, 1.0)
    cos = np.concatenate([np.cos(ang)[:, idx], np.ones((n_ctx, GROUP_W))], axis=0)
    sin = np.concatenate([np.sin(ang)[:, idx] * sign, np.zeros((n_ctx, GROUP_W))], axis=0)
    return cos.astype(np.float32), sin.astype(np.float32)


def _lane_vec(v, n=128):
    v = v.reshape(-1).astype(F32)
    return jnp.zeros((1, n), F32).at[0, :v.shape[0]].set(v)


def _route(aff, cap):
    gate, idx = lax.top_k(jnp.swapaxes(aff, 1, 2), cap)
    return gate, idx


def kernel(x, c, ctx, c_ctx, w_mod, b_mod, norm1_g, w_in, na_rpb, dn_conv_w, dn_a_log, dn_dt_bias, dn_norm_g, df_lambda, df_norm_g, ft_w, w_out, norm2_g, w_router, w_gate, w_up, w_down, final_norm_g):
    n_batch, n_lat, _ = x.shape
    n_ctx = ctx.shape[1]
    n_tok = n_lat + n_ctx
    depth = w_mod.shape[0]
    assert n_batch + 1 <= 8 and n_ctx == ROW_TILE and n_lat % ROW_TILE == 0

    xs = jnp.concatenate([x, ctx], axis=1)
    cc = jnp.zeros((8, D_MODEL), F32).at[:n_batch].set(c).at[n_batch].set(c_ctx)
    m_all = _modulation(cc, w_mod, b_mod)

    cos_t, sin_t = (jnp.asarray(a) for a in _rope_tables(n_lat, n_ctx))
    n1 = n_lat // FT_N2
    f1, g_lat, cc_m, sc_m = _ft_tables(n1, FT_N2)
    _, g_ctx, _, _ = _ft_tables(1, n_ctx)
    f1h, f1l = _np_split(f1)
    glh, gll = _np_split(g_lat)
    gch, gcl = _np_split(g_ctx)
    chan = _np_split(cc_m) + _np_split(sc_m)
    cap_lat = EC_FACTOR * n_lat // N_EXPERTS
    cap_ctx = EC_FACTOR * n_ctx // N_EXPERTS
    batch_off = (jnp.arange(n_batch, dtype=jnp.int32) * n_tok)[:, None, None]

    for l in range(depth):
        lam_init = 0.8 - 0.6 * math.exp(-0.3 * l)
        mods = m_all[l, :n_batch + 1].reshape(n_batch + 1, 6, 1, D_MODEL)
        pbf, pf, pu = _in_proj(xs, norm1_g[l], mods, _reorder_in_columns(w_in[l]), cos_t, sin_t, n_batch, n_tok)

        o_na = _na_attention(pbf, _na_bias_table(na_rpb[l]), n_batch, n_lat, n_ctx)

        qkv = _dn_prep(pf, dn_conv_w[l], n_batch, n_lat, n_ctx)
        o_f, o_b = _dn_scan(qkv, pf, _lane_vec(dn_a_log[l]), _lane_vec(dn_dt_bias[l]), n_batch, n_lat, n_ctx)

        o_df = _df_attention(pbf, df_lambda[l], jnp.tile(df_norm_g[l], GROUP_HEADS).reshape(1, GROUP_W),
                             lam_init, n_batch, n_lat, n_ctx)

        ftw = ft_w[l].astype(BF16)
        y1 = _ft_stage1(pu.reshape(n_batch, n_tok // FT_N2, FT_N2 * GROUP_W), f1h, f1l, n_batch, n1)
        o_lat = _ft_stage2(y1.reshape(n_batch, 2, n1, FT_N2, GROUP_W), glh, gll, chan, ftw, n_batch, n1, FT_N2, 8)
        o_lat = jnp.swapaxes(o_lat, 1, 2).reshape(n_batch, n_lat, GROUP_W)
        u_ctx = pu[:, n_lat:]
        y1c = jnp.stack([u_ctx, jnp.zeros_like(u_ctx)], axis=1)[:, :, None]
        o_ctx = _ft_stage2(y1c, gch, gcl, chan, ftw, n_batch, 1, n_ctx, 1).reshape(n_batch, n_ctx, GROUP_W)
        o_ft = jnp.concatenate([o_lat, o_ctx], axis=1)

        wr = jnp.pad(w_router[l], ((0, 0), (0, 128 - N_EXPERTS)))
        xs, h2, aff = _out_proj(xs, mods, o_na, o_f, o_b, pf, jnp.tile(dn_norm_g[l], GROUP_HEADS).reshape(1, GROUP_W),
                                o_df, o_ft, w_out[l].astype(BF16), norm2_g[l], wr, n_batch, n_tok)

        gate_l, idx_l = _route(aff[:, :n_lat, :N_EXPERTS], cap_lat)
        gate_c, idx_c = _route(aff[:, n_lat:, :N_EXPERTS], cap_ctx)
        idx = jnp.concatenate([idx_l, idx_c + n_lat], axis=2)
        gate = jnp.concatenate([gate_l, gate_c], axis=2)
        cap = cap_lat + cap_ctx
        xg = jax.vmap(lambda hb, ib: hb[ib])(h2, idx)
        xg = jnp.swapaxes(xg, 0, 1).reshape(N_EXPERTS, n_batch * cap, D_MODEL)
        y = _moe_ffn(xg, w_gate[l], w_up[l], w_down[l]).reshape(N_EXPERTS, n_batch, cap, D_MODEL)
        y = jnp.swapaxes(y, 0, 1) * gate[..., None]
        moe = jnp.zeros((n_batch * n_tok, D_MODEL), F32).at[(idx + batch_off).reshape(-1)].add(y.reshape(-1, D_MODEL))
        moe = moe.reshape(n_batch, n_tok, D_MODEL)
        g_mlp = jnp.concatenate([jnp.broadcast_to(m_all[l, :n_batch, None, 5 * D_MODEL:], (n_batch, n_lat, D_MODEL)),
                                 jnp.broadcast_to(m_all[l, n_batch, 5 * D_MODEL:], (n_batch, n_ctx, D_MODEL))], axis=1)
        xs = xs + g_mlp * moe

    return _final_norm(xs, final_norm_g, n_batch, n_lat)
```

```python
import functools
import math

import numpy as np
import jax
import jax.numpy as jnp
from jax import lax
from jax.experimental import pallas as pl
from jax.experimental.pallas import tpu as pltpu

F32 = jnp.float32
BF16 = jnp.bfloat16

D_MODEL = 1024
DEPTH = 4
GRID_W = 64
GROUP_W = 256
GROUP_HEADS = 4
HEAD_DIM = 64
NA_WIN_H = 8
NA_WIN_W = 16
DN_CONV_K = 5
DN_CHUNK = 64
DIFF_DIM = 32
FT_DIM = 64
N_EXPERTS = 16
EC_FACTOR = 2
D_EXPERT = 2 * D_MODEL
ROPE_BASE = 10000.0
RMS_EPS = 1e-6
NEG = -1e30

ROW_TILE = 256
N_BF = 6 * GROUP_W
N_F32 = 4 * GROUP_W + 128
VMEM_LIMIT = 56 * 1024 * 1024


def _cparams(*sem):
    return pltpu.CompilerParams(dimension_semantics=sem, vmem_limit_bytes=VMEM_LIMIT)


def _split(x):
    hi = x.astype(BF16)
    lo = (x - hi.astype(F32)).astype(BF16)
    return hi, lo


def _dot(a, b):
    return jnp.dot(a, b, preferred_element_type=F32)


def _dot_nt(a, b):
    return lax.dot_general(a, b, (((1,), (1,)), ((), ())), preferred_element_type=F32)


def _dot_tn(a, b):
    return lax.dot_general(a, b, (((0,), (0,)), ((), ())), preferred_element_type=F32)


def _dot3(a, b):
    ah, al = _split(a)
    bh, bl = _split(b)
    return _dot(ah, bh) + (_dot(al, bh) + _dot(ah, bl))


def _dot3_const(ah, al, b):
    bh, bl = _split(b)
    return _dot(ah, bh) + (_dot(al, bh) + _dot(ah, bl))


def _silu(x):
    return x * jax.nn.sigmoid(x)


def _mod_kernel(s_ref, w_ref, b_ref, o_ref):
    s = _silu(s_ref[...])
    o_ref[...] = _dot(s.astype(BF16), w_ref[...].astype(BF16)) + b_ref[...]


def _modulation(cc, w_mod, b_mod):
    depth = w_mod.shape[0]
    nt = 6 * D_MODEL // 1024
    return pl.pallas_call(
        _mod_kernel,
        grid=(depth, nt),
        in_specs=[
            pl.BlockSpec((8, D_MODEL), lambda l, j: (0, 0)),
            pl.BlockSpec((None, D_MODEL, 1024), lambda l, j: (l, 0, j)),
            pl.BlockSpec((None, 1, 1024), lambda l, j: (l, 0, j)),
        ],
        out_specs=pl.BlockSpec((None, 8, 1024), lambda l, j: (l, 0, j)),
        out_shape=jax.ShapeDtypeStruct((depth, 8, 6 * D_MODEL), F32),
        compiler_params=_cparams("parallel", "parallel"),
        name="modulation",
    )(cc, w_mod, b_mod.reshape(depth, 1, 6 * D_MODEL))


def _norm_mod(x, g, shift, scale):
    y = x * lax.rsqrt(jnp.mean(x * x, axis=-1, keepdims=True) + RMS_EPS) * g
    return y * (1.0 + scale) + shift


def _in_kernel(x_ref, g_ref, m_ref, w_ref, cos_ref, sin_ref, obf_ref, of_ref, ou_ref):
    h = _norm_mod(x_ref[...], g_ref[...], m_ref[0], m_ref[1])
    p = _dot(h.astype(BF16), w_ref[...])
    gw = GROUP_W
    lane = lax.broadcasted_iota(jnp.int32, (1, gw), 1)
    even = (lane % 2) == 0
    cos, sin = cos_ref[...], sin_ref[...]

    def rope(t):
        sw = jnp.where(even, pltpu.roll(t, gw - 1, 1), pltpu.roll(t, 1, 1))
        return t * cos + sw * sin

    obf_ref[:, 0:gw] = (p[:, 0:gw] * HEAD_DIM ** -0.5).astype(BF16)
    obf_ref[:, gw:3 * gw] = p[:, gw:3 * gw].astype(BF16)
    obf_ref[:, 3 * gw:4 * gw] = (rope(p[:, 3 * gw:4 * gw]) * DIFF_DIM ** -0.5).astype(BF16)
    obf_ref[:, 4 * gw:5 * gw] = rope(p[:, 4 * gw:5 * gw]).astype(BF16)
    obf_ref[:, 5 * gw:6 * gw] = p[:, 5 * gw:6 * gw].astype(BF16)
    of_ref[:, 0:4 * gw] = p[:, 6 * gw:10 * gw]
    of_ref[:, 4 * gw:4 * gw + 128] = p[:, 11 * gw:11 * gw + 128]
    ou_ref[...] = p[:, 10 * gw:11 * gw]


def _in_proj(x, g, mods, w_perm, cos_t, sin_t, n_batch, n_tok):
    nt = n_tok // ROW_TILE
    ctx_tile = nt - 1
    n_out = w_perm.shape[1]
    row = lambda b, i: (b, i, 0)
    return pl.pallas_call(
        _in_kernel,
        grid=(n_batch, nt),
        in_specs=[
            pl.BlockSpec((None, ROW_TILE, D_MODEL), row),
            pl.BlockSpec((1, D_MODEL), lambda b, i: (0, 0)),
            pl.BlockSpec((None, 6, 1, D_MODEL), lambda b, i: (jnp.where(i == ctx_tile, n_batch, b), 0, 0, 0)),
            pl.BlockSpec((D_MODEL, n_out), lambda b, i: (0, 0)),
            pl.BlockSpec((ROW_TILE, GROUP_W), lambda b, i: (i, 0)),
            pl.BlockSpec((ROW_TILE, GROUP_W), lambda b, i: (i, 0)),
        ],
        out_specs=[
            pl.BlockSpec((None, ROW_TILE, N_BF), row),
            pl.BlockSpec((None, ROW_TILE, N_F32), row),
            pl.BlockSpec((None, ROW_TILE, GROUP_W), row),
        ],
        out_shape=[
            jax.ShapeDtypeStruct((n_batch, n_tok, N_BF), BF16),
            jax.ShapeDtypeStruct((n_batch, n_tok, N_F32), F32),
            jax.ShapeDtypeStruct((n_batch, n_tok, GROUP_W), F32),
        ],
        compiler_params=_cparams("parallel", "parallel"),
        name="in_proj",
    )(x, g.reshape(1, D_MODEL), mods, w_perm, cos_t, sin_t)


def _na_kernel(q_ref, k_ref, v_ref, bias_ref, o_ref, *, n_lat, n_ctx):
    r = pl.program_id(1)
    rows = n_lat // GRID_W
    wh = min(NA_WIN_H, rows)
    base = jnp.where(r < rows, jnp.clip(r - wh // 2, 0, rows - wh), 0)
    start = pl.multiple_of(base * GRID_W, GRID_W)
    n_win = wh * GRID_W
    q = q_ref[...]
    kw = k_ref[pl.ds(start, n_win), :]
    vw = v_ref[pl.ds(start, n_win), :]
    kc = k_ref[n_lat:n_lat + n_ctx, :]
    vc = v_ref[n_lat:n_lat + n_ctx, :]
    outs = []
    for h in range(GROUP_HEADS):
        sl = slice(h * HEAD_DIM, (h + 1) * HEAD_DIM)
        sw = _dot_nt(q[:, sl], kw[:, sl]) + bias_ref[h]
        sc = _dot_nt(q[:, sl], kc[:, sl])
        m = jnp.maximum(jnp.max(sw, axis=-1, keepdims=True), jnp.max(sc, axis=-1, keepdims=True))
        pw = jnp.exp(sw - m)
        pc = jnp.exp(sc - m)
        l = jnp.sum(pw, axis=-1, keepdims=True) + jnp.sum(pc, axis=-1, keepdims=True)
        o = _dot(pw.astype(BF16), vw[:, sl]) + _dot(pc.astype(BF16), vc[:, sl])
        outs.append(o / l)
    o_ref[...] = jnp.concatenate(outs, axis=-1).astype(BF16)


def _na_attention(pbf, bias_tab, n_batch, n_lat, n_ctx):
    n_tok = n_lat + n_ctx
    rows = n_lat // GRID_W
    wh = min(NA_WIN_H, rows)
    n_steps = n_tok // GRID_W

    def bias_idx(b, r):
        lat = jnp.clip(r - wh // 2, 0, rows - wh) - r + NA_WIN_H - 1
        return (jnp.where(r < rows, lat, NA_WIN_H), 0, 0, 0)

    return pl.pallas_call(
        functools.partial(_na_kernel, n_lat=n_lat, n_ctx=n_ctx),
        grid=(n_batch, n_steps),
        in_specs=[
            pl.BlockSpec((None, GRID_W, GROUP_W), lambda b, r: (b, r, 0)),
            pl.BlockSpec((None, n_tok, GROUP_W), lambda b, r: (b, 0, 1)),
            pl.BlockSpec((None, n_tok, GROUP_W), lambda b, r: (b, 0, 2)),
            pl.BlockSpec((None, GROUP_HEADS, GRID_W, wh * GRID_W), bias_idx),
        ],
        out_specs=pl.BlockSpec((None, GRID_W, GROUP_W), lambda b, r: (b, r, 0)),
        out_shape=jax.ShapeDtypeStruct((n_batch, n_tok, GROUP_W), BF16),
        compiler_params=_cparams("parallel", "arbitrary"),
        name="na_attention",
    )(pbf, pbf, pbf, bias_tab)


def _na_bias_table(rpb):
    wh = NA_WIN_H
    cq = np.arange(GRID_W)
    c0 = np.clip(cq - NA_WIN_W // 2, 0, GRID_W - NA_WIN_W)
    in_win = (cq[None, :] >= c0[:, None]) & (cq[None, :] < c0[:, None] + NA_WIN_W)
    dx = np.clip(cq[None, :] - cq[:, None], 1 - NA_WIN_W, NA_WIN_W - 1) + NA_WIN_W - 1
    dy = np.arange(wh)[:, None] + np.arange(wh)[None, :]
    pick_x = (dx[:, :, None] == np.arange(2 * NA_WIN_W - 1)).astype(np.float32)
    pick_y = (dy[:, :, None] == np.arange(2 * NA_WIN_H - 1)).astype(np.float32)
    hi = lax.Precision.HIGHEST
    cols = jnp.einsum('hyx,qkx->hyqk', rpb.astype(F32), pick_x, precision=hi)
    tab = jnp.einsum('swy,hyqk->shqwk', pick_y, cols, precision=hi)
    tab = jnp.where(in_win[None, None, :, None, :], tab, NEG)
    tab = tab.reshape(wh, GROUP_HEADS, GRID_W, wh * GRID_W)
    return jnp.concatenate([tab, jnp.full((1,) + tab.shape[1:], NEG, F32)], axis=0)


DF_KV = 512


def _df_kernel(lam_ref, q_ref, k_ref, v_ref, g_ref, o_ref, s_ref, *, chunks, lam_init):
    lq1, lk1, lq2, lk2 = lam_ref[0:1, :], lam_ref[1:2, :], lam_ref[2:3, :], lam_ref[3:4, :]
    lam = (jnp.exp(jnp.sum(lq1 * lk1, axis=-1, keepdims=True))
           - jnp.exp(jnp.sum(lq2 * lk2, axis=-1, keepdims=True)) + lam_init)
    q = q_ref[...]
    tq, gw = q.shape
    lane = lax.broadcasted_iota(jnp.int32, (tq, gw), 1)
    row2 = lax.broadcasted_iota(jnp.int32, (2 * tq, 1), 0)
    origin = chunks[0][0]
    zero = jnp.zeros_like(q)

    def head(h, out):
        qm = jnp.concatenate([jnp.where(lane // DIFF_DIM == 2 * h, q, zero),
                              jnp.where(lane // DIFF_DIM == 2 * h + 1, q, zero)], axis=0)
        m = jnp.full((2 * tq, 128), NEG, F32)
        for st, sz in chunks:
            s = _dot_nt(qm, k_ref[st:st + sz, :])
            s_ref[:, st - origin:st - origin + sz] = s
            for t in range(sz // 128):
                m = jnp.maximum(m, s[:, t * 128:(t + 1) * 128])
        m = jnp.max(m, axis=-1, keepdims=True)
        l = jnp.zeros((2 * tq, 128), F32)
        for st, sz in chunks:
            e = jnp.exp(s_ref[:, st - origin:st - origin + sz] - m)
            s_ref[:, st - origin:st - origin + sz] = e
            for t in range(sz // 128):
                l = l + e[:, t * 128:(t + 1) * 128]
        r = jnp.where(row2 < tq, 1.0, lam) / jnp.sum(l, axis=-1, keepdims=True)
        acc = jnp.zeros((tq, gw), F32)
        for st, sz in chunks:
            sl = slice(st - origin, st - origin + sz)
            a = (s_ref[:tq, sl] * r[:tq] - s_ref[tq:, sl] * r[tq:]).astype(BF16)
            acc = acc + _dot(a, v_ref[st:st + sz, :])
        return out + jnp.where(lane // HEAD_DIM == h, acc, 0.0)

    o = lax.fori_loop(0, GROUP_HEADS, head, jnp.zeros((tq, gw), F32))
    sq = o * o
    scale = jnp.zeros((tq, gw), F32)
    for h in range(GROUP_HEADS):
        in_head = lane // HEAD_DIM == h
        ms = jnp.sum(jnp.where(in_head, sq, 0.0), axis=-1, keepdims=True) * (1.0 / HEAD_DIM)
        scale = jnp.where(in_head, lax.rsqrt(ms + RMS_EPS), scale)
    o_ref[...] = (o * scale * g_ref[...] * (1.0 - lam_init)).astype(BF16)


def _df_call(pbf, df_lambda, g_tiled, lam_init, n_batch, n_tok, first_tile, n_tiles, chunks, name):
    n_keys = sum(sz for _, sz in chunks)
    return pl.pallas_call(
        functools.partial(_df_kernel, chunks=chunks, lam_init=lam_init),
        grid=(n_batch, n_tiles),
        in_specs=[
            pl.BlockSpec((4, DIFF_DIM), lambda b, i: (0, 0)),
            pl.BlockSpec((None, ROW_TILE, GROUP_W), lambda b, i: (b, first_tile + i, 3)),
            pl.BlockSpec((None, n_tok, GROUP_W), lambda b, i: (b, 0, 4)),
            pl.BlockSpec((None, n_tok, GROUP_W), lambda b, i: (b, 0, 5)),
            pl.BlockSpec((1, GROUP_W), lambda b, i: (0, 0)),
        ],
        out_specs=pl.BlockSpec((None, ROW_TILE, GROUP_W), lambda b, i: (b, i, 0)),
        out_shape=jax.ShapeDtypeStruct((n_batch, n_tiles * ROW_TILE, GROUP_W), BF16),
        scratch_shapes=[pltpu.VMEM((2 * ROW_TILE, n_keys), F32)],
        compiler_params=_cparams("parallel", "arbitrary"),
        name=name,
    )(df_lambda, pbf, pbf, pbf, g_tiled)


def _df_attention(pbf, df_lambda, g_tiled, lam_init, n_batch, n_lat, n_ctx):
    n_tok = n_lat + n_ctx
    lat_chunks = tuple((st, min(DF_KV, n_tok - st)) for st in range(0, n_tok, DF_KV))
    o_lat = _df_call(pbf, df_lambda, g_tiled, lam_init, n_batch, n_tok, 0, n_lat // ROW_TILE, lat_chunks, "df_attention")
    o_ctx = _df_call(pbf, df_lambda, g_tiled, lam_init, n_batch, n_tok, n_lat // ROW_TILE, n_ctx // ROW_TILE,
                     ((n_lat, n_ctx),), "df_attention_ctx")
    return jnp.concatenate([o_lat, o_ctx], axis=1)


CONV_PAD = 8


def _dn_prep_kernel(u_ref, w_ref, o_ref, pad_ref, *, n_lat, n_ctx):
    j = pl.program_id(1)
    zeros = jnp.zeros((CONV_PAD, GROUP_W), F32)
    lat0 = CONV_PAD
    ctx0 = 2 * CONV_PAD + n_lat
    pad_ref[0:CONV_PAD, :] = zeros
    pad_ref[lat0 + n_lat:ctx0, :] = zeros
    pad_ref[ctx0 + n_ctx:ctx0 + n_ctx + CONV_PAD, :] = zeros
    pad_ref[lat0:lat0 + n_lat, :] = u_ref[0:n_lat, :]
    pad_ref[ctx0:ctx0 + n_ctx, :] = u_ref[n_lat:n_lat + n_ctx, :]
    qscale = jnp.where(j == 0, HEAD_DIM ** -0.5, 1.0)
    half = DN_CONV_K // 2
    for (src0, dst0, n) in ((lat0, 0, n_lat), (ctx0, n_lat, n_ctx)):
        for t0 in range(0, n, ROW_TILE):
            acc = jnp.zeros((ROW_TILE, GROUP_W), F32)
            for tap in range(DN_CONV_K):
                a = src0 + t0 + tap - half
                acc = acc + pad_ref[a:a + ROW_TILE, :] * w_ref[tap:tap + 1, :]
            y = _silu(acc)
            parts = []
            for h in range(GROUP_HEADS):
                yh = y[:, h * HEAD_DIM:(h + 1) * HEAD_DIM]
                nrm = lax.rsqrt(jnp.sum(yh * yh, axis=-1, keepdims=True) + 1e-6) * qscale
                parts.append(yh * jnp.where(j == 2, 1.0, nrm))
            o_ref[dst0 + t0:dst0 + t0 + ROW_TILE, :] = jnp.concatenate(parts, axis=-1)


def _dn_prep(pf, conv_w, n_batch, n_lat, n_ctx):
    n_tok = n_lat + n_ctx
    return pl.pallas_call(
        functools.partial(_dn_prep_kernel, n_lat=n_lat, n_ctx=n_ctx),
        grid=(n_batch, 3),
        in_specs=[
            pl.BlockSpec((None, n_tok, GROUP_W), lambda b, j: (b, 0, j)),
            pl.BlockSpec((DN_CONV_K, GROUP_W), lambda b, j: (0, j)),
        ],
        out_specs=pl.BlockSpec((None, n_tok, GROUP_W), lambda b, j: (b, 0, j)),
        out_shape=jax.ShapeDtypeStruct((n_batch, n_tok, 3 * GROUP_W), F32),
        scratch_shapes=[pltpu.VMEM((n_tok + 3 * CONV_PAD, GROUP_W), F32)],
        compiler_params=_cparams("parallel", "arbitrary"),
        name="dn_prep",
    )(pf, conv_w)


DN_BLOCK = 4 * DN_CHUNK


def _split3(x):
    x1 = x.astype(BF16)
    r = x - x1.astype(F32)
    x2 = r.astype(BF16)
    return x1, x2, (r - x2.astype(F32)).astype(BF16)


def _block_diag(x, masks):
    zero = jnp.zeros_like(x)
    return jnp.concatenate([jnp.where(m, x, zero) for m in masks], axis=0)


def _mm3_heads(lhs, rhs, masks):
    lh, ll = _split(lhs)
    rh, rl = _split(rhs)
    bh, bl = _block_diag(rh, masks), _block_diag(rl, masks)
    return _dot(jnp.concatenate([lh, ll, lh], axis=1), jnp.concatenate([bh, bh, bl], axis=0))


def _dn_kernel(qf_ref, kf_ref, vf_ref, abf_ref, qb_ref, kb_ref, vb_ref, abb_ref, av_ref, dt_ref, exp_ref,
               of_ref, ob_ref, s_ref):
    @pl.when(pl.program_id(1) == 0)
    def _():
        s_ref[...] = jnp.zeros_like(s_ref)

    c, nb, gw = DN_CHUNK, DN_BLOCK, GROUP_W
    n_sub = nb // c
    lane4 = lax.broadcasted_iota(jnp.int32, (c, gw), 1)
    masks = [(lane4 // HEAD_DIM) == h for h in range(GROUP_HEADS)]
    ri = lax.broadcasted_iota(jnp.int32, (c, gw), 0)
    cj = lane4 % HEAD_DIM
    eye = (ri == cj).astype(F32)
    bi = lax.broadcasted_iota(jnp.int32, (nb, nb), 0)
    bj = lax.broadcasted_iota(jnp.int32, (nb, nb), 1)
    same_chunk = (bi // c) == (bj // c)
    lane1 = lax.broadcasted_iota(jnp.int32, (nb, 128), 1)
    dirs = ((qf_ref, kf_ref, vf_ref, abf_ref), (qb_ref, kb_ref, vb_ref, abb_ref))

    chains = []
    for d, (q_ref, k_ref, v_ref, ab_ref) in enumerate(dirs):
        incl = (ri >= cj) if d == 0 else (ri <= cj)
        strict = (ri > cj) if d == 0 else (ri < cj)
        ab = ab_ref[...]
        x = ab + dt_ref[...]
        softplus = jnp.maximum(x, 0.0) + jnp.log1p(jnp.exp(-jnp.abs(x)))
        g = -jnp.exp(av_ref[...]) * softplus
        tri = (same_chunk & ((bi >= bj) if d == 0 else (bi <= bj))).astype(BF16)
        g1, g2, g3 = _split3(g)
        gcs = _dot(tri, g1) + (_dot(tri, g2) + _dot(tri, g3))
        x1, x2, x3 = _split3(jnp.where(lane1 < 2 * GROUP_HEADS, gcs, jax.nn.sigmoid(ab)))
        e = exp_ref[d]
        spread = _dot(x1, e) + (_dot(x2, e) + _dot(x3, e))
        q_all, k_all, v_all = q_ref[...], k_ref[...], v_ref[...]
        last = c - 1 if d == 0 else 0
        for j in range(n_sub):
            rows = slice(j * c, (j + 1) * c)
            gc, beta = spread[rows, :gw], spread[rows, gw:]
            q, k, v = q_all[rows], k_all[rows], v_all[rows]
            gc_row = jnp.sum(gc * eye, axis=0, keepdims=True)
            decay = jnp.exp(jnp.where(incl, gc - gc_row, NEG))
            kbeta = k * beta
            kq = _dot_nt(jnp.concatenate([kbeta, q], axis=0).astype(BF16), _block_diag(k.astype(BF16), masks))
            eg = jnp.exp(gc)
            g_last = gc[last:last + 1, :]
            chains.append(dict(
                lm=jnp.where(strict, kq[:c] * decay, 0.0),
                qk=jnp.where(incl, kq[c:] * decay, 0.0).astype(BF16),
                rhs=jnp.concatenate([_block_diag((v * beta).astype(BF16), masks),
                                     _block_diag((kbeta * eg).astype(BF16), masks)], axis=1),
                q_dec=q * eg,
                k_dec=(k * jnp.exp(g_last - gc)).astype(BF16),
                g_end=jnp.exp(g_last)))

    for ch in chains:
        ch["p"] = eye - ch["lm"]
        ch["sq"] = _mm3_heads(ch["lm"], ch["lm"], masks)
    n = 4
    while n < c:
        for ch in chains:
            r = _mm3_heads(jnp.concatenate([ch["sq"], ch["p"]], axis=0), ch["sq"], masks)
            ch["sq"], ch["p"] = r[:c], ch["p"] + r[c:]
        n *= 2
    for ch in chains:
        t_inv = ch["p"] + _mm3_heads(ch["p"], ch["sq"], masks)
        uw = _dot(t_inv.astype(BF16), ch["rhs"])
        ch["u"] = uw[:, :gw]
        ch["wq"] = jnp.concatenate([uw[:, gw:], ch["q_dec"]], axis=0).astype(BF16)

    states = [s_ref[0], s_ref[1]]
    outs = [[None] * n_sub, [None] * n_sub]
    for step in range(n_sub):
        for d in range(2):
            j = step if d == 0 else n_sub - 1 - step
            ch = chains[d * n_sub + j]
            s = states[d]
            ws = _dot(ch["wq"], _block_diag(s.astype(BF16), masks))
            vb = (ch["u"] - ws[:c]).astype(BF16)
            outs[d][j] = ws[c:] + _dot(ch["qk"], _block_diag(vb, masks))
            full = _dot_tn(ch["k_dec"], vb)
            upd = jnp.where(masks[0], full[0:c], 0.0)
            for h in range(1, GROUP_HEADS):
                upd = upd + jnp.where(masks[h], full[h * c:(h + 1) * c], 0.0)
            states[d] = s * ch["g_end"] + upd
    of_ref[...] = jnp.concatenate(outs[0], axis=0)
    ob_ref[...] = jnp.concatenate(outs[1], axis=0)
    s_ref[...] = jnp.stack(states, axis=0)


def _dn_scan(qkv, pf, a_vec, dt_vec, n_batch, n_lat, n_ctx):
    n_tok = n_lat + n_ctx
    assert n_ctx == DN_BLOCK and n_lat % DN_BLOCK == 0
    nblk = n_tok // DN_BLOCK
    ab_blk = 4 * GROUP_W // 128

    def fwd(n):
        return jnp.where(n == 0, nblk - 1, n - 1)

    def bwd(n):
        return jnp.where(n == 0, nblk - 1, nblk - 1 - n)

    def spec(order, blk, width):
        return pl.BlockSpec((None, DN_BLOCK, width), lambda b, n: (b, order(n), blk))

    spread = np.zeros((2, 128, 2 * GROUP_W), np.float32)
    for d in range(2):
        for h in range(GROUP_HEADS):
            spread[d, GROUP_HEADS * d + h, h * HEAD_DIM:(h + 1) * HEAD_DIM] = 1.0
            spread[d, 2 * GROUP_HEADS + GROUP_HEADS * d + h, GROUP_W + h * HEAD_DIM:GROUP_W + (h + 1) * HEAD_DIM] = 1.0

    lane = pl.BlockSpec((1, 128), lambda b, n: (0, 0))
    out = jax.ShapeDtypeStruct((n_batch, n_tok, GROUP_W), F32)
    return pl.pallas_call(
        _dn_kernel,
        grid=(n_batch, nblk),
        in_specs=[spec(fwd, 0, GROUP_W), spec(fwd, 1, GROUP_W), spec(fwd, 2, GROUP_W), spec(fwd, ab_blk, 128),
                  spec(bwd, 0, GROUP_W), spec(bwd, 1, GROUP_W), spec(bwd, 2, GROUP_W), spec(bwd, ab_blk, 128),
                  lane, lane, pl.BlockSpec((2, 128, 2 * GROUP_W), lambda b, n: (0, 0, 0))],
        out_specs=[spec(fwd, 0, GROUP_W), spec(bwd, 0, GROUP_W)],
        out_shape=[out, out],
        scratch_shapes=[pltpu.VMEM((2, HEAD_DIM, GROUP_W), F32)],
        compiler_params=_cparams("parallel", "arbitrary"),
        name="dn_scan",
    )(qkv, qkv, qkv, pf, qkv, qkv, qkv, pf, a_vec, dt_vec, jnp.asarray(spread, BF16))


FT_N2 = 64


def _ft1_kernel(x_ref, fh_ref, fl_ref, o_ref):
    o_ref[...] = _dot3_const(fh_ref[...], fl_ref[...], x_ref[...])


def _ft_stage1(u_rows, f_hi, f_lo, n_batch, n1, tn=2048):
    width = u_rows.shape[2]
    return pl.pallas_call(
        _ft1_kernel,
        grid=(n_batch, width // tn),
        in_specs=[
            pl.BlockSpec((None, n1, tn), lambda b, j: (b, 0, j)),
            pl.BlockSpec((2 * n1, n1), lambda b, j: (0, 0)),
            pl.BlockSpec((2 * n1, n1), lambda b, j: (0, 0)),
        ],
        out_specs=pl.BlockSpec((None, 2 * n1, tn), lambda b, j: (b, 0, j)),
        out_shape=jax.ShapeDtypeStruct((n_batch, 2 * n1, width), F32),
        compiler_params=_cparams("parallel", "parallel"),
        name="ft_stage1",
    )(u_rows, f_hi, f_lo)


def _ft3_kernel(y_ref, gh_ref, gl_ref, ch_ref, cl_ref, sh_ref, sl_ref, w_ref, o_ref, *, pb, n2, norm):
    vr, vi = [], []
    for p in range(pb):
        rhs = jnp.concatenate([y_ref[0, p], y_ref[1, p]], axis=0)
        v = _dot3_const(gh_ref[p], gl_ref[p], rhs)
        vr.append(v[:n2])
        vi.append(v[n2:])
    vr = jnp.concatenate(vr, axis=0)
    vi = jnp.concatenate(vi, axis=0)
    vrh, vrl = _split(vr)
    vih, vil = _split(vi)
    y = (_dot(vrh, ch_ref[...]) + (_dot(vrl, ch_ref[...]) + _dot(vrh, cl_ref[...]))
         + _dot(vih, sh_ref[...]) + (_dot(vil, sh_ref[...]) + _dot(vih, sl_ref[...]))) * norm
    o = _dot(y.astype(BF16), w_ref[...])
    o_ref[...] = o.reshape(pb, n2, GROUP_W).astype(BF16)


def _ft_stage2(y1, g_hi, g_lo, chan, ft_w, n_batch, n1, n2, pb):
    ch, cl, sh, sl = chan
    full = lambda b, i: (0, 0)
    return pl.pallas_call(
        functools.partial(_ft3_kernel, pb=pb, n2=n2, norm=1.0 / math.sqrt(n1 * n2 * FT_DIM)),
        grid=(n_batch, n1 // pb),
        in_specs=[
            pl.BlockSpec((None, 2, pb, n2, GROUP_W), lambda b, i: (b, 0, i, 0, 0)),
            pl.BlockSpec((pb, 2 * n2, 2 * n2), lambda b, i: (i, 0, 0)),
            pl.BlockSpec((pb, 2 * n2, 2 * n2), lambda b, i: (i, 0, 0)),
            pl.BlockSpec((GROUP_W, GROUP_W), full), pl.BlockSpec((GROUP_W, GROUP_W), full),
            pl.BlockSpec((GROUP_W, GROUP_W), full), pl.BlockSpec((GROUP_W, GROUP_W), full),
            pl.BlockSpec((GROUP_W, GROUP_W), full),
        ],
        out_specs=pl.BlockSpec((None, pb, n2, GROUP_W), lambda b, i: (b, i, 0, 0)),
        out_shape=jax.ShapeDtypeStruct((n_batch, n1, n2, GROUP_W), BF16),
        compiler_params=_cparams("parallel", "parallel"),
        name="ft_stage2",
    )(y1, g_hi, g_lo, ch, cl, sh, sl, ft_w)


def _np_split(a):
    hi = jnp.asarray(a, F32).astype(BF16)
    lo = (jnp.asarray(a, F32) - hi.astype(F32)).astype(BF16)
    return hi, lo


@functools.lru_cache(maxsize=None)
def _ft_tables(n1, n2):
    n = n1 * n2
    p = np.arange(n1)
    ang1 = 2.0 * np.pi * ((p[:, None] * p[None, :]) % n1) / n1
    f1 = np.concatenate([np.cos(ang1), -np.sin(ang1)], axis=0)
    q = np.arange(n2)
    b = np.arange(n2)
    phase = (b[None, None, :] * (p[:, None, None] + n1 * q[None, :, None])) % n
    psi = 2.0 * np.pi * phase / n
    gc, gs = np.cos(psi), np.sin(psi)
    g = np.concatenate([np.concatenate([gc, gs], axis=2), np.concatenate([-gs, gc], axis=2)], axis=1)
    c = np.arange(FT_DIM)
    angc = 2.0 * np.pi * ((c[:, None] * c[None, :]) % FT_DIM) / FT_DIM
    eye = np.eye(GROUP_W // FT_DIM)
    cc, sc = np.kron(eye, np.cos(angc)), np.kron(eye, np.sin(angc))
    return f1.astype(np.float32), g.astype(np.float32), cc.astype(np.float32), sc.astype(np.float32)


def _out_kernel(x_ref, m_ref, na_ref, of_ref, ob_ref, gate_ref, dng_ref, df_ref, ft_ref, w_ref, g2_ref, wr_ref,
                xo_ref, h_ref, aff_ref):
    o = of_ref[...] + ob_ref[...]
    gate = gate_ref[...]
    parts = []
    for h in range(GROUP_HEADS):
        sl = slice(h * HEAD_DIM, (h + 1) * HEAD_DIM)
        oh = o[:, sl]
        parts.append(oh * lax.rsqrt(jnp.mean(oh * oh, axis=-1, keepdims=True) + RMS_EPS))
    y_dn = jnp.concatenate(parts, axis=-1) * dng_ref[...] * _silu(gate)
    mix = jnp.concatenate([na_ref[...], y_dn.astype(BF16), df_ref[...], ft_ref[...]], axis=-1)
    x = x_ref[...] + m_ref[2] * _dot(mix, w_ref[...])
    xo_ref[...] = x
    h2 = _norm_mod(x, g2_ref[...], m_ref[3], m_ref[4])
    bits = pltpu.bitcast(h2.astype(BF16).astype(F32), jnp.uint32)
    half = D_MODEL // 2
    h_ref[...] = bits[:, half:] | (bits[:, :half] >> 16)
    logits = _dot3(h2, wr_ref[...])
    lane = lax.broadcasted_iota(jnp.int32, logits.shape, 1)
    logits = jnp.where(lane < N_EXPERTS, logits, NEG)
    e = jnp.exp(logits - jnp.max(logits, axis=-1, keepdims=True))
    aff_ref[...] = e / jnp.sum(e, axis=-1, keepdims=True)


def _out_proj(x, mods, o_na, o_f, o_b, pf, dn_g, o_df, o_ft, w_out, g2, w_router, n_batch, n_tok):
    nt = n_tok // ROW_TILE
    ctx_tile = nt - 1
    row = lambda b, i: (b, i, 0)
    blk = lambda: pl.BlockSpec((None, ROW_TILE, GROUP_W), row)
    vec = lambda n: pl.BlockSpec((1, n), lambda b, i: (0, 0))
    return pl.pallas_call(
        _out_kernel,
        grid=(n_batch, nt),
        in_specs=[
            pl.BlockSpec((None, ROW_TILE, D_MODEL), row),
            pl.BlockSpec((None, 6, 1, D_MODEL), lambda b, i: (jnp.where(i == ctx_tile, n_batch, b), 0, 0, 0)),
            blk(), blk(), blk(),
            pl.BlockSpec((None, ROW_TILE, GROUP_W), lambda b, i: (b, i, 3)),
            vec(GROUP_W), blk(), blk(),
            pl.BlockSpec((D_MODEL, D_MODEL), lambda b, i: (0, 0)),
            vec(D_MODEL),
            pl.BlockSpec((D_MODEL, 128), lambda b, i: (0, 0)),
        ],
        out_specs=[
            pl.BlockSpec((None, ROW_TILE, D_MODEL), row),
            pl.BlockSpec((None, ROW_TILE, D_MODEL // 2), row),
            pl.BlockSpec((None, ROW_TILE, 128), row),
        ],
        out_shape=[
            jax.ShapeDtypeStruct((n_batch, n_tok, D_MODEL), F32),
            jax.ShapeDtypeStruct((n_batch, n_tok, D_MODEL // 2), jnp.uint32),
            jax.ShapeDtypeStruct((n_batch, n_tok, 128), F32),
        ],
        compiler_params=_cparams("parallel", "parallel"),
        name="out_proj_router",
    )(x, mods, o_na, o_f, o_b, pf, dn_g, o_df, o_ft, w_out, g2.reshape(1, D_MODEL), w_router)


MOE_TF = 256
GATHER_UNROLL = 8


def _gather_kernel(idx_ref, h_ref, o_ref, *, cap, n_e):
    base = (pl.program_id(0) * n_e + pl.program_id(1)) * cap

    def body(j, carry):
        for u in range(GATHER_UNROLL):
            i = j * GATHER_UNROLL + u
            o_ref[pl.ds(i, 1), :] = h_ref[pl.ds(idx_ref[base + i], 1), :]
        return carry

    lax.fori_loop(0, cap // GATHER_UNROLL, body, 0)


def _moe_gather(hp, idx):
    n_batch, n_tok, width = hp.shape
    _, n_e, cap = idx.shape
    assert cap % GATHER_UNROLL == 0
    return pl.pallas_call(
        functools.partial(_gather_kernel, cap=cap, n_e=n_e),
        grid_spec=pltpu.PrefetchScalarGridSpec(
            num_scalar_prefetch=1,
            grid=(n_batch, n_e),
            in_specs=[pl.BlockSpec((None, n_tok, width), lambda b, e, idx_ref: (b, 0, 0))],
            out_specs=pl.BlockSpec((None, cap, width), lambda b, e, idx_ref: (e, b, 0)),
        ),
        out_shape=jax.ShapeDtypeStruct((n_e, n_batch * cap, width), jnp.uint32),
        compiler_params=_cparams("parallel", "arbitrary"),
        name="moe_gather",
    )(idx.reshape(-1), hp)


def _combine_kernel(idx_ref, y_ref, o_ref, *, cap, n_e):
    e = pl.program_id(1)

    @pl.when(e == 0)
    def _():
        o_ref[...] = jnp.zeros_like(o_ref)

    base = (pl.program_id(0) * n_e + e) * cap

    def body(j, carry):
        for u in range(GATHER_UNROLL):
            i = j * GATHER_UNROLL + u
            t = idx_ref[base + i]
            o_ref[pl.ds(t, 1), :] = o_ref[pl.ds(t, 1), :] + y_ref[pl.ds(i, 1), :]
        return carry

    lax.fori_loop(0, cap // GATHER_UNROLL, body, 0)


def _moe_combine(y, idx, n_tok):
    n_batch, n_e, cap = idx.shape
    d = y.shape[2]
    return pl.pallas_call(
        functools.partial(_combine_kernel, cap=cap, n_e=n_e),
        grid_spec=pltpu.PrefetchScalarGridSpec(
            num_scalar_prefetch=1,
            grid=(n_batch, n_e),
            in_specs=[pl.BlockSpec((None, cap, d), lambda b, e, idx_ref: (e, b, 0))],
            out_specs=pl.BlockSpec((None, n_tok, d), lambda b, e, idx_ref: (b, 0, 0)),
        ),
        out_shape=jax.ShapeDtypeStruct((n_batch, n_tok, d), F32),
        compiler_params=_cparams("parallel", "arbitrary"),
        name="moe_combine",
    )(idx.reshape(-1), y)


def _moe_kernel(x_ref, wg_ref, wu_ref, wd_ref, gate_ref, o_ref):
    f = pl.program_id(1)
    packed = x_ref[...]
    x = jnp.concatenate([pltpu.bitcast(packed << 16, F32), pltpu.bitcast(packed & jnp.uint32(0xFFFF0000), F32)],
                        axis=1).astype(BF16)
    a = _dot(x, wg_ref[...].astype(BF16))
    u = _dot(x, wu_ref[...].astype(BF16))
    hid = (_silu(a) * u).astype(BF16)
    y = _dot(hid, wd_ref[...].astype(BF16))

    last = pl.num_programs(1) - 1

    @pl.when(f == 0)
    def _():
        o_ref[...] = y

    @pl.when((f != 0) & (f != last))
    def _():
        o_ref[...] += y

    @pl.when(f == last)
    def _():
        o_ref[...] = (o_ref[...] + y) * gate_ref[...]


def _moe_ffn(xg, w_gate, w_up, w_down, gate, layer):
    n_e, m, _ = xg.shape
    return pl.pallas_call(
        _moe_kernel,
        grid=(n_e, D_EXPERT // MOE_TF),
        in_specs=[
            pl.BlockSpec((None, m, D_MODEL // 2), lambda e, f: (e, 0, 0)),
            pl.BlockSpec((None, None, D_MODEL, MOE_TF), lambda e, f: (layer, e, 0, f)),
            pl.BlockSpec((None, None, D_MODEL, MOE_TF), lambda e, f: (layer, e, 0, f)),
            pl.BlockSpec((None, None, MOE_TF, D_MODEL), lambda e, f: (layer, e, f, 0)),
            pl.BlockSpec((None, m, 1), lambda e, f: (e, 0, 0)),
        ],
        out_specs=pl.BlockSpec((None, m, D_MODEL), lambda e, f: (e, 0, 0)),
        out_shape=jax.ShapeDtypeStruct((n_e, m, D_MODEL), F32),
        compiler_params=_cparams("parallel", "arbitrary"),
        name="moe_ffn",
    )(xg, w_gate, w_up, w_down, gate)


def _final_kernel(x_ref, g_ref, o_ref):
    x = x_ref[...]
    o_ref[...] = x * lax.rsqrt(jnp.mean(x * x, axis=-1, keepdims=True) + RMS_EPS) * g_ref[...]


def _final_norm(x, g, n_batch, n_lat):
    row = lambda b, i: (b, i, 0)
    return pl.pallas_call(
        _final_kernel,
        grid=(n_batch, n_lat // ROW_TILE),
        in_specs=[pl.BlockSpec((None, ROW_TILE, D_MODEL), row), pl.BlockSpec((1, D_MODEL), lambda b, i: (0, 0))],
        out_specs=pl.BlockSpec((None, ROW_TILE, D_MODEL), row),
        out_shape=jax.ShapeDtypeStruct((n_batch, n_lat, D_MODEL), F32),
        compiler_params=_cparams("parallel", "parallel"),
        name="final_norm",
    )(x, g.reshape(1, D_MODEL))


def _reorder_in_columns(w):
    gw = GROUP_W
    o = 6 * gw + 4 * GROUP_HEADS
    pad = jnp.zeros((w.shape[0], 128 - 4 * GROUP_HEADS), w.dtype)
    parts = [w[:, 0:3 * gw], w[:, o + gw:o + 4 * gw], w[:, 3 * gw:6 * gw], w[:, o:o + gw],
             w[:, o + 4 * gw:o + 5 * gw], w[:, 6 * gw:o], pad]
    return jnp.concatenate(parts, axis=1).astype(BF16)


@functools.lru_cache(maxsize=None)
def _rope_tables(n_lat, n_ctx):
    t = np.arange(n_lat)
    pos = np.stack([t // GRID_W, t % GRID_W], axis=-1).astype(np.float32)
    n_freq = DIFF_DIM // 4
    inv = (ROPE_BASE ** (-np.arange(n_freq, dtype=np.float32) / n_freq)).astype(np.float32)
    ang = (pos[:, :, None] * inv).reshape(n_lat, 2 * n_freq)
    lane = np.arange(GROUP_W)
    idx = (lane % DIFF_DIM) // 2
    sign = np.where(lane % 2 == 0, -1.0, 1.0)
    cos = np.concatenate([np.cos(ang)[:, idx], np.ones((n_ctx, GROUP_W))], axis=0)
    sin = np.concatenate([np.sin(ang)[:, idx] * sign, np.zeros((n_ctx, GROUP_W))], axis=0)
    return cos.astype(np.float32), sin.astype(np.float32)


def _lane_vec(v, n=128):
    v = v.reshape(-1).astype(F32)
    return jnp.zeros((1, n), F32).at[0, :v.shape[0]].set(v)


def _route(aff, cap):
    gate, idx = lax.top_k(jnp.swapaxes(aff, 1, 2), cap)
    return gate, idx


def kernel(x, c, ctx, c_ctx, w_mod, b_mod, norm1_g, w_in, na_rpb, dn_conv_w, dn_a_log, dn_dt_bias, dn_norm_g, df_lambda, df_norm_g, ft_w, w_out, norm2_g, w_router, w_gate, w_up, w_down, final_norm_g):
    n_batch, n_lat, _ = x.shape
    n_ctx = ctx.shape[1]
    n_tok = n_lat + n_ctx
    depth = w_mod.shape[0]
    assert n_batch + 1 <= 8 and n_ctx == ROW_TILE and n_lat % ROW_TILE == 0

    xs = jnp.concatenate([x, ctx], axis=1)
    cc = jnp.zeros((8, D_MODEL), F32).at[:n_batch].set(c).at[n_batch].set(c_ctx)
    m_all = _modulation(cc, w_mod, b_mod)

    cos_t, sin_t = (jnp.asarray(a) for a in _rope_tables(n_lat, n_ctx))
    n1 = n_lat // FT_N2
    f1, g_lat, cc_m, sc_m = _ft_tables(n1, FT_N2)
    _, g_ctx, _, _ = _ft_tables(1, n_ctx)
    f1h, f1l = _np_split(f1)
    glh, gll = _np_split(g_lat)
    gch, gcl = _np_split(g_ctx)
    chan = _np_split(cc_m) + _np_split(sc_m)
    cap_lat = EC_FACTOR * n_lat // N_EXPERTS
    cap_ctx = EC_FACTOR * n_ctx // N_EXPERTS

    for l in range(depth):
        lam_init = 0.8 - 0.6 * math.exp(-0.3 * l)
        mods = m_all[l, :n_batch + 1].reshape(n_batch + 1, 6, 1, D_MODEL)
        pbf, pf, pu = _in_proj(xs, norm1_g[l], mods, _reorder_in_columns(w_in[l]), cos_t, sin_t, n_batch, n_tok)

        o_na = _na_attention(pbf, _na_bias_table(na_rpb[l]), n_batch, n_lat, n_ctx)

        qkv = _dn_prep(pf, dn_conv_w[l], n_batch, n_lat, n_ctx)
        o_f, o_b = _dn_scan(qkv, pf, _lane_vec(dn_a_log[l]), _lane_vec(dn_dt_bias[l]), n_batch, n_lat, n_ctx)

        o_df = _df_attention(pbf, df_lambda[l], jnp.tile(df_norm_g[l], GROUP_HEADS).reshape(1, GROUP_W),
                             lam_init, n_batch, n_lat, n_ctx)

        ftw = ft_w[l].astype(BF16)
        y1 = _ft_stage1(pu.reshape(n_batch, n_tok // FT_N2, FT_N2 * GROUP_W), f1h, f1l, n_batch, n1)
        o_lat = _ft_stage2(y1.reshape(n_batch, 2, n1, FT_N2, GROUP_W), glh, gll, chan, ftw, n_batch, n1, FT_N2, 8)
        o_lat = jnp.swapaxes(o_lat, 1, 2).reshape(n_batch, n_lat, GROUP_W)
        u_ctx = pu[:, n_lat:]
        y1c = jnp.stack([u_ctx, jnp.zeros_like(u_ctx)], axis=1)[:, :, None]
        o_ctx = _ft_stage2(y1c, gch, gcl, chan, ftw, n_batch, 1, n_ctx, 1).reshape(n_batch, n_ctx, GROUP_W)
        o_ft = jnp.concatenate([o_lat, o_ctx], axis=1)

        wr = jnp.pad(w_router[l], ((0, 0), (0, 128 - N_EXPERTS)))
        xs, h2, aff = _out_proj(xs, mods, o_na, o_f, o_b, pf, jnp.tile(dn_norm_g[l], GROUP_HEADS).reshape(1, GROUP_W),
                                o_df, o_ft, w_out[l].astype(BF16), norm2_g[l], wr, n_batch, n_tok)

        gate_l, idx_l = _route(aff[:, :n_lat, :N_EXPERTS], cap_lat)
        gate_c, idx_c = _route(aff[:, n_lat:, :N_EXPERTS], cap_ctx)
        idx = jnp.concatenate([idx_l, idx_c + n_lat], axis=2)
        gate = jnp.concatenate([gate_l, gate_c], axis=2)
        cap = cap_lat + cap_ctx
        gate_col = jnp.swapaxes(gate, 0, 1).reshape(N_EXPERTS, n_batch * cap, 1)
        y = _moe_ffn(_moe_gather(h2, idx), w_gate, w_up, w_down, gate_col, l)
        moe = _moe_combine(y, idx, n_tok)
        g_mlp = jnp.concatenate([jnp.broadcast_to(m_all[l, :n_batch, None, 5 * D_MODEL:], (n_batch, n_lat, D_MODEL)),
                                 jnp.broadcast_to(m_all[l, n_batch, 5 * D_MODEL:], (n_batch, n_ctx, D_MODEL))], axis=1)
        xs = xs + g_mlp * moe

    return _final_norm(xs, final_norm_g, n_batch, n_lat)
```

```python
import functools
import math

import numpy as np
import jax
import jax.numpy as jnp
from jax import lax
from jax.experimental import pallas as pl
from jax.experimental.pallas import tpu as pltpu

F32 = jnp.float32
BF16 = jnp.bfloat16

D_MODEL = 1024
DEPTH = 4
GRID_W = 64
GROUP_W = 256
GROUP_HEADS = 4
HEAD_DIM = 64
NA_WIN_H = 8
NA_WIN_W = 16
DN_CONV_K = 5
DN_CHUNK = 64
DIFF_DIM = 32
FT_DIM = 64
N_EXPERTS = 16
EC_FACTOR = 2
D_EXPERT = 2 * D_MODEL
ROPE_BASE = 10000.0
RMS_EPS = 1e-6
NEG = -1e30

ROW_TILE = 256
N_BF = 6 * GROUP_W
N_F32 = 4 * GROUP_W + 128
VMEM_LIMIT = 56 * 1024 * 1024


def _cparams(*sem):
    return pltpu.CompilerParams(dimension_semantics=sem, vmem_limit_bytes=VMEM_LIMIT)


def _split(x):
    hi = x.astype(BF16)
    lo = (x - hi.astype(F32)).astype(BF16)
    return hi, lo


def _dot(a, b):
    return jnp.dot(a, b, preferred_element_type=F32)


def _dot_nt(a, b):
    return lax.dot_general(a, b, (((1,), (1,)), ((), ())), preferred_element_type=F32)


def _dot_tn(a, b):
    return lax.dot_general(a, b, (((0,), (0,)), ((), ())), preferred_element_type=F32)


def _dot3(a, b):
    ah, al = _split(a)
    bh, bl = _split(b)
    return _dot(ah, bh) + (_dot(al, bh) + _dot(ah, bl))


def _dot3_const(ah, al, b):
    bh, bl = _split(b)
    return _dot(ah, bh) + (_dot(al, bh) + _dot(ah, bl))


def _silu(x):
    return x * jax.nn.sigmoid(x)


def _mod_kernel(s_ref, w_ref, b_ref, o_ref):
    s = _silu(s_ref[...])
    o_ref[...] = _dot(s.astype(BF16), w_ref[...].astype(BF16)) + b_ref[...]


def _modulation(cc, w_mod, b_mod):
    depth = w_mod.shape[0]
    nt = 6 * D_MODEL // 1024
    return pl.pallas_call(
        _mod_kernel,
        grid=(depth, nt),
        in_specs=[
            pl.BlockSpec((8, D_MODEL), lambda l, j: (0, 0)),
            pl.BlockSpec((None, D_MODEL, 1024), lambda l, j: (l, 0, j)),
            pl.BlockSpec((None, 1, 1024), lambda l, j: (l, 0, j)),
        ],
        out_specs=pl.BlockSpec((None, 8, 1024), lambda l, j: (l, 0, j)),
        out_shape=jax.ShapeDtypeStruct((depth, 8, 6 * D_MODEL), F32),
        compiler_params=_cparams("parallel", "parallel"),
        name="modulation",
    )(cc, w_mod, b_mod.reshape(depth, 1, 6 * D_MODEL))


def _norm_mod(x, g, shift, scale):
    y = x * lax.rsqrt(jnp.mean(x * x, axis=-1, keepdims=True) + RMS_EPS) * g
    return y * (1.0 + scale) + shift


def _in_kernel(x_ref, g_ref, m_ref, w_ref, cos_ref, sin_ref, obf_ref, of_ref, ou_ref):
    h = _norm_mod(x_ref[...], g_ref[...], m_ref[0], m_ref[1])
    p = _dot(h.astype(BF16), w_ref[...])
    gw = GROUP_W
    lane = lax.broadcasted_iota(jnp.int32, (1, gw), 1)
    even = (lane % 2) == 0
    cos, sin = cos_ref[...], sin_ref[...]

    def rope(t):
        sw = jnp.where(even, pltpu.roll(t, gw - 1, 1), pltpu.roll(t, 1, 1))
        return t * cos + sw * sin

    obf_ref[:, 0:gw] = (p[:, 0:gw] * HEAD_DIM ** -0.5).astype(BF16)
    obf_ref[:, gw:3 * gw] = p[:, gw:3 * gw].astype(BF16)
    obf_ref[:, 3 * gw:4 * gw] = (rope(p[:, 3 * gw:4 * gw]) * DIFF_DIM ** -0.5).astype(BF16)
    obf_ref[:, 4 * gw:5 * gw] = rope(p[:, 4 * gw:5 * gw]).astype(BF16)
    obf_ref[:, 5 * gw:6 * gw] = p[:, 5 * gw:6 * gw].astype(BF16)
    of_ref[:, 0:4 * gw] = p[:, 6 * gw:10 * gw]
    of_ref[:, 4 * gw:4 * gw + 128] = p[:, 11 * gw:11 * gw + 128]
    ou_ref[...] = p[:, 10 * gw:11 * gw]


def _in_proj(x, g, mods, w_perm, cos_t, sin_t, n_batch, n_tok):
    nt = n_tok // ROW_TILE
    ctx_tile = nt - 1
    n_out = w_perm.shape[1]
    row = lambda b, i: (b, i, 0)
    return pl.pallas_call(
        _in_kernel,
        grid=(n_batch, nt),
        in_specs=[
            pl.BlockSpec((None, ROW_TILE, D_MODEL), row),
            pl.BlockSpec((1, D_MODEL), lambda b, i: (0, 0)),
            pl.BlockSpec((None, 6, 1, D_MODEL), lambda b, i: (jnp.where(i == ctx_tile, n_batch, b), 0, 0, 0)),
            pl.BlockSpec((D_MODEL, n_out), lambda b, i: (0, 0)),
            pl.BlockSpec((ROW_TILE, GROUP_W), lambda b, i: (i, 0)),
            pl.BlockSpec((ROW_TILE, GROUP_W), lambda b, i: (i, 0)),
        ],
        out_specs=[
            pl.BlockSpec((None, ROW_TILE, N_BF), row),
            pl.BlockSpec((None, ROW_TILE, N_F32), row),
            pl.BlockSpec((None, ROW_TILE, GROUP_W), row),
        ],
        out_shape=[
            jax.ShapeDtypeStruct((n_batch, n_tok, N_BF), BF16),
            jax.ShapeDtypeStruct((n_batch, n_tok, N_F32), F32),
            jax.ShapeDtypeStruct((n_batch, n_tok, GROUP_W), F32),
        ],
        compiler_params=_cparams("parallel", "parallel"),
        name="in_proj",
    )(x, g.reshape(1, D_MODEL), mods, w_perm, cos_t, sin_t)


def _na_kernel(q_ref, k_ref, v_ref, bias_ref, o_ref, *, n_lat, n_ctx):
    i = pl.program_id(1)
    rows = n_lat // GRID_W
    wh = min(NA_WIN_H, rows)
    n_win = wh * GRID_W
    per = ROW_TILE // GRID_W
    kc = k_ref[n_lat:n_lat + n_ctx, :]
    vc = v_ref[n_lat:n_lat + n_ctx, :]
    lane = lax.broadcasted_iota(jnp.int32, (GRID_W, GROUP_W), 1)
    masks = [lane // HEAD_DIM == h for h in range(GROUP_HEADS)]
    for u in range(per):
        r = i * per + u
        is_lat = r < rows
        base = jnp.where(is_lat, jnp.clip(r - wh // 2, 0, rows - wh), 0)
        slab = jnp.where(is_lat, base - r + NA_WIN_H - 1, NA_WIN_H)
        start = pl.multiple_of(base * GRID_W, GRID_W)
        q = q_ref[u * GRID_W:(u + 1) * GRID_W, :]
        qm = jnp.concatenate([jnp.where(m, q, jnp.zeros_like(q)) for m in masks], axis=0)
        keys = jnp.concatenate([k_ref[pl.ds(start, n_win), :], kc], axis=0)
        vals = jnp.concatenate([v_ref[pl.ds(start, n_win), :], vc], axis=0)
        s = _dot_nt(qm, keys)
        sw = s[:, :n_win] + bias_ref[slab].reshape(GROUP_HEADS * GRID_W, n_win)
        sc = s[:, n_win:]
        m = jnp.maximum(jnp.max(sw, axis=-1, keepdims=True), jnp.max(sc, axis=-1, keepdims=True))
        pw = jnp.exp(sw - m)
        pc = jnp.exp(sc - m)
        l = jnp.sum(pw, axis=-1, keepdims=True) + jnp.sum(pc, axis=-1, keepdims=True)
        res = _dot(jnp.concatenate([pw, pc], axis=1).astype(BF16), vals) / l
        o = jnp.where(masks[0], res[0:GRID_W], 0.0)
        for h in range(1, GROUP_HEADS):
            o = o + jnp.where(masks[h], res[h * GRID_W:(h + 1) * GRID_W], 0.0)
        o_ref[u * GRID_W:(u + 1) * GRID_W, :] = o.astype(BF16)


def _na_attention(pbf, bias_tab, n_batch, n_lat, n_ctx):
    n_tok = n_lat + n_ctx
    return pl.pallas_call(
        functools.partial(_na_kernel, n_lat=n_lat, n_ctx=n_ctx),
        grid=(n_batch, n_tok // ROW_TILE),
        in_specs=[
            pl.BlockSpec((None, ROW_TILE, GROUP_W), lambda b, i: (b, i, 0)),
            pl.BlockSpec((None, n_tok, GROUP_W), lambda b, i: (b, 0, 1)),
            pl.BlockSpec((None, n_tok, GROUP_W), lambda b, i: (b, 0, 2)),
            pl.BlockSpec(bias_tab.shape, lambda b, i: (0, 0, 0, 0)),
        ],
        out_specs=pl.BlockSpec((None, ROW_TILE, GROUP_W), lambda b, i: (b, i, 0)),
        out_shape=jax.ShapeDtypeStruct((n_batch, n_tok, GROUP_W), BF16),
        compiler_params=_cparams("parallel", "arbitrary"),
        name="na_attention",
    )(pbf, pbf, pbf, bias_tab)


def _na_bias_table(rpb):
    wh = NA_WIN_H
    cq = np.arange(GRID_W)
    c0 = np.clip(cq - NA_WIN_W // 2, 0, GRID_W - NA_WIN_W)
    in_win = (cq[None, :] >= c0[:, None]) & (cq[None, :] < c0[:, None] + NA_WIN_W)
    dx = np.clip(cq[None, :] - cq[:, None], 1 - NA_WIN_W, NA_WIN_W - 1) + NA_WIN_W - 1
    dy = np.arange(wh)[:, None] + np.arange(wh)[None, :]
    pick_x = (dx[:, :, None] == np.arange(2 * NA_WIN_W - 1)).astype(np.float32)
    pick_y = (dy[:, :, None] == np.arange(2 * NA_WIN_H - 1)).astype(np.float32)
    hi = lax.Precision.HIGHEST
    cols = jnp.einsum('hyx,qkx->hyqk', rpb.astype(F32), pick_x, precision=hi)
    tab = jnp.einsum('swy,hyqk->shqwk', pick_y, cols, precision=hi)
    tab = jnp.where(in_win[None, None, :, None, :], tab, NEG)
    tab = tab.reshape(wh, GROUP_HEADS, GRID_W, wh * GRID_W)
    return jnp.concatenate([tab, jnp.full((1,) + tab.shape[1:], NEG, F32)], axis=0)


DF_KV = 512


def _df_kernel(lam_ref, q_ref, k_ref, v_ref, g_ref, o_ref, s_ref, *, chunks, lam_init):
    lq1, lk1, lq2, lk2 = lam_ref[0:1, :], lam_ref[1:2, :], lam_ref[2:3, :], lam_ref[3:4, :]
    lam = (jnp.exp(jnp.sum(lq1 * lk1, axis=-1, keepdims=True))
           - jnp.exp(jnp.sum(lq2 * lk2, axis=-1, keepdims=True)) + lam_init)
    q = q_ref[...]
    tq, gw = q.shape
    lane = lax.broadcasted_iota(jnp.int32, (tq, gw), 1)
    row2 = lax.broadcasted_iota(jnp.int32, (2 * tq, 1), 0)
    origin = chunks[0][0]
    zero = jnp.zeros_like(q)

    def head(h, out):
        qm = [jnp.where(lane // DIFF_DIM == 2 * h + mp, q, zero) for mp in range(2)]
        m = [jnp.full((tq, 128), NEG, F32) for _ in range(2)]
        for st, sz in chunks:
            k = k_ref[st:st + sz, :]
            for mp in range(2):
                s = _dot_nt(qm[mp], k)
                s_ref[mp * tq:(mp + 1) * tq, st - origin:st - origin + sz] = s
                for t in range(sz // 128):
                    m[mp] = jnp.maximum(m[mp], s[:, t * 128:(t + 1) * 128])
        m = [jnp.max(x, axis=-1, keepdims=True) for x in m]
        ones_lane = ((h + 1) % GROUP_HEADS) * HEAD_DIM
        acc = [jnp.zeros((tq, gw), F32) for _ in range(2)]
        for st, sz in chunks:
            v = v_ref[st:st + sz, :]
            lane_v = lax.broadcasted_iota(jnp.int32, v.shape, 1)
            v = jnp.where(lane_v == ones_lane, jnp.ones_like(v), v)
            for mp in range(2):
                e = jnp.exp((s_ref[mp * tq:(mp + 1) * tq, st - origin:st - origin + sz] - m[mp]).astype(BF16))
                acc[mp] = acc[mp] + _dot(e, v)
        w = []
        for mp in range(2):
            l = jnp.sum(jnp.where(lane == ones_lane, acc[mp], 0.0), axis=-1, keepdims=True)
            w.append(acc[mp] * ((1.0 if mp == 0 else lam) / l))
        return out + jnp.where(lane // HEAD_DIM == h, w[0] - w[1], 0.0)

    o = lax.fori_loop(0, GROUP_HEADS, head, jnp.zeros((tq, gw), F32))
    sq = o * o
    scale = jnp.zeros((tq, gw), F32)
    for h in range(GROUP_HEADS):
        in_head = lane // HEAD_DIM == h
        ms = jnp.sum(jnp.where(in_head, sq, 0.0), axis=-1, keepdims=True) * (1.0 / HEAD_DIM)
        scale = jnp.where(in_head, lax.rsqrt(ms + RMS_EPS), scale)
    o_ref[...] = (o * scale * g_ref[...] * (1.0 - lam_init)).astype(BF16)


def _df_call(pbf, df_lambda, g_tiled, lam_init, n_batch, n_tok, first_tile, n_tiles, chunks, name):
    n_keys = sum(sz for _, sz in chunks)
    return pl.pallas_call(
        functools.partial(_df_kernel, chunks=chunks, lam_init=lam_init),
        grid=(n_batch, n_tiles),
        in_specs=[
            pl.BlockSpec((4, DIFF_DIM), lambda b, i: (0, 0)),
            pl.BlockSpec((None, ROW_TILE, GROUP_W), lambda b, i: (b, first_tile + i, 3)),
            pl.BlockSpec((None, n_tok, GROUP_W), lambda b, i: (b, 0, 4)),
            pl.BlockSpec((None, n_tok, GROUP_W), lambda b, i: (b, 0, 5)),
            pl.BlockSpec((1, GROUP_W), lambda b, i: (0, 0)),
        ],
        out_specs=pl.BlockSpec((None, ROW_TILE, GROUP_W), lambda b, i: (b, i, 0)),
        out_shape=jax.ShapeDtypeStruct((n_batch, n_tiles * ROW_TILE, GROUP_W), BF16),
        scratch_shapes=[pltpu.VMEM((2 * ROW_TILE, n_keys), F32)],
        compiler_params=_cparams("parallel", "arbitrary"),
        name=name,
    )(df_lambda, pbf, pbf, pbf, g_tiled)


def _df_attention(pbf, df_lambda, g_tiled, lam_init, n_batch, n_lat, n_ctx):
    n_tok = n_lat + n_ctx
    lat_chunks = tuple((st, min(DF_KV, n_tok - st)) for st in range(0, n_tok, DF_KV))
    o_lat = _df_call(pbf, df_lambda, g_tiled, lam_init, n_batch, n_tok, 0, n_lat // ROW_TILE, lat_chunks, "df_attention")
    o_ctx = _df_call(pbf, df_lambda, g_tiled, lam_init, n_batch, n_tok, n_lat // ROW_TILE, n_ctx // ROW_TILE,
                     ((n_lat, n_ctx),), "df_attention_ctx")
    return jnp.concatenate([o_lat, o_ctx], axis=1)


CONV_PAD = 8


def _dn_prep_kernel(u_ref, w_ref, o_ref, pad_ref, *, n_lat, n_ctx):
    j = pl.program_id(1)
    zeros = jnp.zeros((CONV_PAD, GROUP_W), F32)
    lat0 = CONV_PAD
    ctx0 = 2 * CONV_PAD + n_lat
    pad_ref[0:CONV_PAD, :] = zeros
    pad_ref[lat0 + n_lat:ctx0, :] = zeros
    pad_ref[ctx0 + n_ctx:ctx0 + n_ctx + CONV_PAD, :] = zeros
    pad_ref[lat0:lat0 + n_lat, :] = u_ref[0:n_lat, :]
    pad_ref[ctx0:ctx0 + n_ctx, :] = u_ref[n_lat:n_lat + n_ctx, :]
    qscale = jnp.where(j == 0, HEAD_DIM ** -0.5, 1.0)
    half = DN_CONV_K // 2
    for (src0, dst0, n) in ((lat0, 0, n_lat), (ctx0, n_lat, n_ctx)):
        for t0 in range(0, n, ROW_TILE):
            acc = jnp.zeros((ROW_TILE, GROUP_W), F32)
            for tap in range(DN_CONV_K):
                a = src0 + t0 + tap - half
                acc = acc + pad_ref[a:a + ROW_TILE, :] * w_ref[tap:tap + 1, :]
            y = _silu(acc)
            parts = []
            for h in range(GROUP_HEADS):
                yh = y[:, h * HEAD_DIM:(h + 1) * HEAD_DIM]
                nrm = lax.rsqrt(jnp.sum(yh * yh, axis=-1, keepdims=True) + 1e-6) * qscale
                parts.append(yh * jnp.where(j == 2, 1.0, nrm))
            o_ref[dst0 + t0:dst0 + t0 + ROW_TILE, :] = jnp.concatenate(parts, axis=-1)


def _dn_prep(pf, conv_w, n_batch, n_lat, n_ctx):
    n_tok = n_lat + n_ctx
    return pl.pallas_call(
        functools.partial(_dn_prep_kernel, n_lat=n_lat, n_ctx=n_ctx),
        grid=(n_batch, 3),
        in_specs=[
            pl.BlockSpec((None, n_tok, GROUP_W), lambda b, j: (b, 0, j)),
            pl.BlockSpec((DN_CONV_K, GROUP_W), lambda b, j: (0, j)),
        ],
        out_specs=pl.BlockSpec((None, n_tok, GROUP_W), lambda b, j: (b, 0, j)),
        out_shape=jax.ShapeDtypeStruct((n_batch, n_tok, 3 * GROUP_W), F32),
        scratch_shapes=[pltpu.VMEM((n_tok + 3 * CONV_PAD, GROUP_W), F32)],
        compiler_params=_cparams("parallel", "arbitrary"),
        name="dn_prep",
    )(pf, conv_w)


DN_BLOCK = 4 * DN_CHUNK


def _split3(x):
    x1 = x.astype(BF16)
    r = x - x1.astype(F32)
    x2 = r.astype(BF16)
    return x1, x2, (r - x2.astype(F32)).astype(BF16)


def _block_diag(x, masks):
    zero = jnp.zeros_like(x)
    return jnp.concatenate([jnp.where(m, x, zero) for m in masks], axis=0)


def _mm3_heads(lhs, rhs, masks):
    lh, ll = _split(lhs)
    rh, rl = _split(rhs)
    bh, bl = _block_diag(rh, masks), _block_diag(rl, masks)
    return _dot(jnp.concatenate([lh, ll, lh], axis=1), jnp.concatenate([bh, bh, bl], axis=0))


def _dn_kernel(qf_ref, kf_ref, vf_ref, abf_ref, qb_ref, kb_ref, vb_ref, abb_ref, av_ref, dt_ref, exp_ref,
               of_ref, ob_ref, s_ref):
    @pl.when(pl.program_id(1) == 0)
    def _():
        s_ref[...] = jnp.zeros_like(s_ref)

    c, nb, gw = DN_CHUNK, DN_BLOCK, GROUP_W
    n_sub = nb // c
    lane4 = lax.broadcasted_iota(jnp.int32, (c, gw), 1)
    masks = [(lane4 // HEAD_DIM) == h for h in range(GROUP_HEADS)]
    ri = lax.broadcasted_iota(jnp.int32, (c, gw), 0)
    cj = lane4 % HEAD_DIM
    eye = (ri == cj).astype(F32)
    bi = lax.broadcasted_iota(jnp.int32, (nb, nb), 0)
    bj = lax.broadcasted_iota(jnp.int32, (nb, nb), 1)
    same_chunk = (bi // c) == (bj // c)
    lane1 = lax.broadcasted_iota(jnp.int32, (nb, 128), 1)
    dirs = ((qf_ref, kf_ref, vf_ref, abf_ref), (qb_ref, kb_ref, vb_ref, abb_ref))

    chains = []
    for d, (q_ref, k_ref, v_ref, ab_ref) in enumerate(dirs):
        incl = (ri >= cj) if d == 0 else (ri <= cj)
        strict = (ri > cj) if d == 0 else (ri < cj)
        ab = ab_ref[...]
        x = ab + dt_ref[...]
        softplus = jnp.maximum(x, 0.0) + jnp.log1p(jnp.exp(-jnp.abs(x)))
        g = -jnp.exp(av_ref[...]) * softplus
        tri = (same_chunk & ((bi >= bj) if d == 0 else (bi <= bj))).astype(BF16)
        g1, g2, g3 = _split3(g)
        gcs = _dot(tri, g1) + (_dot(tri, g2) + _dot(tri, g3))
        x1, x2, x3 = _split3(jnp.where(lane1 < 2 * GROUP_HEADS, gcs, jax.nn.sigmoid(ab)))
        e = exp_ref[d]
        spread = _dot(x1, e) + (_dot(x2, e) + _dot(x3, e))
        q_all, k_all, v_all = q_ref[...], k_ref[...], v_ref[...]
        last = c - 1 if d == 0 else 0
        for j in range(n_sub):
            rows = slice(j * c, (j + 1) * c)
            gc, beta = spread[rows, :gw], spread[rows, gw:]
            q, k, v = q_all[rows], k_all[rows], v_all[rows]
            gc_row = jnp.sum(gc * eye, axis=0, keepdims=True)
            decay = jnp.exp(jnp.where(incl, gc - gc_row, NEG))
            kbeta = k * beta
            kq = _dot_nt(jnp.concatenate([kbeta, q], axis=0).astype(BF16), _block_diag(k.astype(BF16), masks))
            eg = jnp.exp(gc)
            g_last = gc[last:last + 1, :]
            chains.append(dict(
                lm=jnp.where(strict, kq[:c] * decay, 0.0),
                qk=jnp.where(incl, kq[c:] * decay, 0.0).astype(BF16),
                rhs=jnp.concatenate([_block_diag((v * beta).astype(BF16), masks),
                                     _block_diag((kbeta * eg).astype(BF16), masks)], axis=1),
                q_dec=q * eg,
                k_dec=(k * jnp.exp(g_last - gc)).astype(BF16),
                g_end=jnp.exp(g_last)))

    for ch in chains:
        ch["p"] = eye - ch["lm"]
        ch["sq"] = _mm3_heads(ch["lm"], ch["lm"], masks)
    n = 4
    while n < c:
        for ch in chains:
            r = _mm3_heads(jnp.concatenate([ch["sq"], ch["p"]], axis=0), ch["sq"], masks)
            ch["sq"], ch["p"] = r[:c], ch["p"] + r[c:]
        n *= 2
    for ch in chains:
        t_inv = ch["p"] + _mm3_heads(ch["p"], ch["sq"], masks)
        uw = _dot(t_inv.astype(BF16), ch["rhs"])
        ch["u"] = uw[:, :gw]
        ch["wq"] = jnp.concatenate([uw[:, gw:], ch["q_dec"]], axis=0).astype(BF16)

    states = [s_ref[0], s_ref[1]]
    outs = [[None] * n_sub, [None] * n_sub]
    for step in range(n_sub):
        for d in range(2):
            j = step if d == 0 else n_sub - 1 - step
            ch = chains[d * n_sub + j]
            s = states[d]
            ws = _dot(ch["wq"], _block_diag(s.astype(BF16), masks))
            vb = (ch["u"] - ws[:c]).astype(BF16)
            outs[d][j] = ws[c:] + _dot(ch["qk"], _block_diag(vb, masks))
            full = _dot_tn(ch["k_dec"], vb)
            upd = jnp.where(masks[0], full[0:c], 0.0)
            for h in range(1, GROUP_HEADS):
                upd = upd + jnp.where(masks[h], full[h * c:(h + 1) * c], 0.0)
            states[d] = s * ch["g_end"] + upd
    of_ref[...] = jnp.concatenate(outs[0], axis=0)
    ob_ref[...] = jnp.concatenate(outs[1], axis=0)
    s_ref[...] = jnp.stack(states, axis=0)


def _dn_scan(qkv, pf, a_vec, dt_vec, n_batch, n_lat, n_ctx):
    n_tok = n_lat + n_ctx
    assert n_ctx == DN_BLOCK and n_lat % DN_BLOCK == 0
    nblk = n_tok // DN_BLOCK
    ab_blk = 4 * GROUP_W // 128

    def fwd(n):
        return jnp.where(n == 0, nblk - 1, n - 1)

    def bwd(n):
        return jnp.where(n == 0, nblk - 1, nblk - 1 - n)

    def spec(order, blk, width):
        return pl.BlockSpec((None, DN_BLOCK, width), lambda b, n: (b, order(n), blk))

    spread = np.zeros((2, 128, 2 * GROUP_W), np.float32)
    for d in range(2):
        for h in range(GROUP_HEADS):
            spread[d, GROUP_HEADS * d + h, h * HEAD_DIM:(h + 1) * HEAD_DIM] = 1.0
            spread[d, 2 * GROUP_HEADS + GROUP_HEADS * d + h, GROUP_W + h * HEAD_DIM:GROUP_W + (h + 1) * HEAD_DIM] = 1.0

    lane = pl.BlockSpec((1, 128), lambda b, n: (0, 0))
    out = jax.ShapeDtypeStruct((n_batch, n_tok, GROUP_W), F32)
    return pl.pallas_call(
        _dn_kernel,
        grid=(n_batch, nblk),
        in_specs=[spec(fwd, 0, GROUP_W), spec(fwd, 1, GROUP_W), spec(fwd, 2, GROUP_W), spec(fwd, ab_blk, 128),
                  spec(bwd, 0, GROUP_W), spec(bwd, 1, GROUP_W), spec(bwd, 2, GROUP_W), spec(bwd, ab_blk, 128),
                  lane, lane, pl.BlockSpec((2, 128, 2 * GROUP_W), lambda b, n: (0, 0, 0))],
        out_specs=[spec(fwd, 0, GROUP_W), spec(bwd, 0, GROUP_W)],
        out_shape=[out, out],
        scratch_shapes=[pltpu.VMEM((2, HEAD_DIM, GROUP_W), F32)],
        compiler_params=_cparams("parallel", "arbitrary"),
        name="dn_scan",
    )(qkv, qkv, qkv, pf, qkv, qkv, qkv, pf, a_vec, dt_vec, jnp.asarray(spread, BF16))


FT_N2 = 64


def _ft1_kernel(x_ref, fh_ref, fl_ref, o_ref):
    o_ref[...] = _dot3_const(fh_ref[...], fl_ref[...], x_ref[...])


def _ft_stage1(u_rows, f_hi, f_lo, n_batch, n1, tn=2048):
    width = u_rows.shape[2]
    return pl.pallas_call(
        _ft1_kernel,
        grid=(n_batch, width // tn),
        in_specs=[
            pl.BlockSpec((None, n1, tn), lambda b, j: (b, 0, j)),
            pl.BlockSpec((2 * n1, n1), lambda b, j: (0, 0)),
            pl.BlockSpec((2 * n1, n1), lambda b, j: (0, 0)),
        ],
        out_specs=pl.BlockSpec((None, 2 * n1, tn), lambda b, j: (b, 0, j)),
        out_shape=jax.ShapeDtypeStruct((n_batch, 2 * n1, width), F32),
        compiler_params=_cparams("parallel", "parallel"),
        name="ft_stage1",
    )(u_rows, f_hi, f_lo)


def _ft3_kernel(y_ref, gh_ref, gl_ref, ch_ref, cl_ref, sh_ref, sl_ref, w_ref, o_ref, *, pb, n2, norm):
    vr, vi = [], []
    for p in range(pb):
        rhs = jnp.concatenate([y_ref[0, p], y_ref[1, p]], axis=0)
        v = _dot3_const(gh_ref[p], gl_ref[p], rhs)
        vr.append(v[:n2])
        vi.append(v[n2:])
    vr = jnp.concatenate(vr, axis=0)
    vi = jnp.concatenate(vi, axis=0)
    vrh, vrl = _split(vr)
    vih, vil = _split(vi)
    y = (_dot(vrh, ch_ref[...]) + (_dot(vrl, ch_ref[...]) + _dot(vrh, cl_ref[...]))
         + _dot(vih, sh_ref[...]) + (_dot(vil, sh_ref[...]) + _dot(vih, sl_ref[...]))) * norm
    o = _dot(y.astype(BF16), w_ref[...])
    o_ref[...] = o.reshape(pb, n2, GROUP_W).astype(BF16)


def _ft_stage2(y1, g_hi, g_lo, chan, ft_w, n_batch, n1, n2, pb):
    ch, cl, sh, sl = chan
    full = lambda b, i: (0, 0)
    return pl.pallas_call(
        functools.partial(_ft3_kernel, pb=pb, n2=n2, norm=1.0 / math.sqrt(n1 * n2 * FT_DIM)),
        grid=(n_batch, n1 // pb),
        in_specs=[
            pl.BlockSpec((None, 2, pb, n2, GROUP_W), lambda b, i: (b, 0, i, 0, 0)),
            pl.BlockSpec((pb, 2 * n2, 2 * n2), lambda b, i: (i, 0, 0)),
            pl.BlockSpec((pb, 2 * n2, 2 * n2), lambda b, i: (i, 0, 0)),
            pl.BlockSpec((GROUP_W, GROUP_W), full), pl.BlockSpec((GROUP_W, GROUP_W), full),
            pl.BlockSpec((GROUP_W, GROUP_W), full), pl.BlockSpec((GROUP_W, GROUP_W), full),
            pl.BlockSpec((GROUP_W, GROUP_W), full),
        ],
        out_specs=pl.BlockSpec((None, pb, n2, GROUP_W), lambda b, i: (b, i, 0, 0)),
        out_shape=jax.ShapeDtypeStruct((n_batch, n1, n2, GROUP_W), BF16),
        compiler_params=_cparams("parallel", "parallel"),
        name="ft_stage2",
    )(y1, g_hi, g_lo, ch, cl, sh, sl, ft_w)


def _np_split(a):
    hi = jnp.asarray(a, F32).astype(BF16)
    lo = (jnp.asarray(a, F32) - hi.astype(F32)).astype(BF16)
    return hi, lo


@functools.lru_cache(maxsize=None)
def _ft_tables(n1, n2):
    n = n1 * n2
    p = np.arange(n1)
    ang1 = 2.0 * np.pi * ((p[:, None] * p[None, :]) % n1) / n1
    f1 = np.concatenate([np.cos(ang1), -np.sin(ang1)], axis=0)
    q = np.arange(n2)
    b = np.arange(n2)
    phase = (b[None, None, :] * (p[:, None, None] + n1 * q[None, :, None])) % n
    psi = 2.0 * np.pi * phase / n
    gc, gs = np.cos(psi), np.sin(psi)
    g = np.concatenate([np.concatenate([gc, gs], axis=2), np.concatenate([-gs, gc], axis=2)], axis=1)
    c = np.arange(FT_DIM)
    angc = 2.0 * np.pi * ((c[:, None] * c[None, :]) % FT_DIM) / FT_DIM
    eye = np.eye(GROUP_W // FT_DIM)
    cc, sc = np.kron(eye, np.cos(angc)), np.kron(eye, np.sin(angc))
    return f1.astype(np.float32), g.astype(np.float32), cc.astype(np.float32), sc.astype(np.float32)


def _out_kernel(x_ref, m_ref, na_ref, of_ref, ob_ref, gate_ref, dng_ref, df_ref, ft_ref, w_ref, g2_ref, wr_ref,
                xo_ref, h_ref, aff_ref):
    o = of_ref[...] + ob_ref[...]
    gate = gate_ref[...]
    parts = []
    for h in range(GROUP_HEADS):
        sl = slice(h * HEAD_DIM, (h + 1) * HEAD_DIM)
        oh = o[:, sl]
        parts.append(oh * lax.rsqrt(jnp.mean(oh * oh, axis=-1, keepdims=True) + RMS_EPS))
    y_dn = jnp.concatenate(parts, axis=-1) * dng_ref[...] * _silu(gate)
    mix = jnp.concatenate([na_ref[...], y_dn.astype(BF16), df_ref[...], ft_ref[...]], axis=-1)
    x = x_ref[...] + m_ref[2] * _dot(mix, w_ref[...])
    xo_ref[...] = x
    h2 = _norm_mod(x, g2_ref[...], m_ref[3], m_ref[4])
    bits = pltpu.bitcast(h2.astype(BF16).astype(F32), jnp.uint32)
    half = D_MODEL // 2
    h_ref[...] = bits[:, half:] | (bits[:, :half] >> 16)
    logits = _dot3(h2, wr_ref[...])
    lane = lax.broadcasted_iota(jnp.int32, logits.shape, 1)
    logits = jnp.where(lane < N_EXPERTS, logits, NEG)
    e = jnp.exp(logits - jnp.max(logits, axis=-1, keepdims=True))
    aff_ref[...] = e / jnp.sum(e, axis=-1, keepdims=True)


def _out_proj(x, mods, o_na, o_f, o_b, pf, dn_g, o_df, o_ft, w_out, g2, w_router, n_batch, n_tok):
    nt = n_tok // ROW_TILE
    ctx_tile = nt - 1
    row = lambda b, i: (b, i, 0)
    blk = lambda: pl.BlockSpec((None, ROW_TILE, GROUP_W), row)
    vec = lambda n: pl.BlockSpec((1, n), lambda b, i: (0, 0))
    return pl.pallas_call(
        _out_kernel,
        grid=(n_batch, nt),
        in_specs=[
            pl.BlockSpec((None, ROW_TILE, D_MODEL), row),
            pl.BlockSpec((None, 6, 1, D_MODEL), lambda b, i: (jnp.where(i == ctx_tile, n_batch, b), 0, 0, 0)),
            blk(), blk(), blk(),
            pl.BlockSpec((None, ROW_TILE, GROUP_W), lambda b, i: (b, i, 3)),
            vec(GROUP_W), blk(), blk(),
            pl.BlockSpec((D_MODEL, D_MODEL), lambda b, i: (0, 0)),
            vec(D_MODEL),
            pl.BlockSpec((D_MODEL, 128), lambda b, i: (0, 0)),
        ],
        out_specs=[
            pl.BlockSpec((None, ROW_TILE, D_MODEL), row),
            pl.BlockSpec((None, ROW_TILE, D_MODEL // 2), row),
            pl.BlockSpec((None, ROW_TILE, 128), row),
        ],
        out_shape=[
            jax.ShapeDtypeStruct((n_batch, n_tok, D_MODEL), F32),
            jax.ShapeDtypeStruct((n_batch, n_tok, D_MODEL // 2), jnp.uint32),
            jax.ShapeDtypeStruct((n_batch, n_tok, 128), F32),
        ],
        compiler_params=_cparams("parallel", "parallel"),
        name="out_proj_router",
    )(x, mods, o_na, o_f, o_b, pf, dn_g, o_df, o_ft, w_out, g2.reshape(1, D_MODEL), w_router)


MOE_TF = 256
GATHER_UNROLL = 8


def _gather_kernel(idx_ref, h_ref, o_ref, *, cap, n_e):
    base = (pl.program_id(0) * n_e + pl.program_id(1)) * cap

    def body(j, carry):
        for u in range(GATHER_UNROLL):
            i = j * GATHER_UNROLL + u
            o_ref[pl.ds(i, 1), :] = h_ref[pl.ds(idx_ref[base + i], 1), :]
        return carry

    lax.fori_loop(0, cap // GATHER_UNROLL, body, 0)


def _moe_gather(hp, idx):
    n_batch, n_tok, width = hp.shape
    _, n_e, cap = idx.shape
    assert cap % GATHER_UNROLL == 0
    return pl.pallas_call(
        functools.partial(_gather_kernel, cap=cap, n_e=n_e),
        grid_spec=pltpu.PrefetchScalarGridSpec(
            num_scalar_prefetch=1,
            grid=(n_batch, n_e),
            in_specs=[pl.BlockSpec((None, n_tok, width), lambda b, e, idx_ref: (b, 0, 0))],
            out_specs=pl.BlockSpec((None, cap, width), lambda b, e, idx_ref: (e, b, 0)),
        ),
        out_shape=jax.ShapeDtypeStruct((n_e, n_batch * cap, width), jnp.uint32),
        compiler_params=_cparams("parallel", "arbitrary"),
        name="moe_gather",
    )(idx.reshape(-1), hp)


def _combine_kernel(idx_ref, y_ref, o_ref, *, cap, n_e):
    e = pl.program_id(1)

    @pl.when(e == 0)
    def _():
        o_ref[...] = jnp.zeros_like(o_ref)

    base = (pl.program_id(0) * n_e + e) * cap

    def body(j, carry):
        for u in range(GATHER_UNROLL):
            i = j * GATHER_UNROLL + u
            t = idx_ref[base + i]
            o_ref[pl.ds(t, 1), :] = o_ref[pl.ds(t, 1), :] + y_ref[pl.ds(i, 1), :]
        return carry

    lax.fori_loop(0, cap // GATHER_UNROLL, body, 0)


def _moe_combine(y, idx, n_tok):
    n_batch, n_e, cap = idx.shape
    d = y.shape[2]
    return pl.pallas_call(
        functools.partial(_combine_kernel, cap=cap, n_e=n_e),
        grid_spec=pltpu.PrefetchScalarGridSpec(
            num_scalar_prefetch=1,
            grid=(n_batch, n_e),
            in_specs=[pl.BlockSpec((None, cap, d), lambda b, e, idx_ref: (e, b, 0))],
            out_specs=pl.BlockSpec((None, n_tok, d), lambda b, e, idx_ref: (b, 0, 0)),
        ),
        out_shape=jax.ShapeDtypeStruct((n_batch, n_tok, d), F32),
        compiler_params=_cparams("parallel", "arbitrary"),
        name="moe_combine",
    )(idx.reshape(-1), y)


def _moe_kernel(x_ref, wg_ref, wu_ref, wd_ref, gate_ref, o_ref):
    f = pl.program_id(1)
    packed = x_ref[...]
    x = jnp.concatenate([pltpu.bitcast(packed << 16, F32), pltpu.bitcast(packed & jnp.uint32(0xFFFF0000), F32)],
                        axis=1).astype(BF16)
    a = _dot(x, wg_ref[...].astype(BF16))
    u = _dot(x, wu_ref[...].astype(BF16))
    hid = (_silu(a) * u).astype(BF16)
    y = _dot(hid, wd_ref[...].astype(BF16))

    last = pl.num_programs(1) - 1

    @pl.when(f == 0)
    def _():
        o_ref[...] = y

    @pl.when((f != 0) & (f != last))
    def _():
        o_ref[...] += y

    @pl.when(f == last)
    def _():
        o_ref[...] = (o_ref[...] + y) * gate_ref[...]


def _moe_ffn(xg, w_gate, w_up, w_down, gate, layer):
    n_e, m, _ = xg.shape
    return pl.pallas_call(
        _moe_kernel,
        grid=(n_e, D_EXPERT // MOE_TF),
        in_specs=[
            pl.BlockSpec((None, m, D_MODEL // 2), lambda e, f: (e, 0, 0)),
            pl.BlockSpec((None, None, D_MODEL, MOE_TF), lambda e, f: (layer, e, 0, f)),
            pl.BlockSpec((None, None, D_MODEL, MOE_TF), lambda e, f: (layer, e, 0, f)),
            pl.BlockSpec((None, None, MOE_TF, D_MODEL), lambda e, f: (layer, e, f, 0)),
            pl.BlockSpec((None, m, 1), lambda e, f: (e, 0, 0)),
        ],
        out_specs=pl.BlockSpec((None, m, D_MODEL), lambda e, f: (e, 0, 0)),
        out_shape=jax.ShapeDtypeStruct((n_e, m, D_MODEL), F32),
        compiler_params=_cparams("parallel", "arbitrary"),
        name="moe_ffn",
    )(xg, w_gate, w_up, w_down, gate)


def _final_kernel(x_ref, g_ref, o_ref):
    x = x_ref[...]
    o_ref[...] = x * lax.rsqrt(jnp.mean(x * x, axis=-1, keepdims=True) + RMS_EPS) * g_ref[...]


def _final_norm(x, g, n_batch, n_lat):
    row = lambda b, i: (b, i, 0)
    return pl.pallas_call(
        _final_kernel,
        grid=(n_batch, n_lat // ROW_TILE),
        in_specs=[pl.BlockSpec((None, ROW_TILE, D_MODEL), row), pl.BlockSpec((1, D_MODEL), lambda b, i: (0, 0))],
        out_specs=pl.BlockSpec((None, ROW_TILE, D_MODEL), row),
        out_shape=jax.ShapeDtypeStruct((n_batch, n_lat, D_MODEL), F32),
        compiler_params=_cparams("parallel", "parallel"),
        name="final_norm",
    )(x, g.reshape(1, D_MODEL))


def _reorder_in_columns(w):
    gw = GROUP_W
    o = 6 * gw + 4 * GROUP_HEADS
    pad = jnp.zeros((w.shape[0], 128 - 4 * GROUP_HEADS), w.dtype)
    parts = [w[:, 0:3 * gw], w[:, o + gw:o + 4 * gw], w[:, 3 * gw:6 * gw], w[:, o:o + gw],
             w[:, o + 4 * gw:o + 5 * gw], w[:, 6 * gw:o], pad]
    return jnp.concatenate(parts, axis=1).astype(BF16)


@functools.lru_cache(maxsize=None)
def _rope_tables(n_lat, n_ctx):
    t = np.arange(n_lat)
    pos = np.stack([t // GRID_W, t % GRID_W], axis=-1).astype(np.float32)
    n_freq = DIFF_DIM // 4
    inv = (ROPE_BASE ** (-np.arange(n_freq, dtype=np.float32) / n_freq)).astype(np.float32)
    ang = (pos[:, :, None] * inv).reshape(n_lat, 2 * n_freq)
    lane = np.arange(GROUP_W)
    idx = (lane % DIFF_DIM) // 2
    sign = np.where(lane % 2 == 0, -1.0, 1.0)
    cos = np.concatenate([np.cos(ang)[:, idx], np.ones((n_ctx, GROUP_W))], axis=0)
    sin = np.concatenate([np.sin(ang)[:, idx] * sign, np.zeros((n_ctx, GROUP_W))], axis=0)
    return cos.astype(np.float32), sin.astype(np.float32)


def _lane_vec(v, n=128):
    v = v.reshape(-1).astype(F32)
    return jnp.zeros((1, n), F32).at[0, :v.shape[0]].set(v)


def _route(aff, cap):
    gate, idx = lax.top_k(jnp.swapaxes(aff, 1, 2), cap)
    return gate, idx


def kernel(x, c, ctx, c_ctx, w_mod, b_mod, norm1_g, w_in, na_rpb, dn_conv_w, dn_a_log, dn_dt_bias, dn_norm_g, df_lambda, df_norm_g, ft_w, w_out, norm2_g, w_router, w_gate, w_up, w_down, final_norm_g):
    n_batch, n_lat, _ = x.shape
    n_ctx = ctx.shape[1]
    n_tok = n_lat + n_ctx
    depth = w_mod.shape[0]
    assert n_batch + 1 <= 8 and n_ctx == ROW_TILE and n_lat % ROW_TILE == 0

    xs = jnp.concatenate([x, ctx], axis=1)
    cc = jnp.zeros((8, D_MODEL), F32).at[:n_batch].set(c).at[n_batch].set(c_ctx)
    m_all = _modulation(cc, w_mod, b_mod)

    cos_t, sin_t = (jnp.asarray(a) for a in _rope_tables(n_lat, n_ctx))
    n1 = n_lat // FT_N2
    f1, g_lat, cc_m, sc_m = _ft_tables(n1, FT_N2)
    _, g_ctx, _, _ = _ft_tables(1, n_ctx)
    f1h, f1l = _np_split(f1)
    glh, gll = _np_split(g_lat)
    gch, gcl = _np_split(g_ctx)
    chan = _np_split(cc_m) + _np_split(sc_m)
    cap_lat = EC_FACTOR * n_lat // N_EXPERTS
    cap_ctx = EC_FACTOR * n_ctx // N_EXPERTS

    for l in range(depth):
        lam_init = 0.8 - 0.6 * math.exp(-0.3 * l)
        mods = m_all[l, :n_batch + 1].reshape(n_batch + 1, 6, 1, D_MODEL)
        pbf, pf, pu = _in_proj(xs, norm1_g[l], mods, _reorder_in_columns(w_in[l]), cos_t, sin_t, n_batch, n_tok)

        o_na = _na_attention(pbf, _na_bias_table(na_rpb[l]), n_batch, n_lat, n_ctx)

        qkv = _dn_prep(pf, dn_conv_w[l], n_batch, n_lat, n_ctx)
        o_f, o_b = _dn_scan(qkv, pf, _lane_vec(dn_a_log[l]), _lane_vec(dn_dt_bias[l]), n_batch, n_lat, n_ctx)

        o_df = _df_attention(pbf, df_lambda[l], jnp.tile(df_norm_g[l], GROUP_HEADS).reshape(1, GROUP_W),
                             lam_init, n_batch, n_lat, n_ctx)

        ftw = ft_w[l].astype(BF16)
        y1 = _ft_stage1(pu.reshape(n_batch, n_tok // FT_N2, FT_N2 * GROUP_W), f1h, f1l, n_batch, n1)
        o_lat = _ft_stage2(y1.reshape(n_batch, 2, n1, FT_N2, GROUP_W), glh, gll, chan, ftw, n_batch, n1, FT_N2, 8)
        o_lat = jnp.swapaxes(o_lat, 1, 2).reshape(n_batch, n_lat, GROUP_W)
        u_ctx = pu[:, n_lat:]
        y1c = jnp.stack([u_ctx, jnp.zeros_like(u_ctx)], axis=1)[:, :, None]
        o_ctx = _ft_stage2(y1c, gch, gcl, chan, ftw, n_batch, 1, n_ctx, 1).reshape(n_batch, n_ctx, GROUP_W)
        o_ft = jnp.concatenate([o_lat, o_ctx], axis=1)

        wr = jnp.pad(w_router[l], ((0, 0), (0, 128 - N_EXPERTS)))
        xs, h2, aff = _out_proj(xs, mods, o_na, o_f, o_b, pf, jnp.tile(dn_norm_g[l], GROUP_HEADS).reshape(1, GROUP_W),
                                o_df, o_ft, w_out[l].astype(BF16), norm2_g[l], wr, n_batch, n_tok)

        gate_l, idx_l = _route(aff[:, :n_lat, :N_EXPERTS], cap_lat)
        gate_c, idx_c = _route(aff[:, n_lat:, :N_EXPERTS], cap_ctx)
        idx = jnp.concatenate([idx_l, idx_c + n_lat], axis=2)
        gate = jnp.concatenate([gate_l, gate_c], axis=2)
        cap = cap_lat + cap_ctx
        gate_col = jnp.swapaxes(gate, 0, 1).reshape(N_EXPERTS, n_batch * cap, 1)
        y = _moe_ffn(_moe_gather(h2, idx), w_gate, w_up, w_down, gate_col, l)
        moe = _moe_combine(y, idx, n_tok)
        g_mlp = jnp.concatenate([jnp.broadcast_to(m_all[l, :n_batch, None, 5 * D_MODEL:], (n_batch, n_lat, D_MODEL)),
                                 jnp.broadcast_to(m_all[l, n_batch, 5 * D_MODEL:], (n_batch, n_ctx, D_MODEL))], axis=1)
        xs = xs + g_mlp * moe

    return _final_norm(xs, final_norm_g, n_batch, n_lat)
```

```python
import functools
import math

import numpy as np
import jax
import jax.numpy as jnp
from jax import lax
from jax.experimental import pallas as pl
from jax.experimental.pallas import tpu as pltpu

F32 = jnp.float32
BF16 = jnp.bfloat16

D_MODEL = 1024
DEPTH = 4
GRID_W = 64
GROUP_W = 256
GROUP_HEADS = 4
HEAD_DIM = 64
NA_WIN_H = 8
NA_WIN_W = 16
DN_CONV_K = 5
DN_CHUNK = 64
DIFF_DIM = 32
FT_DIM = 64
N_EXPERTS = 16
EC_FACTOR = 2
D_EXPERT = 2 * D_MODEL
ROPE_BASE = 10000.0
RMS_EPS = 1e-6
NEG = -1e30

ROW_TILE = 256
N_BF = 6 * GROUP_W
N_F32 = 4 * GROUP_W + 128
VMEM_LIMIT = 56 * 1024 * 1024


def _cparams(*sem):
    return pltpu.CompilerParams(dimension_semantics=sem, vmem_limit_bytes=VMEM_LIMIT)


def _split(x):
    hi = x.astype(BF16)
    lo = (x - hi.astype(F32)).astype(BF16)
    return hi, lo


def _dot(a, b):
    return jnp.dot(a, b, preferred_element_type=F32)


def _dot_nt(a, b):
    return lax.dot_general(a, b, (((1,), (1,)), ((), ())), preferred_element_type=F32)


def _dot_tn(a, b):
    return lax.dot_general(a, b, (((0,), (0,)), ((), ())), preferred_element_type=F32)


def _dot3(a, b):
    ah, al = _split(a)
    bh, bl = _split(b)
    return _dot(ah, bh) + (_dot(al, bh) + _dot(ah, bl))


def _dot3_const(ah, al, b):
    bh, bl = _split(b)
    return _dot(ah, bh) + (_dot(al, bh) + _dot(ah, bl))


def _silu(x):
    return x * jax.nn.sigmoid(x)


def _mod_kernel(s_ref, w_ref, b_ref, o_ref):
    s = _silu(s_ref[...])
    o_ref[...] = _dot(s.astype(BF16), w_ref[...].astype(BF16)) + b_ref[...]


def _modulation(cc, w_mod, b_mod):
    depth = w_mod.shape[0]
    nt = 6 * D_MODEL // 1024
    return pl.pallas_call(
        _mod_kernel,
        grid=(depth, nt),
        in_specs=[
            pl.BlockSpec((8, D_MODEL), lambda l, j: (0, 0)),
            pl.BlockSpec((None, D_MODEL, 1024), lambda l, j: (l, 0, j)),
            pl.BlockSpec((None, 1, 1024), lambda l, j: (l, 0, j)),
        ],
        out_specs=pl.BlockSpec((None, 8, 1024), lambda l, j: (l, 0, j)),
        out_shape=jax.ShapeDtypeStruct((depth, 8, 6 * D_MODEL), F32),
        compiler_params=_cparams("parallel", "parallel"),
        name="modulation",
    )(cc, w_mod, b_mod.reshape(depth, 1, 6 * D_MODEL))


def _norm_mod(x, g, shift, scale):
    y = x * lax.rsqrt(jnp.mean(x * x, axis=-1, keepdims=True) + RMS_EPS) * g
    return y * (1.0 + scale) + shift


def _in_kernel(*refs, residual):
    if residual:
        x_ref, moe_ref, mprev_ref, g_ref, m_ref, w_ref, cos_ref, sin_ref, xo_ref, obf_ref, of_ref, ou_ref = refs
        x = x_ref[...] + mprev_ref[5] * moe_ref[...]
        xo_ref[...] = x
    else:
        x_ref, g_ref, m_ref, w_ref, cos_ref, sin_ref, obf_ref, of_ref, ou_ref = refs
        x = x_ref[...]
    h = _norm_mod(x, g_ref[...], m_ref[0], m_ref[1])
    p = _dot(h.astype(BF16), w_ref[...])
    gw = GROUP_W
    lane = lax.broadcasted_iota(jnp.int32, (1, gw), 1)
    even = (lane % 2) == 0
    cos, sin = cos_ref[...], sin_ref[...]

    def rope(t):
        sw = jnp.where(even, pltpu.roll(t, gw - 1, 1), pltpu.roll(t, 1, 1))
        return t * cos + sw * sin

    obf_ref[:, 0:gw] = (p[:, 0:gw] * HEAD_DIM ** -0.5).astype(BF16)
    obf_ref[:, gw:3 * gw] = p[:, gw:3 * gw].astype(BF16)
    obf_ref[:, 3 * gw:4 * gw] = (rope(p[:, 3 * gw:4 * gw]) * DIFF_DIM ** -0.5).astype(BF16)
    obf_ref[:, 4 * gw:5 * gw] = rope(p[:, 4 * gw:5 * gw]).astype(BF16)
    obf_ref[:, 5 * gw:6 * gw] = p[:, 5 * gw:6 * gw].astype(BF16)
    of_ref[:, 0:4 * gw] = p[:, 6 * gw:10 * gw]
    of_ref[:, 4 * gw:4 * gw + 128] = p[:, 11 * gw:11 * gw + 128]
    ou_ref[...] = p[:, 10 * gw:11 * gw]


def _in_proj(x, prev, g, mods, w_perm, cos_t, sin_t, n_batch, n_tok):
    nt = n_tok // ROW_TILE
    ctx_tile = nt - 1
    n_out = w_perm.shape[1]
    row = lambda b, i: (b, i, 0)
    x_spec = pl.BlockSpec((None, ROW_TILE, D_MODEL), row)
    mod_spec = pl.BlockSpec((None, 6, 1, D_MODEL), lambda b, i: (jnp.where(i == ctx_tile, n_batch, b), 0, 0, 0))
    in_specs = [
        pl.BlockSpec((1, D_MODEL), lambda b, i: (0, 0)),
        mod_spec,
        pl.BlockSpec((D_MODEL, n_out), lambda b, i: (0, 0)),
        pl.BlockSpec((ROW_TILE, GROUP_W), lambda b, i: (i, 0)),
        pl.BlockSpec((ROW_TILE, GROUP_W), lambda b, i: (i, 0)),
    ]
    out_specs = [
        pl.BlockSpec((None, ROW_TILE, N_BF), row),
        pl.BlockSpec((None, ROW_TILE, N_F32), row),
        pl.BlockSpec((None, ROW_TILE, GROUP_W), row),
    ]
    out_shape = [
        jax.ShapeDtypeStruct((n_batch, n_tok, N_BF), BF16),
        jax.ShapeDtypeStruct((n_batch, n_tok, N_F32), F32),
        jax.ShapeDtypeStruct((n_batch, n_tok, GROUP_W), F32),
    ]
    args = (g.reshape(1, D_MODEL), mods, w_perm, cos_t, sin_t)
    if prev is None:
        in_specs, args = [x_spec] + in_specs, (x,) + args
    else:
        in_specs, args = [x_spec, x_spec, mod_spec] + in_specs, (x,) + tuple(prev) + args
        out_specs = [x_spec] + out_specs
        out_shape = [jax.ShapeDtypeStruct((n_batch, n_tok, D_MODEL), F32)] + out_shape
    outs = pl.pallas_call(
        functools.partial(_in_kernel, residual=prev is not None),
        grid=(n_batch, nt),
        in_specs=in_specs,
        out_specs=out_specs,
        out_shape=out_shape,
        compiler_params=_cparams("parallel", "parallel"),
        name="in_proj",
    )(*args)
    return tuple(outs) if prev is not None else (x,) + tuple(outs)


def _na_kernel(q_ref, k_ref, v_ref, bias_ref, o_ref, *, n_lat, n_ctx):
    i = pl.program_id(1)
    rows = n_lat // GRID_W
    wh = min(NA_WIN_H, rows)
    n_win = wh * GRID_W
    per = ROW_TILE // GRID_W
    kc = k_ref[n_lat:n_lat + n_ctx, :]
    vc = v_ref[n_lat:n_lat + n_ctx, :]
    lane = lax.broadcasted_iota(jnp.int32, (GRID_W, GROUP_W), 1)
    masks = [lane // HEAD_DIM == h for h in range(GROUP_HEADS)]
    for u in range(per):
        r = i * per + u
        is_lat = r < rows
        base = jnp.where(is_lat, jnp.clip(r - wh // 2, 0, rows - wh), 0)
        slab = jnp.where(is_lat, base - r + NA_WIN_H - 1, NA_WIN_H)
        start = pl.multiple_of(base * GRID_W, GRID_W)
        q = q_ref[u * GRID_W:(u + 1) * GRID_W, :]
        qm = jnp.concatenate([jnp.where(m, q, jnp.zeros_like(q)) for m in masks], axis=0)
        keys = jnp.concatenate([k_ref[pl.ds(start, n_win), :], kc], axis=0)
        vals = jnp.concatenate([v_ref[pl.ds(start, n_win), :], vc], axis=0)
        s = _dot_nt(qm, keys)
        sw = s[:, :n_win] + bias_ref[slab].reshape(GROUP_HEADS * GRID_W, n_win)
        sc = s[:, n_win:]
        m = jnp.maximum(jnp.max(sw, axis=-1, keepdims=True), jnp.max(sc, axis=-1, keepdims=True))
        pw = jnp.exp(sw - m)
        pc = jnp.exp(sc - m)
        l = jnp.sum(pw, axis=-1, keepdims=True) + jnp.sum(pc, axis=-1, keepdims=True)
        res = _dot(jnp.concatenate([pw, pc], axis=1).astype(BF16), vals) / l
        o = jnp.where(masks[0], res[0:GRID_W], 0.0)
        for h in range(1, GROUP_HEADS):
            o = o + jnp.where(masks[h], res[h * GRID_W:(h + 1) * GRID_W], 0.0)
        o_ref[u * GRID_W:(u + 1) * GRID_W, :] = o.astype(BF16)


def _na_attention(pbf, bias_tab, n_batch, n_lat, n_ctx):
    n_tok = n_lat + n_ctx
    return pl.pallas_call(
        functools.partial(_na_kernel, n_lat=n_lat, n_ctx=n_ctx),
        grid=(n_batch, n_tok // ROW_TILE),
        in_specs=[
            pl.BlockSpec((None, ROW_TILE, GROUP_W), lambda b, i: (b, i, 0)),
            pl.BlockSpec((None, n_tok, GROUP_W), lambda b, i: (b, 0, 1)),
            pl.BlockSpec((None, n_tok, GROUP_W), lambda b, i: (b, 0, 2)),
            pl.BlockSpec(bias_tab.shape, lambda b, i: (0, 0, 0, 0)),
        ],
        out_specs=pl.BlockSpec((None, ROW_TILE, GROUP_W), lambda b, i: (b, i, 0)),
        out_shape=jax.ShapeDtypeStruct((n_batch, n_tok, GROUP_W), BF16),
        compiler_params=_cparams("parallel", "arbitrary"),
        name="na_attention",
    )(pbf, pbf, pbf, bias_tab)


def _na_bias_table(rpb):
    wh = NA_WIN_H
    cq = np.arange(GRID_W)
    c0 = np.clip(cq - NA_WIN_W // 2, 0, GRID_W - NA_WIN_W)
    in_win = (cq[None, :] >= c0[:, None]) & (cq[None, :] < c0[:, None] + NA_WIN_W)
    dx = np.clip(cq[None, :] - cq[:, None], 1 - NA_WIN_W, NA_WIN_W - 1) + NA_WIN_W - 1
    dy = np.arange(wh)[:, None] + np.arange(wh)[None, :]
    pick_x = (dx[:, :, None] == np.arange(2 * NA_WIN_W - 1)).astype(np.float32)
    pick_y = (dy[:, :, None] == np.arange(2 * NA_WIN_H - 1)).astype(np.float32)
    hi = lax.Precision.HIGHEST
    cols = jnp.einsum('hyx,qkx->hyqk', rpb.astype(F32), pick_x, precision=hi)
    tab = jnp.einsum('swy,hyqk->shqwk', pick_y, cols, precision=hi)
    tab = jnp.where(in_win[None, None, :, None, :], tab, NEG)
    tab = tab.reshape(wh, GROUP_HEADS, GRID_W, wh * GRID_W)
    return jnp.concatenate([tab, jnp.full((1,) + tab.shape[1:], NEG, F32)], axis=0)


DF_KV = 512


def _df_kernel(lam_ref, q_ref, k_ref, v_ref, g_ref, o_ref, s_ref, *, chunks, lam_init):
    lq1, lk1, lq2, lk2 = lam_ref[0:1, :], lam_ref[1:2, :], lam_ref[2:3, :], lam_ref[3:4, :]
    lam = (jnp.exp(jnp.sum(lq1 * lk1, axis=-1, keepdims=True))
           - jnp.exp(jnp.sum(lq2 * lk2, axis=-1, keepdims=True)) + lam_init)
    q = q_ref[...]
    tq, gw = q.shape
    lane = lax.broadcasted_iota(jnp.int32, (tq, gw), 1)
    row2 = lax.broadcasted_iota(jnp.int32, (2 * tq, 1), 0)
    origin = chunks[0][0]
    zero = jnp.zeros_like(q)

    def head(h, out):
        qm = [jnp.where(lane // DIFF_DIM == 2 * h + mp, q, zero) for mp in range(2)]
        m = [jnp.full((tq, 128), NEG, F32) for _ in range(2)]
        for st, sz in chunks:
            k = k_ref[st:st + sz, :]
            for mp in range(2):
                s = _dot_nt(qm[mp], k)
                s_ref[mp * tq:(mp + 1) * tq, st - origin:st - origin + sz] = s
                for t in range(sz // 128):
                    m[mp] = jnp.maximum(m[mp], s[:, t * 128:(t + 1) * 128])
        m = [jnp.max(x, axis=-1, keepdims=True) for x in m]
        ones_lane = ((h + 1) % GROUP_HEADS) * HEAD_DIM
        acc = [jnp.zeros((tq, gw), F32) for _ in range(2)]
        for st, sz in chunks:
            v = v_ref[st:st + sz, :]
            lane_v = lax.broadcasted_iota(jnp.int32, v.shape, 1)
            v = jnp.where(lane_v == ones_lane, jnp.ones_like(v), v)
            for mp in range(2):
                e = jnp.exp((s_ref[mp * tq:(mp + 1) * tq, st - origin:st - origin + sz] - m[mp]).astype(BF16))
                acc[mp] = acc[mp] + _dot(e, v)
        w = []
        for mp in range(2):
            l = jnp.sum(jnp.where(lane == ones_lane, acc[mp], 0.0), axis=-1, keepdims=True)
            w.append(acc[mp] * ((1.0 if mp == 0 else lam) / l))
        return out + jnp.where(lane // HEAD_DIM == h, w[0] - w[1], 0.0)

    o = lax.fori_loop(0, GROUP_HEADS, head, jnp.zeros((tq, gw), F32))
    sq = o * o
    scale = jnp.zeros((tq, gw), F32)
    for h in range(GROUP_HEADS):
        in_head = lane // HEAD_DIM == h
        ms = jnp.sum(jnp.where(in_head, sq, 0.0), axis=-1, keepdims=True) * (1.0 / HEAD_DIM)
        scale = jnp.where(in_head, lax.rsqrt(ms + RMS_EPS), scale)
    o_ref[...] = (o * scale * g_ref[...] * (1.0 - lam_init)).astype(BF16)


def _df_call(pbf, df_lambda, g_tiled, lam_init, n_batch, n_tok, first_tile, n_tiles, chunks, name):
    n_keys = sum(sz for _, sz in chunks)
    return pl.pallas_call(
        functools.partial(_df_kernel, chunks=chunks, lam_init=lam_init),
        grid=(n_batch, n_tiles),
        in_specs=[
            pl.BlockSpec((4, DIFF_DIM), lambda b, i: (0, 0)),
            pl.BlockSpec((None, ROW_TILE, GROUP_W), lambda b, i: (b, first_tile + i, 3)),
            pl.BlockSpec((None, n_tok, GROUP_W), lambda b, i: (b, 0, 4)),
            pl.BlockSpec((None, n_tok, GROUP_W), lambda b, i: (b, 0, 5)),
            pl.BlockSpec((1, GROUP_W), lambda b, i: (0, 0)),
        ],
        out_specs=pl.BlockSpec((None, ROW_TILE, GROUP_W), lambda b, i: (b, i, 0)),
        out_shape=jax.ShapeDtypeStruct((n_batch, n_tiles * ROW_TILE, GROUP_W), BF16),
        scratch_shapes=[pltpu.VMEM((2 * ROW_TILE, n_keys), F32)],
        compiler_params=_cparams("parallel", "arbitrary"),
        name=name,
    )(df_lambda, pbf, pbf, pbf, g_tiled)


def _df_attention(pbf, df_lambda, g_tiled, lam_init, n_batch, n_lat, n_ctx):
    n_tok = n_lat + n_ctx
    lat_chunks = tuple((st, min(DF_KV, n_tok - st)) for st in range(0, n_tok, DF_KV))
    o_lat = _df_call(pbf, df_lambda, g_tiled, lam_init, n_batch, n_tok, 0, n_lat // ROW_TILE, lat_chunks, "df_attention")
    o_ctx = _df_call(pbf, df_lambda, g_tiled, lam_init, n_batch, n_tok, n_lat // ROW_TILE, n_ctx // ROW_TILE,
                     ((n_lat, n_ctx),), "df_attention_ctx")
    return jnp.concatenate([o_lat, o_ctx], axis=1)


CONV_PAD = 8


def _dn_prep_kernel(u_ref, w_ref, o_ref, pad_ref, *, n_lat, n_ctx):
    j = pl.program_id(1)
    zeros = jnp.zeros((CONV_PAD, GROUP_W), F32)
    lat0 = CONV_PAD
    ctx0 = 2 * CONV_PAD + n_lat
    pad_ref[0:CONV_PAD, :] = zeros
    pad_ref[lat0 + n_lat:ctx0, :] = zeros
    pad_ref[ctx0 + n_ctx:ctx0 + n_ctx + CONV_PAD, :] = zeros
    pad_ref[lat0:lat0 + n_lat, :] = u_ref[0:n_lat, :]
    pad_ref[ctx0:ctx0 + n_ctx, :] = u_ref[n_lat:n_lat + n_ctx, :]
    qscale = jnp.where(j == 0, HEAD_DIM ** -0.5, 1.0)
    half = DN_CONV_K // 2
    for (src0, dst0, n) in ((lat0, 0, n_lat), (ctx0, n_lat, n_ctx)):
        for t0 in range(0, n, ROW_TILE):
            acc = jnp.zeros((ROW_TILE, GROUP_W), F32)
            for tap in range(DN_CONV_K):
                a = src0 + t0 + tap - half
                acc = acc + pad_ref[a:a + ROW_TILE, :] * w_ref[tap:tap + 1, :]
            y = _silu(acc)
            parts = []
            for h in range(GROUP_HEADS):
                yh = y[:, h * HEAD_DIM:(h + 1) * HEAD_DIM]
                nrm = lax.rsqrt(jnp.sum(yh * yh, axis=-1, keepdims=True) + 1e-6) * qscale
                parts.append(yh * jnp.where(j == 2, 1.0, nrm))
            o_ref[dst0 + t0:dst0 + t0 + ROW_TILE, :] = jnp.concatenate(parts, axis=-1)


def _dn_prep(pf, conv_w, n_batch, n_lat, n_ctx):
    n_tok = n_lat + n_ctx
    return pl.pallas_call(
        functools.partial(_dn_prep_kernel, n_lat=n_lat, n_ctx=n_ctx),
        grid=(n_batch, 3),
        in_specs=[
            pl.BlockSpec((None, n_tok, GROUP_W), lambda b, j: (b, 0, j)),
            pl.BlockSpec((DN_CONV_K, GROUP_W), lambda b, j: (0, j)),
        ],
        out_specs=pl.BlockSpec((None, n_tok, GROUP_W), lambda b, j: (b, 0, j)),
        out_shape=jax.ShapeDtypeStruct((n_batch, n_tok, 3 * GROUP_W), F32),
        scratch_shapes=[pltpu.VMEM((n_tok + 3 * CONV_PAD, GROUP_W), F32)],
        compiler_params=_cparams("parallel", "arbitrary"),
        name="dn_prep",
    )(pf, conv_w)


DN_BLOCK = 4 * DN_CHUNK


def _split3(x):
    x1 = x.astype(BF16)
    r = x - x1.astype(F32)
    x2 = r.astype(BF16)
    return x1, x2, (r - x2.astype(F32)).astype(BF16)


def _block_diag(x, masks):
    zero = jnp.zeros_like(x)
    return jnp.concatenate([jnp.where(m, x, zero) for m in masks], axis=0)


def _mm3_heads(lhs, rhs, masks):
    lh, ll = _split(lhs)
    rh, rl = _split(rhs)
    bh, bl = _block_diag(rh, masks), _block_diag(rl, masks)
    return _dot(jnp.concatenate([lh, ll, lh], axis=1), jnp.concatenate([bh, bh, bl], axis=0))


def _dn_kernel(qf_ref, kf_ref, vf_ref, abf_ref, qb_ref, kb_ref, vb_ref, abb_ref, av_ref, dt_ref, exp_ref,
               of_ref, ob_ref, s_ref):
    @pl.when(pl.program_id(1) == 0)
    def _():
        s_ref[...] = jnp.zeros_like(s_ref)

    c, nb, gw = DN_CHUNK, DN_BLOCK, GROUP_W
    n_sub = nb // c
    lane4 = lax.broadcasted_iota(jnp.int32, (c, gw), 1)
    masks = [(lane4 // HEAD_DIM) == h for h in range(GROUP_HEADS)]
    ri = lax.broadcasted_iota(jnp.int32, (c, gw), 0)
    cj = lane4 % HEAD_DIM
    eye = (ri == cj).astype(F32)
    bi = lax.broadcasted_iota(jnp.int32, (nb, nb), 0)
    bj = lax.broadcasted_iota(jnp.int32, (nb, nb), 1)
    same_chunk = (bi // c) == (bj // c)
    lane1 = lax.broadcasted_iota(jnp.int32, (nb, 128), 1)
    dirs = ((qf_ref, kf_ref, vf_ref, abf_ref), (qb_ref, kb_ref, vb_ref, abb_ref))

    chains = []
    for d, (q_ref, k_ref, v_ref, ab_ref) in enumerate(dirs):
        incl = (ri >= cj) if d == 0 else (ri <= cj)
        strict = (ri > cj) if d == 0 else (ri < cj)
        ab = ab_ref[...]
        x = ab + dt_ref[...]
        softplus = jnp.maximum(x, 0.0) + jnp.log1p(jnp.exp(-jnp.abs(x)))
        g = -jnp.exp(av_ref[...]) * softplus
        tri = (same_chunk & ((bi >= bj) if d == 0 else (bi <= bj))).astype(BF16)
        g1, g2, g3 = _split3(g)
        gcs = _dot(tri, g1) + (_dot(tri, g2) + _dot(tri, g3))
        x1, x2, x3 = _split3(jnp.where(lane1 < 2 * GROUP_HEADS, gcs, jax.nn.sigmoid(ab)))
        e = exp_ref[d]
        spread = _dot(x1, e) + (_dot(x2, e) + _dot(x3, e))
        q_all, k_all, v_all = q_ref[...], k_ref[...], v_ref[...]
        last = c - 1 if d == 0 else 0
        for j in range(n_sub):
            rows = slice(j * c, (j + 1) * c)
            gc, beta = spread[rows, :gw], spread[rows, gw:]
            q, k, v = q_all[rows], k_all[rows], v_all[rows]
            gc_row = jnp.sum(gc * eye, axis=0, keepdims=True)
            decay = jnp.exp(jnp.where(incl, gc - gc_row, NEG))
            kbeta = k * beta
            kq = _dot_nt(jnp.concatenate([kbeta, q], axis=0).astype(BF16), _block_diag(k.astype(BF16), masks))
            eg = jnp.exp(gc)
            g_last = gc[last:last + 1, :]
            chains.append(dict(
                lm=jnp.where(strict, kq[:c] * decay, 0.0),
                qk=jnp.where(incl, kq[c:] * decay, 0.0).astype(BF16),
                rhs=jnp.concatenate([_block_diag((v * beta).astype(BF16), masks),
                                     _block_diag((kbeta * eg).astype(BF16), masks)], axis=1),
                q_dec=q * eg,
                k_dec=(k * jnp.exp(g_last - gc)).astype(BF16),
                g_end=jnp.exp(g_last)))

    def solve(chs):
        for ch in chs:
            ch["p"] = eye - ch["lm"]
            ch["sq"] = _mm3_heads(ch["lm"], ch["lm"], masks)
        n = 4
        while n < c:
            for ch in chs:
                r = _mm3_heads(jnp.concatenate([ch["sq"], ch["p"]], axis=0), ch["sq"], masks)
                ch["sq"], ch["p"] = r[:c], ch["p"] + r[c:]
            n *= 2
        for ch in chs:
            t_inv = ch["p"] + _mm3_heads(ch["p"], ch["sq"], masks)
            uw = _dot(t_inv.astype(BF16), ch["rhs"])
            ch["u"] = uw[:, :gw]
            ch["wq"] = jnp.concatenate([uw[:, gw:], ch["q_dec"]], axis=0).astype(BF16)

    states = [s_ref[0], s_ref[1]]
    outs = [[None] * n_sub, [None] * n_sub]

    def chain_at(step, d):
        return d * n_sub + (step if d == 0 else n_sub - 1 - step)

    def advance(step):
        for d in range(2):
            ch = chains[chain_at(step, d)]
            s = states[d]
            ws = _dot(ch["wq"], _block_diag(s.astype(BF16), masks))
            vb = (ch["u"] - ws[:c]).astype(BF16)
            outs[d][chain_at(step, d) - d * n_sub] = ws[c:] + _dot(ch["qk"], _block_diag(vb, masks))
            full = _dot_tn(ch["k_dec"], vb)
            upd = jnp.where(masks[0], full[0:c], 0.0)
            for h in range(1, GROUP_HEADS):
                upd = upd + jnp.where(masks[h], full[h * c:(h + 1) * c], 0.0)
            states[d] = s * ch["g_end"] + upd

    solve(chains)
    for step in range(n_sub):
        advance(step)
    of_ref[...] = jnp.concatenate(outs[0], axis=0)
    ob_ref[...] = jnp.concatenate(outs[1], axis=0)
    s_ref[...] = jnp.stack(states, axis=0)


def _dn_scan(qkv, pf, a_vec, dt_vec, n_batch, n_lat, n_ctx):
    n_tok = n_lat + n_ctx
    assert n_ctx == DN_BLOCK and n_lat % DN_BLOCK == 0
    nblk = n_tok // DN_BLOCK
    ab_blk = 4 * GROUP_W // 128

    def fwd(n):
        return jnp.where(n == 0, nblk - 1, n - 1)

    def bwd(n):
        return jnp.where(n == 0, nblk - 1, nblk - 1 - n)

    def spec(order, blk, width):
        return pl.BlockSpec((None, DN_BLOCK, width), lambda b, n: (b, order(n), blk))

    spread = np.zeros((2, 128, 2 * GROUP_W), np.float32)
    for d in range(2):
        for h in range(GROUP_HEADS):
            spread[d, GROUP_HEADS * d + h, h * HEAD_DIM:(h + 1) * HEAD_DIM] = 1.0
            spread[d, 2 * GROUP_HEADS + GROUP_HEADS * d + h, GROUP_W + h * HEAD_DIM:GROUP_W + (h + 1) * HEAD_DIM] = 1.0

    lane = pl.BlockSpec((1, 128), lambda b, n: (0, 0))
    out = jax.ShapeDtypeStruct((n_batch, n_tok, GROUP_W), F32)
    return pl.pallas_call(
        _dn_kernel,
        grid=(n_batch, nblk),
        in_specs=[spec(fwd, 0, GROUP_W), spec(fwd, 1, GROUP_W), spec(fwd, 2, GROUP_W), spec(fwd, ab_blk, 128),
                  spec(bwd, 0, GROUP_W), spec(bwd, 1, GROUP_W), spec(bwd, 2, GROUP_W), spec(bwd, ab_blk, 128),
                  lane, lane, pl.BlockSpec((2, 128, 2 * GROUP_W), lambda b, n: (0, 0, 0))],
        out_specs=[spec(fwd, 0, GROUP_W), spec(bwd, 0, GROUP_W)],
        out_shape=[out, out],
        scratch_shapes=[pltpu.VMEM((2, HEAD_DIM, GROUP_W), F32)],
        compiler_params=_cparams("parallel", "arbitrary"),
        name="dn_scan",
    )(qkv, qkv, qkv, pf, qkv, qkv, qkv, pf, a_vec, dt_vec, jnp.asarray(spread, BF16))


FT_N2 = 64


def _ft1_kernel(x_ref, fh_ref, fl_ref, o_ref):
    o_ref[...] = _dot3_const(fh_ref[...], fl_ref[...], x_ref[...])


def _ft_stage1(u_rows, f_hi, f_lo, n_batch, n1, tn=2048):
    width = u_rows.shape[2]
    return pl.pallas_call(
        _ft1_kernel,
        grid=(n_batch, width // tn),
        in_specs=[
            pl.BlockSpec((None, n1, tn), lambda b, j: (b, 0, j)),
            pl.BlockSpec((2 * n1, n1), lambda b, j: (0, 0)),
            pl.BlockSpec((2 * n1, n1), lambda b, j: (0, 0)),
        ],
        out_specs=pl.BlockSpec((None, 2 * n1, tn), lambda b, j: (b, 0, j)),
        out_shape=jax.ShapeDtypeStruct((n_batch, 2 * n1, width), F32),
        compiler_params=_cparams("parallel", "parallel"),
        name="ft_stage1",
    )(u_rows, f_hi, f_lo)


def _ft3_kernel(y_ref, gh_ref, gl_ref, ch_ref, cl_ref, sh_ref, sl_ref, w_ref, o_ref, *, pb, n2, norm):
    vr, vi = [], []
    for p in range(pb):
        rhs = jnp.concatenate([y_ref[0, p], y_ref[1, p]], axis=0)
        v = _dot3_const(gh_ref[p], gl_ref[p], rhs)
        vr.append(v[:n2])
        vi.append(v[n2:])
    vr = jnp.concatenate(vr, axis=0)
    vi = jnp.concatenate(vi, axis=0)
    vrh, vrl = _split(vr)
    vih, vil = _split(vi)
    y = (_dot(vrh, ch_ref[...]) + (_dot(vrl, ch_ref[...]) + _dot(vrh, cl_ref[...]))
         + _dot(vih, sh_ref[...]) + (_dot(vil, sh_ref[...]) + _dot(vih, sl_ref[...]))) * norm
    o = _dot(y.astype(BF16), w_ref[...])
    o_ref[...] = o.reshape(pb, n2, GROUP_W).astype(BF16)


def _ft_stage2(y1, g_hi, g_lo, chan, ft_w, n_batch, n1, n2, pb):
    ch, cl, sh, sl = chan
    full = lambda b, i: (0, 0)
    return pl.pallas_call(
        functools.partial(_ft3_kernel, pb=pb, n2=n2, norm=1.0 / math.sqrt(n1 * n2 * FT_DIM)),
        grid=(n_batch, n1 // pb),
        in_specs=[
            pl.BlockSpec((None, 2, pb, n2, GROUP_W), lambda b, i: (b, 0, i, 0, 0)),
            pl.BlockSpec((pb, 2 * n2, 2 * n2), lambda b, i: (i, 0, 0)),
            pl.BlockSpec((pb, 2 * n2, 2 * n2), lambda b, i: (i, 0, 0)),
            pl.BlockSpec((GROUP_W, GROUP_W), full), pl.BlockSpec((GROUP_W, GROUP_W), full),
            pl.BlockSpec((GROUP_W, GROUP_W), full), pl.BlockSpec((GROUP_W, GROUP_W), full),
            pl.BlockSpec((GROUP_W, GROUP_W), full),
        ],
        out_specs=pl.BlockSpec((None, pb, n2, GROUP_W), lambda b, i: (b, i, 0, 0)),
        out_shape=jax.ShapeDtypeStruct((n_batch, n1, n2, GROUP_W), BF16),
        compiler_params=_cparams("parallel", "parallel"),
        name="ft_stage2",
    )(y1, g_hi, g_lo, ch, cl, sh, sl, ft_w)


def _np_split(a):
    hi = jnp.asarray(a, F32).astype(BF16)
    lo = (jnp.asarray(a, F32) - hi.astype(F32)).astype(BF16)
    return hi, lo


@functools.lru_cache(maxsize=None)
def _ft_tables(n1, n2):
    n = n1 * n2
    p = np.arange(n1)
    ang1 = 2.0 * np.pi * ((p[:, None] * p[None, :]) % n1) / n1
    f1 = np.concatenate([np.cos(ang1), -np.sin(ang1)], axis=0)
    q = np.arange(n2)
    b = np.arange(n2)
    phase = (b[None, None, :] * (p[:, None, None] + n1 * q[None, :, None])) % n
    psi = 2.0 * np.pi * phase / n
    gc, gs = np.cos(psi), np.sin(psi)
    g = np.concatenate([np.concatenate([gc, gs], axis=2), np.concatenate([-gs, gc], axis=2)], axis=1)
    c = np.arange(FT_DIM)
    angc = 2.0 * np.pi * ((c[:, None] * c[None, :]) % FT_DIM) / FT_DIM
    eye = np.eye(GROUP_W // FT_DIM)
    cc, sc = np.kron(eye, np.cos(angc)), np.kron(eye, np.sin(angc))
    return f1.astype(np.float32), g.astype(np.float32), cc.astype(np.float32), sc.astype(np.float32)


def _out_kernel(x_ref, m_ref, na_ref, of_ref, ob_ref, gate_ref, dng_ref, df_ref, ft_ref, w_ref, g2_ref, wr_ref,
                xo_ref, h_ref, aff_ref):
    o = of_ref[...] + ob_ref[...]
    gate = gate_ref[...]
    parts = []
    for h in range(GROUP_HEADS):
        sl = slice(h * HEAD_DIM, (h + 1) * HEAD_DIM)
        oh = o[:, sl]
        parts.append(oh * lax.rsqrt(jnp.mean(oh * oh, axis=-1, keepdims=True) + RMS_EPS))
    y_dn = jnp.concatenate(parts, axis=-1) * dng_ref[...] * _silu(gate)
    mix = jnp.concatenate([na_ref[...], y_dn.astype(BF16), df_ref[...], ft_ref[...]], axis=-1)
    x = x_ref[...] + m_ref[2] * _dot(mix, w_ref[...])
    xo_ref[...] = x
    h2 = _norm_mod(x, g2_ref[...], m_ref[3], m_ref[4])
    bits = pltpu.bitcast(h2.astype(BF16).astype(F32), jnp.uint32)
    half = D_MODEL // 2
    h_ref[...] = bits[:, half:] | (bits[:, :half] >> 16)
    logits = _dot3(h2, wr_ref[...])
    lane = lax.broadcasted_iota(jnp.int32, logits.shape, 1)
    logits = jnp.where(lane < N_EXPERTS, logits, NEG)
    e = jnp.exp(logits - jnp.max(logits, axis=-1, keepdims=True))
    aff_ref[...] = e / jnp.sum(e, axis=-1, keepdims=True)


def _out_proj(x, mods, o_na, o_f, o_b, pf, dn_g, o_df, o_ft, w_out, g2, w_router, n_batch, n_tok):
    nt = n_tok // ROW_TILE
    ctx_tile = nt - 1
    row = lambda b, i: (b, i, 0)
    blk = lambda: pl.BlockSpec((None, ROW_TILE, GROUP_W), row)
    vec = lambda n: pl.BlockSpec((1, n), lambda b, i: (0, 0))
    return pl.pallas_call(
        _out_kernel,
        grid=(n_batch, nt),
        in_specs=[
            pl.BlockSpec((None, ROW_TILE, D_MODEL), row),
            pl.BlockSpec((None, 6, 1, D_MODEL), lambda b, i: (jnp.where(i == ctx_tile, n_batch, b), 0, 0, 0)),
            blk(), blk(), blk(),
            pl.BlockSpec((None, ROW_TILE, GROUP_W), lambda b, i: (b, i, 3)),
            vec(GROUP_W), blk(), blk(),
            pl.BlockSpec((D_MODEL, D_MODEL), lambda b, i: (0, 0)),
            vec(D_MODEL),
            pl.BlockSpec((D_MODEL, 128), lambda b, i: (0, 0)),
        ],
        out_specs=[
            pl.BlockSpec((None, ROW_TILE, D_MODEL), row),
            pl.BlockSpec((None, ROW_TILE, D_MODEL // 2), row),
            pl.BlockSpec((None, ROW_TILE, 128), row),
        ],
        out_shape=[
            jax.ShapeDtypeStruct((n_batch, n_tok, D_MODEL), F32),
            jax.ShapeDtypeStruct((n_batch, n_tok, D_MODEL // 2), jnp.uint32),
            jax.ShapeDtypeStruct((n_batch, n_tok, 128), F32),
        ],
        compiler_params=_cparams("parallel", "parallel"),
        name="out_proj_router",
    )(x, mods, o_na, o_f, o_b, pf, dn_g, o_df, o_ft, w_out, g2.reshape(1, D_MODEL), w_router)


MOE_TF = 256
GATHER_UNROLL = 8


def _gather_kernel(idx_ref, h_ref, o_ref, *, cap, n_e):
    base = (pl.program_id(0) * n_e + pl.program_id(1)) * cap

    def body(j, carry):
        for u in range(GATHER_UNROLL):
            i = j * GATHER_UNROLL + u
            o_ref[pl.ds(i, 1), :] = h_ref[pl.ds(idx_ref[base + i], 1), :]
        return carry

    lax.fori_loop(0, cap // GATHER_UNROLL, body, 0)


def _moe_gather(hp, idx):
    n_batch, n_tok, width = hp.shape
    _, n_e, cap = idx.shape
    assert cap % GATHER_UNROLL == 0
    return pl.pallas_call(
        functools.partial(_gather_kernel, cap=cap, n_e=n_e),
        grid_spec=pltpu.PrefetchScalarGridSpec(
            num_scalar_prefetch=1,
            grid=(n_batch, n_e),
            in_specs=[pl.BlockSpec((None, n_tok, width), lambda b, e, idx_ref: (b, 0, 0))],
            out_specs=pl.BlockSpec((None, cap, width), lambda b, e, idx_ref: (e, b, 0)),
        ),
        out_shape=jax.ShapeDtypeStruct((n_e, n_batch * cap, width), jnp.uint32),
        compiler_params=_cparams("parallel", "arbitrary"),
        name="moe_gather",
    )(idx.reshape(-1), hp)


def _combine_kernel(idx_ref, y_ref, o_ref, *, cap, n_e):
    e = pl.program_id(1)

    @pl.when(e == 0)
    def _():
        o_ref[...] = jnp.zeros_like(o_ref)

    base = (pl.program_id(0) * n_e + e) * cap

    def body(j, carry):
        for u in range(GATHER_UNROLL):
            i = j * GATHER_UNROLL + u
            t = idx_ref[base + i]
            o_ref[pl.ds(t, 1), :] = o_ref[pl.ds(t, 1), :] + y_ref[pl.ds(i, 1), :]
        return carry

    lax.fori_loop(0, cap // GATHER_UNROLL, body, 0)


def _moe_combine(y, idx, n_tok):
    n_batch, n_e, cap = idx.shape
    d = y.shape[2]
    return pl.pallas_call(
        functools.partial(_combine_kernel, cap=cap, n_e=n_e),
        grid_spec=pltpu.PrefetchScalarGridSpec(
            num_scalar_prefetch=1,
            grid=(n_batch, n_e),
            in_specs=[pl.BlockSpec((None, cap, d), lambda b, e, idx_ref: (e, b, 0))],
            out_specs=pl.BlockSpec((None, n_tok, d), lambda b, e, idx_ref: (b, 0, 0)),
        ),
        out_shape=jax.ShapeDtypeStruct((n_batch, n_tok, d), F32),
        compiler_params=_cparams("parallel", "arbitrary"),
        name="moe_combine",
    )(idx.reshape(-1), y)


def _moe_kernel(x_ref, wg_ref, wu_ref, wd_ref, gate_ref, o_ref, xb_ref):
    f = pl.program_id(1)

    @pl.when(f == 0)
    def _():
        packed = x_ref[...]
        xb_ref[...] = jnp.concatenate([pltpu.bitcast(packed << 16, F32),
                                       pltpu.bitcast(packed & jnp.uint32(0xFFFF0000), F32)], axis=1).astype(BF16)
        o_ref[...] = jnp.zeros_like(o_ref)

    x = xb_ref[...]
    a = _dot(x, wg_ref[...].astype(BF16))
    u = _dot(x, wu_ref[...].astype(BF16))
    hid = (_silu(a) * u).astype(BF16)
    o_ref[...] += _dot(hid, wd_ref[...].astype(BF16))

    @pl.when(f == pl.num_programs(1) - 1)
    def _():
        o_ref[...] = o_ref[...] * gate_ref[...]


def _moe_ffn(xg, w_gate, w_up, w_down, gate, layer):
    n_e, m, _ = xg.shape
    return pl.pallas_call(
        _moe_kernel,
        grid=(n_e, D_EXPERT // MOE_TF),
        in_specs=[
            pl.BlockSpec((None, m, D_MODEL // 2), lambda e, f: (e, 0, 0)),
            pl.BlockSpec((None, None, D_MODEL, MOE_TF), lambda e, f: (layer, e, 0, f)),
            pl.BlockSpec((None, None, D_MODEL, MOE_TF), lambda e, f: (layer, e, 0, f)),
            pl.BlockSpec((None, None, MOE_TF, D_MODEL), lambda e, f: (layer, e, f, 0)),
            pl.BlockSpec((None, m, 1), lambda e, f: (e, 0, 0)),
        ],
        out_specs=pl.BlockSpec((None, m, D_MODEL), lambda e, f: (e, 0, 0)),
        out_shape=jax.ShapeDtypeStruct((n_e, m, D_MODEL), F32),
        scratch_shapes=[pltpu.VMEM((m, D_MODEL), BF16)],
        compiler_params=_cparams("parallel", "arbitrary"),
        name="moe_ffn",
    )(xg, w_gate, w_up, w_down, gate)


def _final_kernel(x_ref, moe_ref, m_ref, g_ref, o_ref):
    x = x_ref[...] + m_ref[5] * moe_ref[...]
    o_ref[...] = x * lax.rsqrt(jnp.mean(x * x, axis=-1, keepdims=True) + RMS_EPS) * g_ref[...]


def _final_norm(x, moe, mods, g, n_batch, n_lat):
    row = lambda b, i: (b, i, 0)
    blk = pl.BlockSpec((None, ROW_TILE, D_MODEL), row)
    return pl.pallas_call(
        _final_kernel,
        grid=(n_batch, n_lat // ROW_TILE),
        in_specs=[blk, blk, pl.BlockSpec((None, 6, 1, D_MODEL), lambda b, i: (b, 0, 0, 0)),
                  pl.BlockSpec((1, D_MODEL), lambda b, i: (0, 0))],
        out_specs=blk,
        out_shape=jax.ShapeDtypeStruct((n_batch, n_lat, D_MODEL), F32),
        compiler_params=_cparams("parallel", "parallel"),
        name="final_norm",
    )(x, moe, mods, g.reshape(1, D_MODEL))


def _reorder_in_columns(w):
    gw = GROUP_W
    o = 6 * gw + 4 * GROUP_HEADS
    pad = jnp.zeros((w.shape[0], 128 - 4 * GROUP_HEADS), w.dtype)
    parts = [w[:, 0:3 * gw], w[:, o + gw:o + 4 * gw], w[:, 3 * gw:6 * gw], w[:, o:o + gw],
             w[:, o + 4 * gw:o + 5 * gw], w[:, 6 * gw:o], pad]
    return jnp.concatenate(parts, axis=1).astype(BF16)


@functools.lru_cache(maxsize=None)
def _rope_tables(n_lat, n_ctx):
    t = np.arange(n_lat)
    pos = np.stack([t // GRID_W, t % GRID_W], axis=-1).astype(np.float32)
    n_freq = DIFF_DIM // 4
    inv = (ROPE_BASE ** (-np.arange(n_freq, dtype=np.float32) / n_freq)).astype(np.float32)
    ang = (pos[:, :, None] * inv).reshape(n_lat, 2 * n_freq)
    lane = np.arange(GROUP_W)
    idx = (lane % DIFF_DIM) // 2
    sign = np.where(lane % 2 == 0, -1.0, 1.0)
    cos = np.concatenate([np.cos(ang)[:, idx], np.ones((n_ctx, GROUP_W))], axis=0)
    sin = np.concatenate([np.sin(ang)[:, idx] * sign, np.zeros((n_ctx, GROUP_W))], axis=0)
    return cos.astype(np.float32), sin.astype(np.float32)


def _lane_vec(v, n=128):
    v = v.reshape(-1).astype(F32)
    return jnp.zeros((1, n), F32).at[0, :v.shape[0]].set(v)


def _route(aff, cap):
    gate, idx = lax.top_k(jnp.swapaxes(aff, 1, 2), cap)
    return gate, idx


def kernel(x, c, ctx, c_ctx, w_mod, b_mod, norm1_g, w_in, na_rpb, dn_conv_w, dn_a_log, dn_dt_bias, dn_norm_g, df_lambda, df_norm_g, ft_w, w_out, norm2_g, w_router, w_gate, w_up, w_down, final_norm_g):
    n_batch, n_lat, _ = x.shape
    n_ctx = ctx.shape[1]
    n_tok = n_lat + n_ctx
    depth = w_mod.shape[0]
    assert n_batch + 1 <= 8 and n_ctx == ROW_TILE and n_lat % ROW_TILE == 0

    xs = jnp.concatenate([x, ctx], axis=1)
    cc = jnp.zeros((8, D_MODEL), F32).at[:n_batch].set(c).at[n_batch].set(c_ctx)
    m_all = _modulation(cc, w_mod, b_mod)

    cos_t, sin_t = (jnp.asarray(a) for a in _rope_tables(n_lat, n_ctx))
    n1 = n_lat // FT_N2
    f1, g_lat, cc_m, sc_m = _ft_tables(n1, FT_N2)
    _, g_ctx, _, _ = _ft_tables(1, n_ctx)
    f1h, f1l = _np_split(f1)
    glh, gll = _np_split(g_lat)
    gch, gcl = _np_split(g_ctx)
    chan = _np_split(cc_m) + _np_split(sc_m)
    cap_lat = EC_FACTOR * n_lat // N_EXPERTS
    cap_ctx = EC_FACTOR * n_ctx // N_EXPERTS

    prev = None
    for l in range(depth):
        lam_init = 0.8 - 0.6 * math.exp(-0.3 * l)
        mods = m_all[l, :n_batch + 1].reshape(n_batch + 1, 6, 1, D_MODEL)
        xs, pbf, pf, pu = _in_proj(xs, prev, norm1_g[l], mods, _reorder_in_columns(w_in[l]), cos_t, sin_t, n_batch, n_tok)

        o_na = _na_attention(pbf, _na_bias_table(na_rpb[l]), n_batch, n_lat, n_ctx)

        qkv = _dn_prep(pf, dn_conv_w[l], n_batch, n_lat, n_ctx)
        o_f, o_b = _dn_scan(qkv, pf, _lane_vec(dn_a_log[l]), _lane_vec(dn_dt_bias[l]), n_batch, n_lat, n_ctx)

        o_df = _df_attention(pbf, df_lambda[l], jnp.tile(df_norm_g[l], GROUP_HEADS).reshape(1, GROUP_W),
                             lam_init, n_batch, n_lat, n_ctx)

        ftw = ft_w[l].astype(BF16)
        y1 = _ft_stage1(pu.reshape(n_batch, n_tok // FT_N2, FT_N2 * GROUP_W), f1h, f1l, n_batch, n1)
        o_lat = _ft_stage2(y1.reshape(n_batch, 2, n1, FT_N2, GROUP_W), glh, gll, chan, ftw, n_batch, n1, FT_N2, 8)
        o_lat = jnp.swapaxes(o_lat, 1, 2).reshape(n_batch, n_lat, GROUP_W)
        u_ctx = pu[:, n_lat:]
        y1c = jnp.stack([u_ctx, jnp.zeros_like(u_ctx)], axis=1)[:, :, None]
        o_ctx = _ft_stage2(y1c, gch, gcl, chan, ftw, n_batch, 1, n_ctx, 1).reshape(n_batch, n_ctx, GROUP_W)
        o_ft = jnp.concatenate([o_lat, o_ctx], axis=1)

        wr = jnp.pad(w_router[l], ((0, 0), (0, 128 - N_EXPERTS)))
        xs, h2, aff = _out_proj(xs, mods, o_na, o_f, o_b, pf, jnp.tile(dn_norm_g[l], GROUP_HEADS).reshape(1, GROUP_W),
                                o_df, o_ft, w_out[l].astype(BF16), norm2_g[l], wr, n_batch, n_tok)

        gate_l, idx_l = _route(aff[:, :n_lat, :N_EXPERTS], cap_lat)
        gate_c, idx_c = _route(aff[:, n_lat:, :N_EXPERTS], cap_ctx)
        idx = jnp.concatenate([idx_l, idx_c + n_lat], axis=2)
        gate = jnp.concatenate([gate_l, gate_c], axis=2)
        cap = cap_lat + cap_ctx
        gate_col = jnp.swapaxes(gate, 0, 1).reshape(N_EXPERTS, n_batch * cap, 1)
        y = _moe_ffn(_moe_gather(h2, idx), w_gate, w_up, w_down, gate_col, l)
        prev = (_moe_combine(y, idx, n_tok), mods)

    return _final_norm(xs, prev[0], prev[1], final_norm_g, n_batch, n_lat)
```

```python
import functools
import math

import numpy as np
import jax
import jax.numpy as jnp
from jax import lax
from jax.experimental import pallas as pl
from jax.experimental.pallas import tpu as pltpu

F32 = jnp.float32
BF16 = jnp.bfloat16

D_MODEL = 1024
DEPTH = 4
GRID_W = 64
GROUP_W = 256
GROUP_HEADS = 4
HEAD_DIM = 64
NA_WIN_H = 8
NA_WIN_W = 16
DN_CONV_K = 5
DN_CHUNK = 64
DIFF_DIM = 32
FT_DIM = 64
N_EXPERTS = 16
EC_FACTOR = 2
D_EXPERT = 2 * D_MODEL
ROPE_BASE = 10000.0
RMS_EPS = 1e-6
NEG = -1e30

ROW_TILE = 256
N_BF = 6 * GROUP_W
N_F32 = 4 * GROUP_W + 128
VMEM_LIMIT = 56 * 1024 * 1024


def _cparams(*sem):
    return pltpu.CompilerParams(dimension_semantics=sem, vmem_limit_bytes=VMEM_LIMIT)


def _split(x):
    hi = x.astype(BF16)
    lo = (x - hi.astype(F32)).astype(BF16)
    return hi, lo


def _dot(a, b):
    return jnp.dot(a, b, preferred_element_type=F32)


def _dot_nt(a, b):
    return lax.dot_general(a, b, (((1,), (1,)), ((), ())), preferred_element_type=F32)


def _dot_tn(a, b):
    return lax.dot_general(a, b, (((0,), (0,)), ((), ())), preferred_element_type=F32)


def _dot3(a, b):
    ah, al = _split(a)
    bh, bl = _split(b)
    return _dot(ah, bh) + (_dot(al, bh) + _dot(ah, bl))


def _dot3_const(ah, al, b):
    bh, bl = _split(b)
    return _dot(ah, bh) + (_dot(al, bh) + _dot(ah, bl))


def _silu(x):
    return x * jax.nn.sigmoid(x)


def _mod_kernel(s_ref, w_ref, b_ref, o_ref):
    s = _silu(s_ref[...])
    o_ref[...] = _dot(s.astype(BF16), w_ref[...].astype(BF16)) + b_ref[...]


def _modulation(cc, w_mod, b_mod):
    depth = w_mod.shape[0]
    nt = 6 * D_MODEL // 1024
    return pl.pallas_call(
        _mod_kernel,
        grid=(depth, nt),
        in_specs=[
            pl.BlockSpec((8, D_MODEL), lambda l, j: (0, 0)),
            pl.BlockSpec((None, D_MODEL, 1024), lambda l, j: (l, 0, j)),
            pl.BlockSpec((None, 1, 1024), lambda l, j: (l, 0, j)),
        ],
        out_specs=pl.BlockSpec((None, 8, 1024), lambda l, j: (l, 0, j)),
        out_shape=jax.ShapeDtypeStruct((depth, 8, 6 * D_MODEL), F32),
        compiler_params=_cparams("parallel", "parallel"),
        name="modulation",
    )(cc, w_mod, b_mod.reshape(depth, 1, 6 * D_MODEL))


def _norm_mod(x, g, shift, scale):
    y = x * lax.rsqrt(jnp.mean(x * x, axis=-1, keepdims=True) + RMS_EPS) * g
    return y * (1.0 + scale) + shift


def _in_kernel(*refs, residual):
    if residual:
        x_ref, moe_ref, mprev_ref, g_ref, m_ref, w_ref, cos_ref, sin_ref, xo_ref, obf_ref, of_ref, ou_ref = refs
        x = x_ref[...] + mprev_ref[5] * moe_ref[...]
        xo_ref[...] = x
    else:
        x_ref, g_ref, m_ref, w_ref, cos_ref, sin_ref, obf_ref, of_ref, ou_ref = refs
        x = x_ref[...]
    h = _norm_mod(x, g_ref[...], m_ref[0], m_ref[1])
    p = _dot(h.astype(BF16), w_ref[...])
    gw = GROUP_W
    lane = lax.broadcasted_iota(jnp.int32, (1, gw), 1)
    even = (lane % 2) == 0
    cos, sin = cos_ref[...], sin_ref[...]

    def rope(t):
        sw = jnp.where(even, pltpu.roll(t, gw - 1, 1), pltpu.roll(t, 1, 1))
        return t * cos + sw * sin

    obf_ref[:, 0:gw] = (p[:, 0:gw] * HEAD_DIM ** -0.5).astype(BF16)
    obf_ref[:, gw:3 * gw] = p[:, gw:3 * gw].astype(BF16)
    obf_ref[:, 3 * gw:4 * gw] = (rope(p[:, 3 * gw:4 * gw]) * DIFF_DIM ** -0.5).astype(BF16)
    obf_ref[:, 4 * gw:5 * gw] = rope(p[:, 4 * gw:5 * gw]).astype(BF16)
    obf_ref[:, 5 * gw:6 * gw] = p[:, 5 * gw:6 * gw].astype(BF16)
    of_ref[:, 0:4 * gw] = p[:, 6 * gw:10 * gw]
    of_ref[:, 4 * gw:4 * gw + 128] = p[:, 11 * gw:11 * gw + 128]
    ou_ref[...] = p[:, 10 * gw:11 * gw]


def _in_proj(x, prev, g, mods, w_perm, cos_t, sin_t, n_batch, n_tok):
    nt = n_tok // ROW_TILE
    ctx_tile = nt - 1
    n_out = w_perm.shape[1]
    row = lambda b, i: (b, i, 0)
    x_spec = pl.BlockSpec((None, ROW_TILE, D_MODEL), row)
    mod_spec = pl.BlockSpec((None, 6, 1, D_MODEL), lambda b, i: (jnp.where(i == ctx_tile, n_batch, b), 0, 0, 0))
    in_specs = [
        pl.BlockSpec((1, D_MODEL), lambda b, i: (0, 0)),
        mod_spec,
        pl.BlockSpec((D_MODEL, n_out), lambda b, i: (0, 0)),
        pl.BlockSpec((ROW_TILE, GROUP_W), lambda b, i: (i, 0)),
        pl.BlockSpec((ROW_TILE, GROUP_W), lambda b, i: (i, 0)),
    ]
    out_specs = [
        pl.BlockSpec((None, ROW_TILE, N_BF), row),
        pl.BlockSpec((None, ROW_TILE, N_F32), row),
        pl.BlockSpec((None, ROW_TILE, GROUP_W), row),
    ]
    out_shape = [
        jax.ShapeDtypeStruct((n_batch, n_tok, N_BF), BF16),
        jax.ShapeDtypeStruct((n_batch, n_tok, N_F32), F32),
        jax.ShapeDtypeStruct((n_batch, n_tok, GROUP_W), F32),
    ]
    args = (g.reshape(1, D_MODEL), mods, w_perm, cos_t, sin_t)
    if prev is None:
        in_specs, args = [x_spec] + in_specs, (x,) + args
    else:
        in_specs, args = [x_spec, x_spec, mod_spec] + in_specs, (x,) + tuple(prev) + args
        out_specs = [x_spec] + out_specs
        out_shape = [jax.ShapeDtypeStruct((n_batch, n_tok, D_MODEL), F32)] + out_shape
    outs = pl.pallas_call(
        functools.partial(_in_kernel, residual=prev is not None),
        grid=(n_batch, nt),
        in_specs=in_specs,
        out_specs=out_specs,
        out_shape=out_shape,
        compiler_params=_cparams("parallel", "parallel"),
        name="in_proj",
    )(*args)
    return tuple(outs) if prev is not None else (x,) + tuple(outs)


def _na_kernel(q_ref, k_ref, v_ref, bias_ref, o_ref, *, n_lat, n_ctx):
    i = pl.program_id(1)
    rows = n_lat // GRID_W
    wh = min(NA_WIN_H, rows)
    n_win = wh * GRID_W
    per = ROW_TILE // GRID_W
    kc = k_ref[n_lat:n_lat + n_ctx, :]
    vc = v_ref[n_lat:n_lat + n_ctx, :]
    lane = lax.broadcasted_iota(jnp.int32, (GRID_W, GROUP_W), 1)
    masks = [lane // HEAD_DIM == h for h in range(GROUP_HEADS)]
    for u in range(per):
        r = i * per + u
        is_lat = r < rows
        base = jnp.where(is_lat, jnp.clip(r - wh // 2, 0, rows - wh), 0)
        slab = jnp.where(is_lat, base - r + NA_WIN_H - 1, NA_WIN_H)
        start = pl.multiple_of(base * GRID_W, GRID_W)
        q = q_ref[u * GRID_W:(u + 1) * GRID_W, :]
        qm = jnp.concatenate([jnp.where(m, q, jnp.zeros_like(q)) for m in masks], axis=0)
        keys = jnp.concatenate([k_ref[pl.ds(start, n_win), :], kc], axis=0)
        vals = jnp.concatenate([v_ref[pl.ds(start, n_win), :], vc], axis=0)
        s = _dot_nt(qm, keys)
        sw = s[:, :n_win] + bias_ref[slab].reshape(GROUP_HEADS * GRID_W, n_win)
        sc = s[:, n_win:]
        m = jnp.maximum(jnp.max(sw, axis=-1, keepdims=True), jnp.max(sc, axis=-1, keepdims=True))
        pw = jnp.exp(sw - m)
        pc = jnp.exp(sc - m)
        l = jnp.sum(pw, axis=-1, keepdims=True) + jnp.sum(pc, axis=-1, keepdims=True)
        res = _dot(jnp.concatenate([pw, pc], axis=1).astype(BF16), vals) / l
        o = jnp.where(masks[0], res[0:GRID_W], 0.0)
        for h in range(1, GROUP_HEADS):
            o = o + jnp.where(masks[h], res[h * GRID_W:(h + 1) * GRID_W], 0.0)
        o_ref[u * GRID_W:(u + 1) * GRID_W, :] = o.astype(BF16)


def _na_attention(pbf, bias_tab, n_batch, n_lat, n_ctx):
    n_tok = n_lat + n_ctx
    return pl.pallas_call(
        functools.partial(_na_kernel, n_lat=n_lat, n_ctx=n_ctx),
        grid=(n_batch, n_tok // ROW_TILE),
        in_specs=[
            pl.BlockSpec((None, ROW_TILE, GROUP_W), lambda b, i: (b, i, 0)),
            pl.BlockSpec((None, n_tok, GROUP_W), lambda b, i: (b, 0, 1)),
            pl.BlockSpec((None, n_tok, GROUP_W), lambda b, i: (b, 0, 2)),
            pl.BlockSpec(bias_tab.shape, lambda b, i: (0, 0, 0, 0)),
        ],
        out_specs=pl.BlockSpec((None, ROW_TILE, GROUP_W), lambda b, i: (b, i, 0)),
        out_shape=jax.ShapeDtypeStruct((n_batch, n_tok, GROUP_W), BF16),
        compiler_params=_cparams("parallel", "arbitrary"),
        name="na_attention",
    )(pbf, pbf, pbf, bias_tab)


def _na_bias_table(rpb):
    wh = NA_WIN_H
    cq = np.arange(GRID_W)
    c0 = np.clip(cq - NA_WIN_W // 2, 0, GRID_W - NA_WIN_W)
    in_win = (cq[None, :] >= c0[:, None]) & (cq[None, :] < c0[:, None] + NA_WIN_W)
    dx = np.clip(cq[None, :] - cq[:, None], 1 - NA_WIN_W, NA_WIN_W - 1) + NA_WIN_W - 1
    dy = np.arange(wh)[:, None] + np.arange(wh)[None, :]
    pick_x = (dx[:, :, None] == np.arange(2 * NA_WIN_W - 1)).astype(np.float32)
    pick_y = (dy[:, :, None] == np.arange(2 * NA_WIN_H - 1)).astype(np.float32)
    hi = lax.Precision.HIGHEST
    cols = jnp.einsum('hyx,qkx->hyqk', rpb.astype(F32), pick_x, precision=hi)
    tab = jnp.einsum('swy,hyqk->shqwk', pick_y, cols, precision=hi)
    tab = jnp.where(in_win[None, None, :, None, :], tab, NEG)
    tab = tab.reshape(wh, GROUP_HEADS, GRID_W, wh * GRID_W)
    return jnp.concatenate([tab, jnp.full((1,) + tab.shape[1:], NEG, F32)], axis=0)


DF_KV = 512


def _df_kernel(lam_ref, q_ref, k_ref, v_ref, g_ref, o_ref, s_ref, *, chunks, lam_init):
    lq1, lk1, lq2, lk2 = lam_ref[0:1, :], lam_ref[1:2, :], lam_ref[2:3, :], lam_ref[3:4, :]
    lam = (jnp.exp(jnp.sum(lq1 * lk1, axis=-1, keepdims=True))
           - jnp.exp(jnp.sum(lq2 * lk2, axis=-1, keepdims=True)) + lam_init)
    q = q_ref[...]
    tq, gw = q.shape
    lane = lax.broadcasted_iota(jnp.int32, (tq, gw), 1)
    row2 = lax.broadcasted_iota(jnp.int32, (2 * tq, 1), 0)
    origin = chunks[0][0]
    zero = jnp.zeros_like(q)

    def head(h, out):
        qm = [jnp.where(lane // DIFF_DIM == 2 * h + mp, q, zero) for mp in range(2)]
        m = [jnp.full((tq, 128), NEG, F32) for _ in range(2)]
        for st, sz in chunks:
            k = k_ref[st:st + sz, :]
            for mp in range(2):
                s = _dot_nt(qm[mp], k)
                s_ref[mp * tq:(mp + 1) * tq, st - origin:st - origin + sz] = s
                for t in range(sz // 128):
                    m[mp] = jnp.maximum(m[mp], s[:, t * 128:(t + 1) * 128])
        m = [jnp.max(x, axis=-1, keepdims=True) for x in m]
        ones_lane = ((h + 1) % GROUP_HEADS) * HEAD_DIM
        acc = [jnp.zeros((tq, gw), F32) for _ in range(2)]
        for st, sz in chunks:
            v = v_ref[st:st + sz, :]
            lane_v = lax.broadcasted_iota(jnp.int32, v.shape, 1)
            v = jnp.where(lane_v == ones_lane, jnp.ones_like(v), v)
            for mp in range(2):
                e = jnp.exp((s_ref[mp * tq:(mp + 1) * tq, st - origin:st - origin + sz] - m[mp]).astype(BF16))
                acc[mp] = acc[mp] + _dot(e, v)
        w = []
        for mp in range(2):
            l = jnp.sum(jnp.where(lane == ones_lane, acc[mp], 0.0), axis=-1, keepdims=True)
            w.append(acc[mp] * ((1.0 if mp == 0 else lam) / l))
        return out + jnp.where(lane // HEAD_DIM == h, w[0] - w[1], 0.0)

    o = lax.fori_loop(0, GROUP_HEADS, head, jnp.zeros((tq, gw), F32))
    sq = o * o
    scale = jnp.zeros((tq, gw), F32)
    for h in range(GROUP_HEADS):
        in_head = lane // HEAD_DIM == h
        ms = jnp.sum(jnp.where(in_head, sq, 0.0), axis=-1, keepdims=True) * (1.0 / HEAD_DIM)
        scale = jnp.where(in_head, lax.rsqrt(ms + RMS_EPS), scale)
    o_ref[...] = (o * scale * g_ref[...] * (1.0 - lam_init)).astype(BF16)


def _df_call(pbf, df_lambda, g_tiled, lam_init, n_batch, n_tok, first_tile, n_tiles, chunks, name):
    n_keys = sum(sz for _, sz in chunks)
    return pl.pallas_call(
        functools.partial(_df_kernel, chunks=chunks, lam_init=lam_init),
        grid=(n_batch, n_tiles),
        in_specs=[
            pl.BlockSpec((4, DIFF_DIM), lambda b, i: (0, 0)),
            pl.BlockSpec((None, ROW_TILE, GROUP_W), lambda b, i: (b, first_tile + i, 3)),
            pl.BlockSpec((None, n_tok, GROUP_W), lambda b, i: (b, 0, 4)),
            pl.BlockSpec((None, n_tok, GROUP_W), lambda b, i: (b, 0, 5)),
            pl.BlockSpec((1, GROUP_W), lambda b, i: (0, 0)),
        ],
        out_specs=pl.BlockSpec((None, ROW_TILE, GROUP_W), lambda b, i: (b, i, 0)),
        out_shape=jax.ShapeDtypeStruct((n_batch, n_tiles * ROW_TILE, GROUP_W), BF16),
        scratch_shapes=[pltpu.VMEM((2 * ROW_TILE, n_keys), F32)],
        compiler_params=_cparams("parallel", "arbitrary"),
        name=name,
    )(df_lambda, pbf, pbf, pbf, g_tiled)


def _df_attention(pbf, df_lambda, g_tiled, lam_init, n_batch, n_lat, n_ctx):
    n_tok = n_lat + n_ctx
    lat_chunks = tuple((st, min(DF_KV, n_tok - st)) for st in range(0, n_tok, DF_KV))
    o_lat = _df_call(pbf, df_lambda, g_tiled, lam_init, n_batch, n_tok, 0, n_lat // ROW_TILE, lat_chunks, "df_attention")
    o_ctx = _df_call(pbf, df_lambda, g_tiled, lam_init, n_batch, n_tok, n_lat // ROW_TILE, n_ctx // ROW_TILE,
                     ((n_lat, n_ctx),), "df_attention_ctx")
    return jnp.concatenate([o_lat, o_ctx], axis=1)


CONV_PAD = 8


def _dn_prep_kernel(u_ref, w_ref, o_ref, pad_ref, *, n_lat, n_ctx):
    j = pl.program_id(1)
    zeros = jnp.zeros((CONV_PAD, GROUP_W), F32)
    lat0 = CONV_PAD
    ctx0 = 2 * CONV_PAD + n_lat
    pad_ref[0:CONV_PAD, :] = zeros
    pad_ref[lat0 + n_lat:ctx0, :] = zeros
    pad_ref[ctx0 + n_ctx:ctx0 + n_ctx + CONV_PAD, :] = zeros
    pad_ref[lat0:lat0 + n_lat, :] = u_ref[0:n_lat, :]
    pad_ref[ctx0:ctx0 + n_ctx, :] = u_ref[n_lat:n_lat + n_ctx, :]
    qscale = jnp.where(j == 0, HEAD_DIM ** -0.5, 1.0)
    half = DN_CONV_K // 2
    for (src0, dst0, n) in ((lat0, 0, n_lat), (ctx0, n_lat, n_ctx)):
        for t0 in range(0, n, ROW_TILE):
            acc = jnp.zeros((ROW_TILE, GROUP_W), F32)
            for tap in range(DN_CONV_K):
                a = src0 + t0 + tap - half
                acc = acc + pad_ref[a:a + ROW_TILE, :] * w_ref[tap:tap + 1, :]
            y = _silu(acc)
            parts = []
            for h in range(GROUP_HEADS):
                yh = y[:, h * HEAD_DIM:(h + 1) * HEAD_DIM]
                nrm = lax.rsqrt(jnp.sum(yh * yh, axis=-1, keepdims=True) + 1e-6) * qscale
                parts.append(yh * jnp.where(j == 2, 1.0, nrm))
            o_ref[dst0 + t0:dst0 + t0 + ROW_TILE, :] = jnp.concatenate(parts, axis=-1)


def _dn_prep(pf, conv_w, n_batch, n_lat, n_ctx):
    n_tok = n_lat + n_ctx
    return pl.pallas_call(
        functools.partial(_dn_prep_kernel, n_lat=n_lat, n_ctx=n_ctx),
        grid=(n_batch, 3),
        in_specs=[
            pl.BlockSpec((None, n_tok, GROUP_W), lambda b, j: (b, 0, j)),
            pl.BlockSpec((DN_CONV_K, GROUP_W), lambda b, j: (0, j)),
        ],
        out_specs=pl.BlockSpec((None, n_tok, GROUP_W), lambda b, j: (b, 0, j)),
        out_shape=jax.ShapeDtypeStruct((n_batch, n_tok, 3 * GROUP_W), F32),
        scratch_shapes=[pltpu.VMEM((n_tok + 3 * CONV_PAD, GROUP_W), F32)],
        compiler_params=_cparams("parallel", "arbitrary"),
        name="dn_prep",
    )(pf, conv_w)


DN_BLOCK = 4 * DN_CHUNK


def _split3(x):
    x1 = x.astype(BF16)
    r = x - x1.astype(F32)
    x2 = r.astype(BF16)
    return x1, x2, (r - x2.astype(F32)).astype(BF16)


def _block_diag(x, masks):
    zero = jnp.zeros_like(x)
    return jnp.concatenate([jnp.where(m, x, zero) for m in masks], axis=0)


def _mm3_heads(lhs, rhs, masks):
    lh, ll = _split(lhs)
    rh, rl = _split(rhs)
    bh, bl = _block_diag(rh, masks), _block_diag(rl, masks)
    return _dot(jnp.concatenate([lh, ll, lh], axis=1), jnp.concatenate([bh, bh, bl], axis=0))


def _dn_kernel(qf_ref, kf_ref, vf_ref, abf_ref, qb_ref, kb_ref, vb_ref, abb_ref, av_ref, dt_ref, exp_ref,
               of_ref, ob_ref, s_ref):
    @pl.when(pl.program_id(1) == 0)
    def _():
        s_ref[...] = jnp.zeros_like(s_ref)

    c, nb, gw = DN_CHUNK, DN_BLOCK, GROUP_W
    n_sub = nb // c
    lane4 = lax.broadcasted_iota(jnp.int32, (c, gw), 1)
    masks = [(lane4 // HEAD_DIM) == h for h in range(GROUP_HEADS)]
    ri = lax.broadcasted_iota(jnp.int32, (c, gw), 0)
    cj = lane4 % HEAD_DIM
    eye = (ri == cj).astype(F32)
    bi = lax.broadcasted_iota(jnp.int32, (nb, nb), 0)
    bj = lax.broadcasted_iota(jnp.int32, (nb, nb), 1)
    same_chunk = (bi // c) == (bj // c)
    lane1 = lax.broadcasted_iota(jnp.int32, (nb, 128), 1)
    dirs = ((qf_ref, kf_ref, vf_ref, abf_ref), (qb_ref, kb_ref, vb_ref, abb_ref))

    chains = []
    for d, (q_ref, k_ref, v_ref, ab_ref) in enumerate(dirs):
        incl = (ri >= cj) if d == 0 else (ri <= cj)
        strict = (ri > cj) if d == 0 else (ri < cj)
        ab = ab_ref[...]
        x = ab + dt_ref[...]
        softplus = jnp.maximum(x, 0.0) + jnp.log1p(jnp.exp(-jnp.abs(x)))
        g = -jnp.exp(av_ref[...]) * softplus
        tri = (same_chunk & ((bi >= bj) if d == 0 else (bi <= bj))).astype(BF16)
        g1, g2, g3 = _split3(g)
        gcs = _dot(tri, g1) + (_dot(tri, g2) + _dot(tri, g3))
        x1, x2, x3 = _split3(jnp.where(lane1 < 2 * GROUP_HEADS, gcs, jax.nn.sigmoid(ab)))
        e = exp_ref[d]
        spread = _dot(x1, e) + (_dot(x2, e) + _dot(x3, e))
        q_all, k_all, v_all = q_ref[...], k_ref[...], v_ref[...]
        last = c - 1 if d == 0 else 0
        for j in range(n_sub):
            rows = slice(j * c, (j + 1) * c)
            gc, beta = spread[rows, :gw], spread[rows, gw:]
            q, k, v = q_all[rows], k_all[rows], v_all[rows]
            gc_row = jnp.sum(gc * eye, axis=0, keepdims=True)
            decay = jnp.exp(jnp.where(incl, gc - gc_row, NEG))
            kbeta = k * beta
            kq = _dot_nt(jnp.concatenate([kbeta, q], axis=0).astype(BF16), _block_diag(k.astype(BF16), masks))
            eg = jnp.exp(gc)
            g_last = gc[last:last + 1, :]
            chains.append(dict(
                lm=jnp.where(strict, kq[:c] * decay, 0.0),
                qk=jnp.where(incl, kq[c:] * decay, 0.0).astype(BF16),
                rhs=jnp.concatenate([_block_diag((v * beta).astype(BF16), masks),
                                     _block_diag((kbeta * eg).astype(BF16), masks)], axis=1),
                q_dec=q * eg,
                k_dec=(k * jnp.exp(g_last - gc)).astype(BF16),
                g_end=jnp.exp(g_last)))

    def neumann_level(first):
        for ch in chains:
            if first:
                ch["p"] = eye - ch["lm"]
                ch["sq"] = _mm3_heads(ch["lm"], ch["lm"], masks)
            else:
                r = _mm3_heads(jnp.concatenate([ch["sq"], ch["p"]], axis=0), ch["sq"], masks)
                ch["sq"], ch["p"] = r[:c], ch["p"] + r[c:]

    states = [s_ref[0], s_ref[1]]
    outs = [[None] * n_sub, [None] * n_sub]

    def walk(step):
        for d in range(2):
            j = step if d == 0 else n_sub - 1 - step
            ch = prev[d * n_sub + j]
            s = states[d]
            ws = _dot(ch["wq"], _block_diag(s.astype(BF16), masks))
            vb = (ch["u"] - ws[:c]).astype(BF16)
            outs[d][j] = ws[c:] + _dot(ch["qk"], _block_diag(vb, masks))
            full = _dot_tn(ch["k_dec"], vb)
            upd = jnp.where(masks[0], full[0:c], 0.0)
            for h in range(1, GROUP_HEADS):
                upd = upd + jnp.where(masks[h], full[h * c:(h + 1) * c], 0.0)
            states[d] = s * ch["g_end"] + upd

    n_levels = int(math.log2(c)) - 1
    for level in range(n_levels):
        neumann_level(level == 0)
    t_inv = [ch["p"] + _mm3_heads(ch["p"], ch["sq"], masks) for ch in chains]
    uw = [_dot(t.astype(BF16), ch["rhs"]) for t, ch in zip(t_inv, chains)]
    prev = [dict(u=x[:, :gw], wq=jnp.concatenate([x[:, gw:], ch["q_dec"]], axis=0).astype(BF16),
                 qk=ch["qk"], k_dec=ch["k_dec"], g_end=ch["g_end"]) for x, ch in zip(uw, chains)]
    for step in range(n_sub):
        walk(step)
    of_ref[...] = jnp.concatenate(outs[0], axis=0)
    ob_ref[...] = jnp.concatenate(outs[1], axis=0)
    s_ref[...] = jnp.stack(states, axis=0)


def _dn_scan(qkv, pf, a_vec, dt_vec, n_batch, n_lat, n_ctx):
    n_tok = n_lat + n_ctx
    assert n_ctx == DN_BLOCK and n_lat % DN_BLOCK == 0
    nblk = n_tok // DN_BLOCK
    ab_blk = 4 * GROUP_W // 128

    def fwd(n):
        return jnp.where(n == 0, nblk - 1, n - 1)

    def bwd(n):
        return jnp.where(n == 0, nblk - 1, nblk - 1 - n)

    def spec(order, blk, width):
        return pl.BlockSpec((None, DN_BLOCK, width), lambda b, n: (b, order(n), blk))

    spread = np.zeros((2, 128, 2 * GROUP_W), np.float32)
    for d in range(2):
        for h in range(GROUP_HEADS):
            spread[d, GROUP_HEADS * d + h, h * HEAD_DIM:(h + 1) * HEAD_DIM] = 1.0
            spread[d, 2 * GROUP_HEADS + GROUP_HEADS * d + h, GROUP_W + h * HEAD_DIM:GROUP_W + (h + 1) * HEAD_DIM] = 1.0

    lane = pl.BlockSpec((1, 128), lambda b, n: (0, 0))
    out = jax.ShapeDtypeStruct((n_batch, n_tok, GROUP_W), F32)
    return pl.pallas_call(
        _dn_kernel,
        grid=(n_batch, nblk),
        in_specs=[spec(fwd, 0, GROUP_W), spec(fwd, 1, GROUP_W), spec(fwd, 2, GROUP_W), spec(fwd, ab_blk, 128),
                  spec(bwd, 0, GROUP_W), spec(bwd, 1, GROUP_W), spec(bwd, 2, GROUP_W), spec(bwd, ab_blk, 128),
                  lane, lane, pl.BlockSpec((2, 128, 2 * GROUP_W), lambda b, n: (0, 0, 0))],
        out_specs=[spec(fwd, 0, GROUP_W), spec(bwd, 0, GROUP_W)],
        out_shape=[out, out],
        scratch_shapes=[pltpu.VMEM((2, HEAD_DIM, GROUP_W), F32)],
        compiler_params=_cparams("parallel", "arbitrary"),
        name="dn_scan",
    )(qkv, qkv, qkv, pf, qkv, qkv, qkv, pf, a_vec, dt_vec, jnp.asarray(spread, BF16))


FT_N2 = 64


def _ft1_kernel(x_ref, fh_ref, fl_ref, o_ref):
    o_ref[...] = _dot3_const(fh_ref[...], fl_ref[...], x_ref[...])


def _ft_stage1(u_rows, f_hi, f_lo, n_batch, n1, tn=2048):
    width = u_rows.shape[2]
    return pl.pallas_call(
        _ft1_kernel,
        grid=(n_batch, width // tn),
        in_specs=[
            pl.BlockSpec((None, n1, tn), lambda b, j: (b, 0, j)),
            pl.BlockSpec((2 * n1, n1), lambda b, j: (0, 0)),
            pl.BlockSpec((2 * n1, n1), lambda b, j: (0, 0)),
        ],
        out_specs=pl.BlockSpec((None, 2 * n1, tn), lambda b, j: (b, 0, j)),
        out_shape=jax.ShapeDtypeStruct((n_batch, 2 * n1, width), F32),
        compiler_params=_cparams("parallel", "parallel"),
        name="ft_stage1",
    )(u_rows, f_hi, f_lo)


def _ft3_kernel(y_ref, gh_ref, gl_ref, ch_ref, cl_ref, sh_ref, sl_ref, w_ref, o_ref, *, pb, n2, norm):
    vr, vi = [], []
    for p in range(pb):
        rhs = jnp.concatenate([y_ref[0, p], y_ref[1, p]], axis=0)
        v = _dot3_const(gh_ref[p], gl_ref[p], rhs)
        vr.append(v[:n2])
        vi.append(v[n2:])
    vr = jnp.concatenate(vr, axis=0)
    vi = jnp.concatenate(vi, axis=0)
    vrh, vrl = _split(vr)
    vih, vil = _split(vi)
    y = (_dot(vrh, ch_ref[...]) + (_dot(vrl, ch_ref[...]) + _dot(vrh, cl_ref[...]))
         + _dot(vih, sh_ref[...]) + (_dot(vil, sh_ref[...]) + _dot(vih, sl_ref[...]))) * norm
    o = _dot(y.astype(BF16), w_ref[...])
    o_ref[...] = o.reshape(pb, n2, GROUP_W).astype(BF16)


def _ft_stage2(y1, g_hi, g_lo, chan, ft_w, n_batch, n1, n2, pb):
    ch, cl, sh, sl = chan
    full = lambda b, i: (0, 0)
    return pl.pallas_call(
        functools.partial(_ft3_kernel, pb=pb, n2=n2, norm=1.0 / math.sqrt(n1 * n2 * FT_DIM)),
        grid=(n_batch, n1 // pb),
        in_specs=[
            pl.BlockSpec((None, 2, pb, n2, GROUP_W), lambda b, i: (b, 0, i, 0, 0)),
            pl.BlockSpec((pb, 2 * n2, 2 * n2), lambda b, i: (i, 0, 0)),
            pl.BlockSpec((pb, 2 * n2, 2 * n2), lambda b, i: (i, 0, 0)),
            pl.BlockSpec((GROUP_W, GROUP_W), full), pl.BlockSpec((GROUP_W, GROUP_W), full),
            pl.BlockSpec((GROUP_W, GROUP_W), full), pl.BlockSpec((GROUP_W, GROUP_W), full),
            pl.BlockSpec((GROUP_W, GROUP_W), full),
        ],
        out_specs=pl.BlockSpec((None, pb, n2, GROUP_W), lambda b, i: (b, i, 0, 0)),
        out_shape=jax.ShapeDtypeStruct((n_batch, n1, n2, GROUP_W), BF16),
        compiler_params=_cparams("parallel", "parallel"),
        name="ft_stage2",
    )(y1, g_hi, g_lo, ch, cl, sh, sl, ft_w)


def _np_split(a):
    hi = jnp.asarray(a, F32).astype(BF16)
    lo = (jnp.asarray(a, F32) - hi.astype(F32)).astype(BF16)
    return hi, lo


@functools.lru_cache(maxsize=None)
def _ft_tables(n1, n2):
    n = n1 * n2
    p = np.arange(n1)
    ang1 = 2.0 * np.pi * ((p[:, None] * p[None, :]) % n1) / n1
    f1 = np.concatenate([np.cos(ang1), -np.sin(ang1)], axis=0)
    q = np.arange(n2)
    b = np.arange(n2)
    phase = (b[None, None, :] * (p[:, None, None] + n1 * q[None, :, None])) % n
    psi = 2.0 * np.pi * phase / n
    gc, gs = np.cos(psi), np.sin(psi)
    g = np.concatenate([np.concatenate([gc, gs], axis=2), np.concatenate([-gs, gc], axis=2)], axis=1)
    c = np.arange(FT_DIM)
    angc = 2.0 * np.pi * ((c[:, None] * c[None, :]) % FT_DIM) / FT_DIM
    eye = np.eye(GROUP_W // FT_DIM)
    cc, sc = np.kron(eye, np.cos(angc)), np.kron(eye, np.sin(angc))
    return f1.astype(np.float32), g.astype(np.float32), cc.astype(np.float32), sc.astype(np.float32)


def _out_kernel(x_ref, m_ref, na_ref, of_ref, ob_ref, gate_ref, dng_ref, df_ref, ft_ref, w_ref, g2_ref, wr_ref,
                xo_ref, h_ref, aff_ref):
    o = of_ref[...] + ob_ref[...]
    gate = gate_ref[...]
    parts = []
    for h in range(GROUP_HEADS):
        sl = slice(h * HEAD_DIM, (h + 1) * HEAD_DIM)
        oh = o[:, sl]
        parts.append(oh * lax.rsqrt(jnp.mean(oh * oh, axis=-1, keepdims=True) + RMS_EPS))
    y_dn = jnp.concatenate(parts, axis=-1) * dng_ref[...] * _silu(gate)
    mix = jnp.concatenate([na_ref[...], y_dn.astype(BF16), df_ref[...], ft_ref[...]], axis=-1)
    x = x_ref[...] + m_ref[2] * _dot(mix, w_ref[...])
    xo_ref[...] = x
    h2 = _norm_mod(x, g2_ref[...], m_ref[3], m_ref[4])
    bits = pltpu.bitcast(h2.astype(BF16).astype(F32), jnp.uint32)
    half = D_MODEL // 2
    h_ref[...] = bits[:, half:] | (bits[:, :half] >> 16)
    logits = _dot3(h2, wr_ref[...])
    lane = lax.broadcasted_iota(jnp.int32, logits.shape, 1)
    logits = jnp.where(lane < N_EXPERTS, logits, NEG)
    e = jnp.exp(logits - jnp.max(logits, axis=-1, keepdims=True))
    aff_ref[...] = e / jnp.sum(e, axis=-1, keepdims=True)


def _out_proj(x, mods, o_na, o_f, o_b, pf, dn_g, o_df, o_ft, w_out, g2, w_router, n_batch, n_tok):
    nt = n_tok // ROW_TILE
    ctx_tile = nt - 1
    row = lambda b, i: (b, i, 0)
    blk = lambda: pl.BlockSpec((None, ROW_TILE, GROUP_W), row)
    vec = lambda n: pl.BlockSpec((1, n), lambda b, i: (0, 0))
    return pl.pallas_call(
        _out_kernel,
        grid=(n_batch, nt),
        in_specs=[
            pl.BlockSpec((None, ROW_TILE, D_MODEL), row),
            pl.BlockSpec((None, 6, 1, D_MODEL), lambda b, i: (jnp.where(i == ctx_tile, n_batch, b), 0, 0, 0)),
            blk(), blk(), blk(),
            pl.BlockSpec((None, ROW_TILE, GROUP_W), lambda b, i: (b, i, 3)),
            vec(GROUP_W), blk(), blk(),
            pl.BlockSpec((D_MODEL, D_MODEL), lambda b, i: (0, 0)),
            vec(D_MODEL),
            pl.BlockSpec((D_MODEL, 128), lambda b, i: (0, 0)),
        ],
        out_specs=[
            pl.BlockSpec((None, ROW_TILE, D_MODEL), row),
            pl.BlockSpec((None, ROW_TILE, D_MODEL // 2), row),
            pl.BlockSpec((None, ROW_TILE, 128), row),
        ],
        out_shape=[
            jax.ShapeDtypeStruct((n_batch, n_tok, D_MODEL), F32),
            jax.ShapeDtypeStruct((n_batch, n_tok, D_MODEL // 2), jnp.uint32),
            jax.ShapeDtypeStruct((n_batch, n_tok, 128), F32),
        ],
        compiler_params=_cparams("parallel", "parallel"),
        name="out_proj_router",
    )(x, mods, o_na, o_f, o_b, pf, dn_g, o_df, o_ft, w_out, g2.reshape(1, D_MODEL), w_router)


MOE_TF = 256
GATHER_UNROLL = 8


def _gather_kernel(idx_ref, h_ref, o_ref, *, cap, n_e):
    base = (pl.program_id(0) * n_e + pl.program_id(1)) * cap

    def body(j, carry):
        for u in range(GATHER_UNROLL):
            o_ref[j, u:u + 1, :] = h_ref[pl.ds(idx_ref[base + j * GATHER_UNROLL + u], 1), :]
        return carry

    lax.fori_loop(0, cap // GATHER_UNROLL, body, 0)


def _moe_gather(hp, idx):
    n_batch, n_tok, width = hp.shape
    _, n_e, cap = idx.shape
    assert cap % GATHER_UNROLL == 0 and GATHER_UNROLL == 8
    groups = cap // GATHER_UNROLL
    out = pl.pallas_call(
        functools.partial(_gather_kernel, cap=cap, n_e=n_e),
        grid_spec=pltpu.PrefetchScalarGridSpec(
            num_scalar_prefetch=1,
            grid=(n_batch, n_e),
            in_specs=[pl.BlockSpec((None, n_tok, width), lambda b, e, idx_ref: (b, 0, 0))],
            out_specs=pl.BlockSpec((None, groups, GATHER_UNROLL, width), lambda b, e, idx_ref: (e, b, 0, 0)),
        ),
        out_shape=jax.ShapeDtypeStruct((n_e, n_batch * groups, GATHER_UNROLL, width), jnp.uint32),
        compiler_params=_cparams("parallel", "arbitrary"),
        name="moe_gather",
    )(idx.reshape(-1), hp)
    return out.reshape(n_e, n_batch * cap, width)


def _combine_kernel(idx_ref, y_ref, o_ref, *, cap, n_e):
    e = pl.program_id(1)

    @pl.when(e == 0)
    def _():
        o_ref[...] = jnp.zeros_like(o_ref)

    base = (pl.program_id(0) * n_e + e) * cap

    def body(j, carry):
        first = j * GATHER_UNROLL
        toks = [idx_ref[base + first + u] for u in range(GATHER_UNROLL)]
        rows = [o_ref[pl.ds(t, 1), :] + y_ref[j, u:u + 1, :] for u, t in enumerate(toks)]
        for t, row in zip(toks, rows):
            o_ref[pl.ds(t, 1), :] = row
        return carry

    lax.fori_loop(0, cap // GATHER_UNROLL, body, 0)


def _moe_combine(y, idx, n_tok):
    n_batch, n_e, cap = idx.shape
    d = y.shape[2]
    groups = cap // GATHER_UNROLL
    return pl.pallas_call(
        functools.partial(_combine_kernel, cap=cap, n_e=n_e),
        grid_spec=pltpu.PrefetchScalarGridSpec(
            num_scalar_prefetch=1,
            grid=(n_batch, n_e),
            in_specs=[pl.BlockSpec((None, groups, GATHER_UNROLL, d), lambda b, e, idx_ref: (e, b, 0, 0))],
            out_specs=pl.BlockSpec((None, n_tok, d), lambda b, e, idx_ref: (b, 0, 0)),
        ),
        out_shape=jax.ShapeDtypeStruct((n_batch, n_tok, d), F32),
        compiler_params=_cparams("parallel", "arbitrary"),
        name="moe_combine",
    )(idx.reshape(-1), y.reshape(n_e, n_batch * groups, GATHER_UNROLL, d))


def _moe_kernel(x_ref, wg_ref, wu_ref, wd_ref, gate_ref, o_ref, xb_ref):
    f = pl.program_id(1)

    @pl.when(f == 0)
    def _():
        packed = x_ref[...]
        xb_ref[...] = jnp.concatenate([pltpu.bitcast(packed << 16, F32),
                                       pltpu.bitcast(packed & jnp.uint32(0xFFFF0000), F32)], axis=1).astype(BF16)
        o_ref[...] = jnp.zeros_like(o_ref)

    x = xb_ref[...]
    a = _dot(x, wg_ref[...].astype(BF16))
    u = _dot(x, wu_ref[...].astype(BF16))
    hid = (_silu(a) * u).astype(BF16)
    o_ref[...] += _dot(hid, wd_ref[...].astype(BF16))

    @pl.when(f == pl.num_programs(1) - 1)
    def _():
        o_ref[...] = o_ref[...] * gate_ref[...]


def _moe_ffn(xg, w_gate, w_up, w_down, gate, layer):
    n_e, m, _ = xg.shape
    return pl.pallas_call(
        _moe_kernel,
        grid=(n_e, D_EXPERT // MOE_TF),
        in_specs=[
            pl.BlockSpec((None, m, D_MODEL // 2), lambda e, f: (e, 0, 0)),
            pl.BlockSpec((None, None, D_MODEL, MOE_TF), lambda e, f: (layer, e, 0, f)),
            pl.BlockSpec((None, None, D_MODEL, MOE_TF), lambda e, f: (layer, e, 0, f)),
            pl.BlockSpec((None, None, MOE_TF, D_MODEL), lambda e, f: (layer, e, f, 0)),
            pl.BlockSpec((None, m, 1), lambda e, f: (e, 0, 0)),
        ],
        out_specs=pl.BlockSpec((None, m, D_MODEL), lambda e, f: (e, 0, 0)),
        out_shape=jax.ShapeDtypeStruct((n_e, m, D_MODEL), F32),
        scratch_shapes=[pltpu.VMEM((m, D_MODEL), BF16)],
        compiler_params=_cparams("parallel", "arbitrary"),
        name="moe_ffn",
    )(xg, w_gate, w_up, w_down, gate)


def _final_kernel(x_ref, moe_ref, m_ref, g_ref, o_ref):
    x = x_ref[...] + m_ref[5] * moe_ref[...]
    o_ref[...] = x * lax.rsqrt(jnp.mean(x * x, axis=-1, keepdims=True) + RMS_EPS) * g_ref[...]


def _final_norm(x, moe, mods, g, n_batch, n_lat):
    row = lambda b, i: (b, i, 0)
    blk = pl.BlockSpec((None, ROW_TILE, D_MODEL), row)
    return pl.pallas_call(
        _final_kernel,
        grid=(n_batch, n_lat // ROW_TILE),
        in_specs=[blk, blk, pl.BlockSpec((None, 6, 1, D_MODEL), lambda b, i: (b, 0, 0, 0)),
                  pl.BlockSpec((1, D_MODEL), lambda b, i: (0, 0))],
        out_specs=blk,
        out_shape=jax.ShapeDtypeStruct((n_batch, n_lat, D_MODEL), F32),
        compiler_params=_cparams("parallel", "parallel"),
        name="final_norm",
    )(x, moe, mods, g.reshape(1, D_MODEL))


def _reorder_in_columns(w):
    gw = GROUP_W
    o = 6 * gw + 4 * GROUP_HEADS
    pad = jnp.zeros((w.shape[0], 128 - 4 * GROUP_HEADS), w.dtype)
    parts = [w[:, 0:3 * gw], w[:, o + gw:o + 4 * gw], w[:, 3 * gw:6 * gw], w[:, o:o + gw],
             w[:, o + 4 * gw:o + 5 * gw], w[:, 6 * gw:o], pad]
    return jnp.concatenate(parts, axis=1).astype(BF16)


@functools.lru_cache(maxsize=None)
def _rope_tables(n_lat, n_ctx):
    t = np.arange(n_lat)
    pos = np.stack([t // GRID_W, t % GRID_W], axis=-1).astype(np.float32)
    n_freq = DIFF_DIM // 4
    inv = (ROPE_BASE ** (-np.arange(n_freq, dtype=np.float32) / n_freq)).astype(np.float32)
    ang = (pos[:, :, None] * inv).reshape(n_lat, 2 * n_freq)
    lane = np.arange(GROUP_W)
    idx = (lane % DIFF_DIM) // 2
    sign = np.where(lane % 2 == 0, -1.0, 1.0)
    cos = np.concatenate([np.cos(ang)[:, idx], np.ones((n_ctx, GROUP_W))], axis=0)
    sin = np.concatenate([np.sin(ang)[:, idx] * sign, np.zeros((n_ctx, GROUP_W))], axis=0)
    return cos.astype(np.float32), sin.astype(np.float32)


def _lane_vec(v, n=128):
    v = v.reshape(-1).astype(F32)
    return jnp.zeros((1, n), F32).at[0, :v.shape[0]].set(v)


def _route(aff, cap):
    gate, idx = lax.top_k(jnp.swapaxes(aff, 1, 2), cap)
    return gate, idx


def kernel(x, c, ctx, c_ctx, w_mod, b_mod, norm1_g, w_in, na_rpb, dn_conv_w, dn_a_log, dn_dt_bias, dn_norm_g, df_lambda, df_norm_g, ft_w, w_out, norm2_g, w_router, w_gate, w_up, w_down, final_norm_g):
    n_batch, n_lat, _ = x.shape
    n_ctx = ctx.shape[1]
    n_tok = n_lat + n_ctx
    depth = w_mod.shape[0]
    assert n_batch + 1 <= 8 and n_ctx == ROW_TILE and n_lat % ROW_TILE == 0

    xs = jnp.concatenate([x, ctx], axis=1)
    cc = jnp.zeros((8, D_MODEL), F32).at[:n_batch].set(c).at[n_batch].set(c_ctx)
    m_all = _modulation(cc, w_mod, b_mod)

    cos_t, sin_t = (jnp.asarray(a) for a in _rope_tables(n_lat, n_ctx))
    n1 = n_lat // FT_N2
    f1, g_lat, cc_m, sc_m = _ft_tables(n1, FT_N2)
    _, g_ctx, _, _ = _ft_tables(1, n_ctx)
    f1h, f1l = _np_split(f1)
    glh, gll = _np_split(g_lat)
    gch, gcl = _np_split(g_ctx)
    chan = _np_split(cc_m) + _np_split(sc_m)
    cap_lat = EC_FACTOR * n_lat // N_EXPERTS
    cap_ctx = EC_FACTOR * n_ctx // N_EXPERTS

    prev = None
    for l in range(depth):
        lam_init = 0.8 - 0.6 * math.exp(-0.3 * l)
        mods = m_all[l, :n_batch + 1].reshape(n_batch + 1, 6, 1, D_MODEL)
        xs, pbf, pf, pu = _in_proj(xs, prev, norm1_g[l], mods, _reorder_in_columns(w_in[l]), cos_t, sin_t, n_batch, n_tok)

        o_na = _na_attention(pbf, _na_bias_table(na_rpb[l]), n_batch, n_lat, n_ctx)

        qkv = _dn_prep(pf, dn_conv_w[l], n_batch, n_lat, n_ctx)
        o_f, o_b = _dn_scan(qkv, pf, _lane_vec(dn_a_log[l]), _lane_vec(dn_dt_bias[l]), n_batch, n_lat, n_ctx)

        o_df = _df_attention(pbf, df_lambda[l], jnp.tile(df_norm_g[l], GROUP_HEADS).reshape(1, GROUP_W),
                             lam_init, n_batch, n_lat, n_ctx)

        ftw = ft_w[l].astype(BF16)
        y1 = _ft_stage1(pu.reshape(n_batch, n_tok // FT_N2, FT_N2 * GROUP_W), f1h, f1l, n_batch, n1)
        o_lat = _ft_stage2(y1.reshape(n_batch, 2, n1, FT_N2, GROUP_W), glh, gll, chan, ftw, n_batch, n1, FT_N2, 8)
        o_lat = jnp.swapaxes(o_lat, 1, 2).reshape(n_batch, n_lat, GROUP_W)
        u_ctx = pu[:, n_lat:]
        y1c = jnp.stack([u_ctx, jnp.zeros_like(u_ctx)], axis=1)[:, :, None]
        o_ctx = _ft_stage2(y1c, gch, gcl, chan, ftw, n_batch, 1, n_ctx, 1).reshape(n_batch, n_ctx, GROUP_W)
        o_ft = jnp.concatenate([o_lat, o_ctx], axis=1)

        wr = jnp.pad(w_router[l], ((0, 0), (0, 128 - N_EXPERTS)))
        xs, h2, aff = _out_proj(xs, mods, o_na, o_f, o_b, pf, jnp.tile(dn_norm_g[l], GROUP_HEADS).reshape(1, GROUP_W),
                                o_df, o_ft, w_out[l].astype(BF16), norm2_g[l], wr, n_batch, n_tok)

        gate_l, idx_l = _route(aff[:, :n_lat, :N_EXPERTS], cap_lat)
        gate_c, idx_c = _route(aff[:, n_lat:, :N_EXPERTS], cap_ctx)
        idx = jnp.concatenate([idx_l, idx_c + n_lat], axis=2)
        gate = jnp.concatenate([gate_l, gate_c], axis=2)
        cap = cap_lat + cap_ctx
        gate_col = jnp.swapaxes(gate, 0, 1).reshape(N_EXPERTS, n_batch * cap, 1)
        y = _moe_ffn(_moe_gather(h2, idx), w_gate, w_up, w_down, gate_col, l)
        prev = (_moe_combine(y, idx, n_tok), mods)

    return _final_norm(xs, prev[0], prev[1], final_norm_g, n_batch, n_lat)
```

```python
import functools
import math

import numpy as np
import jax
import jax.numpy as jnp
from jax import lax
from jax.experimental import pallas as pl
from jax.experimental.pallas import tpu as pltpu

F32 = jnp.float32
BF16 = jnp.bfloat16

D_MODEL = 1024
DEPTH = 4
GRID_W = 64
GROUP_W = 256
GROUP_HEADS = 4
HEAD_DIM = 64
NA_WIN_H = 8
NA_WIN_W = 16
DN_CONV_K = 5
DN_CHUNK = 64
DIFF_DIM = 32
FT_DIM = 64
N_EXPERTS = 16
EC_FACTOR = 2
D_EXPERT = 2 * D_MODEL
ROPE_BASE = 10000.0
RMS_EPS = 1e-6
NEG = -1e30

ROW_TILE = 256
N_BF = 6 * GROUP_W
N_F32 = 4 * GROUP_W + 128
VMEM_LIMIT = 56 * 1024 * 1024


def _cparams(*sem):
    return pltpu.CompilerParams(dimension_semantics=sem, vmem_limit_bytes=VMEM_LIMIT)


def _split(x):
    hi = x.astype(BF16)
    lo = (x - hi.astype(F32)).astype(BF16)
    return hi, lo


def _dot(a, b):
    return jnp.dot(a, b, preferred_element_type=F32)


def _dot_nt(a, b):
    return lax.dot_general(a, b, (((1,), (1,)), ((), ())), preferred_element_type=F32)


def _dot_tn(a, b):
    return lax.dot_general(a, b, (((0,), (0,)), ((), ())), preferred_element_type=F32)


def _dot3(a, b):
    ah, al = _split(a)
    bh, bl = _split(b)
    return _dot(ah, bh) + (_dot(al, bh) + _dot(ah, bl))


def _dot3_const(ah, al, b):
    bh, bl = _split(b)
    return _dot(ah, bh) + (_dot(al, bh) + _dot(ah, bl))


def _silu(x):
    return x * jax.nn.sigmoid(x)


def _mod_kernel(s_ref, w_ref, b_ref, o_ref):
    s = _silu(s_ref[...])
    o_ref[...] = _dot(s.astype(BF16), w_ref[...].astype(BF16)) + b_ref[...]


def _modulation(cc, w_mod, b_mod):
    depth = w_mod.shape[0]
    nt = 6 * D_MODEL // 1024
    return pl.pallas_call(
        _mod_kernel,
        grid=(depth, nt),
        in_specs=[
            pl.BlockSpec((8, D_MODEL), lambda l, j: (0, 0)),
            pl.BlockSpec((None, D_MODEL, 1024), lambda l, j: (l, 0, j)),
            pl.BlockSpec((None, 1, 1024), lambda l, j: (l, 0, j)),
        ],
        out_specs=pl.BlockSpec((None, 8, 1024), lambda l, j: (l, 0, j)),
        out_shape=jax.ShapeDtypeStruct((depth, 8, 6 * D_MODEL), F32),
        compiler_params=_cparams("parallel", "parallel"),
        name="modulation",
    )(cc, w_mod, b_mod.reshape(depth, 1, 6 * D_MODEL))


def _norm_mod(x, g, shift, scale):
    y = x * lax.rsqrt(jnp.mean(x * x, axis=-1, keepdims=True) + RMS_EPS) * g
    return y * (1.0 + scale) + shift


def _in_kernel(*refs, residual):
    if residual:
        x_ref, moe_ref, mprev_ref, g_ref, m_ref, w_ref, cos_ref, sin_ref, xo_ref, obf_ref, of_ref, ou_ref = refs
        x = x_ref[...] + mprev_ref[5] * moe_ref[...]
        xo_ref[...] = x
    else:
        x_ref, g_ref, m_ref, w_ref, cos_ref, sin_ref, obf_ref, of_ref, ou_ref = refs
        x = x_ref[...]
    h = _norm_mod(x, g_ref[...], m_ref[0], m_ref[1])
    p = _dot(h.astype(BF16), w_ref[...])
    gw = GROUP_W
    lane = lax.broadcasted_iota(jnp.int32, (1, gw), 1)
    even = (lane % 2) == 0
    cos, sin = cos_ref[...], sin_ref[...]

    def rope(t):
        sw = jnp.where(even, pltpu.roll(t, gw - 1, 1), pltpu.roll(t, 1, 1))
        return t * cos + sw * sin

    obf_ref[:, 0:gw] = (p[:, 0:gw] * HEAD_DIM ** -0.5).astype(BF16)
    obf_ref[:, gw:3 * gw] = p[:, gw:3 * gw].astype(BF16)
    obf_ref[:, 3 * gw:4 * gw] = (rope(p[:, 3 * gw:4 * gw]) * DIFF_DIM ** -0.5).astype(BF16)
    obf_ref[:, 4 * gw:5 * gw] = rope(p[:, 4 * gw:5 * gw]).astype(BF16)
    obf_ref[:, 5 * gw:6 * gw] = p[:, 5 * gw:6 * gw].astype(BF16)
    of_ref[:, 0:4 * gw] = p[:, 6 * gw:10 * gw]
    of_ref[:, 4 * gw:4 * gw + 128] = p[:, 11 * gw:11 * gw + 128]
    ou_ref[...] = p[:, 10 * gw:11 * gw]


def _in_proj(x, prev, g, mods, w_perm, cos_t, sin_t, n_batch, n_tok):
    nt = n_tok // ROW_TILE
    ctx_tile = nt - 1
    n_out = w_perm.shape[1]
    row = lambda b, i: (b, i, 0)
    x_spec = pl.BlockSpec((None, ROW_TILE, D_MODEL), row)
    mod_spec = pl.BlockSpec((None, 6, 1, D_MODEL), lambda b, i: (jnp.where(i == ctx_tile, n_batch, b), 0, 0, 0))
    in_specs = [
        pl.BlockSpec((1, D_MODEL), lambda b, i: (0, 0)),
        mod_spec,
        pl.BlockSpec((D_MODEL, n_out), lambda b, i: (0, 0)),
        pl.BlockSpec((ROW_TILE, GROUP_W), lambda b, i: (i, 0)),
        pl.BlockSpec((ROW_TILE, GROUP_W), lambda b, i: (i, 0)),
    ]
    out_specs = [
        pl.BlockSpec((None, ROW_TILE, N_BF), row),
        pl.BlockSpec((None, ROW_TILE, N_F32), row),
        pl.BlockSpec((None, ROW_TILE, GROUP_W), row),
    ]
    out_shape = [
        jax.ShapeDtypeStruct((n_batch, n_tok, N_BF), BF16),
        jax.ShapeDtypeStruct((n_batch, n_tok, N_F32), F32),
        jax.ShapeDtypeStruct((n_batch, n_tok, GROUP_W), F32),
    ]
    args = (g.reshape(1, D_MODEL), mods, w_perm, cos_t, sin_t)
    if prev is None:
        in_specs, args = [x_spec] + in_specs, (x,) + args
    else:
        in_specs, args = [x_spec, x_spec, mod_spec] + in_specs, (x,) + tuple(prev) + args
        out_specs = [x_spec] + out_specs
        out_shape = [jax.ShapeDtypeStruct((n_batch, n_tok, D_MODEL), F32)] + out_shape
    outs = pl.pallas_call(
        functools.partial(_in_kernel, residual=prev is not None),
        grid=(n_batch, nt),
        in_specs=in_specs,
        out_specs=out_specs,
        out_shape=out_shape,
        compiler_params=_cparams("parallel", "parallel"),
        name="in_proj",
    )(*args)
    return tuple(outs) if prev is not None else (x,) + tuple(outs)


def _na_kernel(q_ref, k_ref, v_ref, bias_ref, o_ref, *, n_lat, n_ctx):
    i = pl.program_id(1)
    rows = n_lat // GRID_W
    wh = min(NA_WIN_H, rows)
    n_win = wh * GRID_W
    per = ROW_TILE // GRID_W
    kc = k_ref[n_lat:n_lat + n_ctx, :]
    vc = v_ref[n_lat:n_lat + n_ctx, :]
    lane = lax.broadcasted_iota(jnp.int32, (GRID_W, GROUP_W), 1)
    masks = [lane // HEAD_DIM == h for h in range(GROUP_HEADS)]
    outs = []
    for u in range(per):
        r = i * per + u
        is_lat = r < rows
        base = jnp.where(is_lat, jnp.clip(r - wh // 2, 0, rows - wh), 0)
        slab = jnp.where(is_lat, base - r + NA_WIN_H - 1, NA_WIN_H)
        start = pl.multiple_of(base * GRID_W, GRID_W)
        q = q_ref[u * GRID_W:(u + 1) * GRID_W, :]
        qm = jnp.concatenate([jnp.where(m, q, jnp.zeros_like(q)) for m in masks], axis=0)
        keys = jnp.concatenate([k_ref[pl.ds(start, n_win), :], kc], axis=0)
        vals = jnp.concatenate([v_ref[pl.ds(start, n_win), :], vc], axis=0)
        s = _dot_nt(qm, keys)
        sw = s[:, :n_win] + bias_ref[slab].reshape(GROUP_HEADS * GRID_W, n_win)
        sc = s[:, n_win:]
        m = jnp.maximum(jnp.max(sw, axis=-1, keepdims=True), jnp.max(sc, axis=-1, keepdims=True))
        pw = jnp.exp(sw - m)
        pc = jnp.exp(sc - m)
        l = jnp.sum(pw, axis=-1, keepdims=True) + jnp.sum(pc, axis=-1, keepdims=True)
        res = _dot(jnp.concatenate([pw, pc], axis=1).astype(BF16), vals) / l
        o = jnp.where(masks[0], res[0:GRID_W], 0.0)
        for h in range(1, GROUP_HEADS):
            o = o + jnp.where(masks[h], res[h * GRID_W:(h + 1) * GRID_W], 0.0)
        outs.append(o)
    o_ref[...] = jnp.concatenate(outs, axis=0).astype(BF16)


def _na_attention(pbf, bias_tab, n_batch, n_lat, n_ctx):
    n_tok = n_lat + n_ctx
    return pl.pallas_call(
        functools.partial(_na_kernel, n_lat=n_lat, n_ctx=n_ctx),
        grid=(n_batch, n_tok // ROW_TILE),
        in_specs=[
            pl.BlockSpec((None, ROW_TILE, GROUP_W), lambda b, i: (b, i, 0)),
            pl.BlockSpec((None, n_tok, GROUP_W), lambda b, i: (b, 0, 1)),
            pl.BlockSpec((None, n_tok, GROUP_W), lambda b, i: (b, 0, 2)),
            pl.BlockSpec(bias_tab.shape, lambda b, i: (0, 0, 0, 0)),
        ],
        out_specs=pl.BlockSpec((None, ROW_TILE, GROUP_W), lambda b, i: (b, i, 0)),
        out_shape=jax.ShapeDtypeStruct((n_batch, n_tok, GROUP_W), BF16),
        compiler_params=_cparams("parallel", "arbitrary"),
        name="na_attention",
    )(pbf, pbf, pbf, bias_tab)


def _na_bias_table(rpb):
    wh = NA_WIN_H
    cq = np.arange(GRID_W)
    c0 = np.clip(cq - NA_WIN_W // 2, 0, GRID_W - NA_WIN_W)
    in_win = (cq[None, :] >= c0[:, None]) & (cq[None, :] < c0[:, None] + NA_WIN_W)
    dx = np.clip(cq[None, :] - cq[:, None], 1 - NA_WIN_W, NA_WIN_W - 1) + NA_WIN_W - 1
    dy = np.arange(wh)[:, None] + np.arange(wh)[None, :]
    pick_x = (dx[:, :, None] == np.arange(2 * NA_WIN_W - 1)).astype(np.float32)
    pick_y = (dy[:, :, None] == np.arange(2 * NA_WIN_H - 1)).astype(np.float32)
    hi = lax.Precision.HIGHEST
    cols = jnp.einsum('hyx,qkx->hyqk', rpb.astype(F32), pick_x, precision=hi)
    tab = jnp.einsum('swy,hyqk->shqwk', pick_y, cols, precision=hi)
    tab = jnp.where(in_win[None, None, :, None, :], tab, NEG)
    tab = tab.reshape(wh, GROUP_HEADS, GRID_W, wh * GRID_W)
    return jnp.concatenate([tab, jnp.full((1,) + tab.shape[1:], NEG, F32)], axis=0)


DF_KV = 512


def _df_kernel(lam_ref, q_ref, k_ref, v_ref, g_ref, o_ref, s_ref, *, chunks, lam_init):
    lq1, lk1, lq2, lk2 = lam_ref[0:1, :], lam_ref[1:2, :], lam_ref[2:3, :], lam_ref[3:4, :]
    lam = (jnp.exp(jnp.sum(lq1 * lk1, axis=-1, keepdims=True))
           - jnp.exp(jnp.sum(lq2 * lk2, axis=-1, keepdims=True)) + lam_init)
    q = q_ref[...]
    tq, gw = q.shape
    lane = lax.broadcasted_iota(jnp.int32, (tq, gw), 1)
    row2 = lax.broadcasted_iota(jnp.int32, (2 * tq, 1), 0)
    origin = chunks[0][0]
    zero = jnp.zeros_like(q)

    def head(h, out):
        qm = [jnp.where(lane // DIFF_DIM == 2 * h + mp, q, zero) for mp in range(2)]
        m = [jnp.full((tq, 128), NEG, F32) for _ in range(2)]
        for st, sz in chunks:
            k = k_ref[st:st + sz, :]
            for mp in range(2):
                s = _dot_nt(qm[mp], k)
                s_ref[mp * tq:(mp + 1) * tq, st - origin:st - origin + sz] = s
                for t in range(sz // 128):
                    m[mp] = jnp.maximum(m[mp], s[:, t * 128:(t + 1) * 128])
        m = [jnp.max(x, axis=-1, keepdims=True) for x in m]
        ones_lane = ((h + 1) % GROUP_HEADS) * HEAD_DIM
        acc = [jnp.zeros((tq, gw), F32) for _ in range(2)]
        for st, sz in chunks:
            v = v_ref[st:st + sz, :]
            lane_v = lax.broadcasted_iota(jnp.int32, v.shape, 1)
            v = jnp.where(lane_v == ones_lane, jnp.ones_like(v), v)
            for mp in range(2):
                e = jnp.exp((s_ref[mp * tq:(mp + 1) * tq, st - origin:st - origin + sz] - m[mp]).astype(BF16))
                acc[mp] = acc[mp] + _dot(e, v)
        w = []
        for mp in range(2):
            l = jnp.sum(jnp.where(lane == ones_lane, acc[mp], 0.0), axis=-1, keepdims=True)
            w.append(acc[mp] * ((1.0 if mp == 0 else lam) / l))
        return out + jnp.where(lane // HEAD_DIM == h, w[0] - w[1], 0.0)

    o = lax.fori_loop(0, GROUP_HEADS, head, jnp.zeros((tq, gw), F32))
    sq = o * o
    scale = jnp.zeros((tq, gw), F32)
    for h in range(GROUP_HEADS):
        in_head = lane // HEAD_DIM == h
        ms = jnp.sum(jnp.where(in_head, sq, 0.0), axis=-1, keepdims=True) * (1.0 / HEAD_DIM)
        scale = jnp.where(in_head, lax.rsqrt(ms + RMS_EPS), scale)
    o_ref[...] = (o * scale * g_ref[...] * (1.0 - lam_init)).astype(BF16)


def _df_call(pbf, df_lambda, g_tiled, lam_init, n_batch, n_tok, first_tile, n_tiles, chunks, name):
    n_keys = sum(sz for _, sz in chunks)
    return pl.pallas_call(
        functools.partial(_df_kernel, chunks=chunks, lam_init=lam_init),
        grid=(n_batch, n_tiles),
        in_specs=[
            pl.BlockSpec((4, DIFF_DIM), lambda b, i: (0, 0)),
            pl.BlockSpec((None, ROW_TILE, GROUP_W), lambda b, i: (b, first_tile + i, 3)),
            pl.BlockSpec((None, n_tok, GROUP_W), lambda b, i: (b, 0, 4)),
            pl.BlockSpec((None, n_tok, GROUP_W), lambda b, i: (b, 0, 5)),
            pl.BlockSpec((1, GROUP_W), lambda b, i: (0, 0)),
        ],
        out_specs=pl.BlockSpec((None, ROW_TILE, GROUP_W), lambda b, i: (b, i, 0)),
        out_shape=jax.ShapeDtypeStruct((n_batch, n_tiles * ROW_TILE, GROUP_W), BF16),
        scratch_shapes=[pltpu.VMEM((2 * ROW_TILE, n_keys), F32)],
        compiler_params=_cparams("parallel", "arbitrary"),
        name=name,
    )(df_lambda, pbf, pbf, pbf, g_tiled)


def _df_attention(pbf, df_lambda, g_tiled, lam_init, n_batch, n_lat, n_ctx):
    n_tok = n_lat + n_ctx
    lat_chunks = tuple((st, min(DF_KV, n_tok - st)) for st in range(0, n_tok, DF_KV))
    o_lat = _df_call(pbf, df_lambda, g_tiled, lam_init, n_batch, n_tok, 0, n_lat // ROW_TILE, lat_chunks, "df_attention")
    o_ctx = _df_call(pbf, df_lambda, g_tiled, lam_init, n_batch, n_tok, n_lat // ROW_TILE, n_ctx // ROW_TILE,
                     ((n_lat, n_ctx),), "df_attention_ctx")
    return jnp.concatenate([o_lat, o_ctx], axis=1)


CONV_PAD = 8


def _dn_prep_kernel(u_ref, w_ref, o_ref, pad_ref, *, n_lat, n_ctx):
    j = pl.program_id(1)
    zeros = jnp.zeros((CONV_PAD, GROUP_W), F32)
    lat0 = CONV_PAD
    ctx0 = 2 * CONV_PAD + n_lat
    pad_ref[0:CONV_PAD, :] = zeros
    pad_ref[lat0 + n_lat:ctx0, :] = zeros
    pad_ref[ctx0 + n_ctx:ctx0 + n_ctx + CONV_PAD, :] = zeros
    pad_ref[lat0:lat0 + n_lat, :] = u_ref[0:n_lat, :]
    pad_ref[ctx0:ctx0 + n_ctx, :] = u_ref[n_lat:n_lat + n_ctx, :]
    qscale = jnp.where(j == 0, HEAD_DIM ** -0.5, 1.0)
    half = DN_CONV_K // 2
    for (src0, dst0, n) in ((lat0, 0, n_lat), (ctx0, n_lat, n_ctx)):
        for t0 in range(0, n, ROW_TILE):
            acc = jnp.zeros((ROW_TILE, GROUP_W), F32)
            for tap in range(DN_CONV_K):
                a = src0 + t0 + tap - half
                acc = acc + pad_ref[a:a + ROW_TILE, :] * w_ref[tap:tap + 1, :]
            y = _silu(acc)
            parts = []
            for h in range(GROUP_HEADS):
                yh = y[:, h * HEAD_DIM:(h + 1) * HEAD_DIM]
                nrm = lax.rsqrt(jnp.sum(yh * yh, axis=-1, keepdims=True) + 1e-6) * qscale
                parts.append(yh * jnp.where(j == 2, 1.0, nrm))
            o_ref[dst0 + t0:dst0 + t0 + ROW_TILE, :] = jnp.concatenate(parts, axis=-1)


def _dn_prep(pf, conv_w, n_batch, n_lat, n_ctx):
    n_tok = n_lat + n_ctx
    return pl.pallas_call(
        functools.partial(_dn_prep_kernel, n_lat=n_lat, n_ctx=n_ctx),
        grid=(n_batch, 3),
        in_specs=[
            pl.BlockSpec((None, n_tok, GROUP_W), lambda b, j: (b, 0, j)),
            pl.BlockSpec((DN_CONV_K, GROUP_W), lambda b, j: (0, j)),
        ],
        out_specs=pl.BlockSpec((None, n_tok, GROUP_W), lambda b, j: (b, 0, j)),
        out_shape=jax.ShapeDtypeStruct((n_batch, n_tok, 3 * GROUP_W), F32),
        scratch_shapes=[pltpu.VMEM((n_tok + 3 * CONV_PAD, GROUP_W), F32)],
        compiler_params=_cparams("parallel", "arbitrary"),
        name="dn_prep",
    )(pf, conv_w)


DN_BLOCK = 4 * DN_CHUNK


def _split3(x):
    x1 = x.astype(BF16)
    r = x - x1.astype(F32)
    x2 = r.astype(BF16)
    return x1, x2, (r - x2.astype(F32)).astype(BF16)


def _block_diag(x, masks):
    zero = jnp.zeros_like(x)
    return jnp.concatenate([jnp.where(m, x, zero) for m in masks], axis=0)


def _mm3_heads(lhs, rhs, masks):
    lh, ll = _split(lhs)
    rh, rl = _split(rhs)
    bh, bl = _block_diag(rh, masks), _block_diag(rl, masks)
    return _dot(jnp.concatenate([lh, ll, lh], axis=1), jnp.concatenate([bh, bh, bl], axis=0))


def _dn_kernel(qf_ref, kf_ref, vf_ref, abf_ref, qb_ref, kb_ref, vb_ref, abb_ref, av_ref, dt_ref, exp_ref,
               of_ref, ob_ref, s_ref):
    @pl.when(pl.program_id(1) == 0)
    def _():
        s_ref[...] = jnp.zeros_like(s_ref)

    c, nb, gw = DN_CHUNK, DN_BLOCK, GROUP_W
    n_sub = nb // c
    lane4 = lax.broadcasted_iota(jnp.int32, (c, gw), 1)
    masks = [(lane4 // HEAD_DIM) == h for h in range(GROUP_HEADS)]
    ri = lax.broadcasted_iota(jnp.int32, (c, gw), 0)
    cj = lane4 % HEAD_DIM
    eye = (ri == cj).astype(F32)
    bi = lax.broadcasted_iota(jnp.int32, (nb, nb), 0)
    bj = lax.broadcasted_iota(jnp.int32, (nb, nb), 1)
    same_chunk = (bi // c) == (bj // c)
    lane1 = lax.broadcasted_iota(jnp.int32, (nb, 128), 1)
    dirs = ((qf_ref, kf_ref, vf_ref, abf_ref), (qb_ref, kb_ref, vb_ref, abb_ref))

    chains = []
    for d, (q_ref, k_ref, v_ref, ab_ref) in enumerate(dirs):
        q_all, k_all, v_all = q_ref[...], k_ref[...], v_ref[...]
        for j in range(n_sub):
            rows = slice(j * c, (j + 1) * c)
            q, k = q_all[rows], k_all[rows]
            kq = _dot_nt(jnp.concatenate([k, q], axis=0).astype(BF16), _block_diag(k.astype(BF16), masks))
            chains.append(dict(d=d, rows=rows, q=q, k=k, v=v_all[rows], kq=kq))
    gates = []
    for d, (q_ref, k_ref, v_ref, ab_ref) in enumerate(dirs):
        ab = ab_ref[...]
        x = ab + dt_ref[...]
        softplus = jnp.maximum(x, 0.0) + jnp.log1p(jnp.exp(-jnp.abs(x)))
        g = -jnp.exp(av_ref[...]) * softplus
        tri = (same_chunk & ((bi >= bj) if d == 0 else (bi <= bj))).astype(BF16)
        g1, g2, g3 = _split3(g)
        gcs = _dot(tri, g1) + (_dot(tri, g2) + _dot(tri, g3))
        gates.append(jnp.where(lane1 < 2 * GROUP_HEADS, gcs, jax.nn.sigmoid(ab)))
    spread = []
    for d in range(2):
        x1, x2, x3 = _split3(gates[d])
        e = exp_ref[d]
        spread.append(_dot(x1, e) + (_dot(x2, e) + _dot(x3, e)))
    for ch in chains:
        d, q, k, v, kq = ch["d"], ch["q"], ch["k"], ch["v"], ch["kq"]
        incl = (ri >= cj) if d == 0 else (ri <= cj)
        strict = (ri > cj) if d == 0 else (ri < cj)
        last = c - 1 if d == 0 else 0
        gc, beta = spread[d][ch["rows"], :gw], spread[d][ch["rows"], gw:]
        gc_row = jnp.sum(gc * eye, axis=0, keepdims=True)
        decay = jnp.exp(jnp.where(incl, gc - gc_row, NEG))
        eg = jnp.exp(gc)
        g_last = gc[last:last + 1, :]
        ch.update(
            lm=jnp.where(strict, kq[:c] * beta * decay, 0.0),
            qk=jnp.where(incl, kq[c:] * decay, 0.0).astype(BF16),
            rhs=jnp.concatenate([_block_diag((v * beta).astype(BF16), masks),
                                 _block_diag((k * beta * eg).astype(BF16), masks)], axis=1),
            q_dec=q * eg,
            k_dec=(k * jnp.exp(g_last - gc)).astype(BF16),
            g_end=jnp.exp(g_last))

    def neumann_level(first):
        for ch in chains:
            if first:
                ch["p"] = eye - ch["lm"]
                ch["sq"] = _mm3_heads(ch["lm"], ch["lm"], masks)
            else:
                r = _mm3_heads(jnp.concatenate([ch["sq"], ch["p"]], axis=0), ch["sq"], masks)
                ch["sq"], ch["p"] = r[:c], ch["p"] + r[c:]

    def head_blocks(full):
        out = jnp.where(masks[0], full[0:c], 0.0)
        for h in range(1, GROUP_HEADS):
            out = out + jnp.where(masks[h], full[h * c:(h + 1) * c], 0.0)
        return out

    n_levels = int(math.log2(c)) - 1
    for level in range(n_levels):
        neumann_level(level == 0)
    t_inv = [ch["p"] + _mm3_heads(ch["p"], ch["sq"], masks) for ch in chains]
    uw = [_dot(t.astype(BF16), ch["rhs"]).astype(BF16) for t, ch in zip(t_inv, chains)]
    kd_uw = [_dot_tn(ch["k_dec"], x) for x, ch in zip(uw, chains)]
    qk_uw = [_dot(ch["qk"], jnp.concatenate([_block_diag(x[:, :gw], masks), _block_diag(x[:, gw:], masks)], axis=1))
             for x, ch in zip(uw, chains)]
    solved = [dict(ms=jnp.concatenate([head_blocks(kd[:, gw:]), ch["q_dec"] - qq[:, gw:]], axis=0).astype(BF16),
                   b=head_blocks(kd[:, :gw]), d=qq[:, :gw], g_end=ch["g_end"])
              for kd, qq, ch in zip(kd_uw, qk_uw, chains)]

    states = [s_ref[0], s_ref[1]]
    outs = [[None] * n_sub, [None] * n_sub]
    for step in range(n_sub):
        for d in range(2):
            j = step if d == 0 else n_sub - 1 - step
            ch = solved[d * n_sub + j]
            s = states[d]
            res = _dot(ch["ms"], _block_diag(s.astype(BF16), masks))
            outs[d][j] = res[c:] + ch["d"]
            states[d] = s * ch["g_end"] - res[:c] + ch["b"]
    of_ref[...] = jnp.concatenate(outs[0], axis=0)
    ob_ref[...] = jnp.concatenate(outs[1], axis=0)
    s_ref[...] = jnp.stack(states, axis=0)


def _dn_scan(qkv, pf, a_vec, dt_vec, n_batch, n_lat, n_ctx):
    n_tok = n_lat + n_ctx
    assert n_ctx == DN_BLOCK and n_lat % DN_BLOCK == 0
    nblk = n_tok // DN_BLOCK
    ab_blk = 4 * GROUP_W // 128

    def fwd(n):
        return jnp.where(n == 0, nblk - 1, n - 1)

    def bwd(n):
        return jnp.where(n == 0, nblk - 1, nblk - 1 - n)

    def spec(order, blk, width):
        return pl.BlockSpec((None, DN_BLOCK, width), lambda b, n: (b, order(n), blk))

    spread = np.zeros((2, 128, 2 * GROUP_W), np.float32)
    for d in range(2):
        for h in range(GROUP_HEADS):
            spread[d, GROUP_HEADS * d + h, h * HEAD_DIM:(h + 1) * HEAD_DIM] = 1.0
            spread[d, 2 * GROUP_HEADS + GROUP_HEADS * d + h, GROUP_W + h * HEAD_DIM:GROUP_W + (h + 1) * HEAD_DIM] = 1.0

    lane = pl.BlockSpec((1, 128), lambda b, n: (0, 0))
    out = jax.ShapeDtypeStruct((n_batch, n_tok, GROUP_W), F32)
    return pl.pallas_call(
        _dn_kernel,
        grid=(n_batch, nblk),
        in_specs=[spec(fwd, 0, GROUP_W), spec(fwd, 1, GROUP_W), spec(fwd, 2, GROUP_W), spec(fwd, ab_blk, 128),
                  spec(bwd, 0, GROUP_W), spec(bwd, 1, GROUP_W), spec(bwd, 2, GROUP_W), spec(bwd, ab_blk, 128),
                  lane, lane, pl.BlockSpec((2, 128, 2 * GROUP_W), lambda b, n: (0, 0, 0))],
        out_specs=[spec(fwd, 0, GROUP_W), spec(bwd, 0, GROUP_W)],
        out_shape=[out, out],
        scratch_shapes=[pltpu.VMEM((2, HEAD_DIM, GROUP_W), F32)],
        compiler_params=_cparams("parallel", "arbitrary"),
        name="dn_scan",
    )(qkv, qkv, qkv, pf, qkv, qkv, qkv, pf, a_vec, dt_vec, jnp.asarray(spread, BF16))


FT_N2 = 64


def _ft1_kernel(x_ref, fh_ref, fl_ref, o_ref):
    o_ref[...] = _dot3_const(fh_ref[...], fl_ref[...], x_ref[...])


def _ft_stage1(u_rows, f_hi, f_lo, n_batch, n1, tn=2048):
    width = u_rows.shape[2]
    return pl.pallas_call(
        _ft1_kernel,
        grid=(n_batch, width // tn),
        in_specs=[
            pl.BlockSpec((None, n1, tn), lambda b, j: (b, 0, j)),
            pl.BlockSpec((2 * n1, n1), lambda b, j: (0, 0)),
            pl.BlockSpec((2 * n1, n1), lambda b, j: (0, 0)),
        ],
        out_specs=pl.BlockSpec((None, 2 * n1, tn), lambda b, j: (b, 0, j)),
        out_shape=jax.ShapeDtypeStruct((n_batch, 2 * n1, width), F32),
        compiler_params=_cparams("parallel", "parallel"),
        name="ft_stage1",
    )(u_rows, f_hi, f_lo)


def _ft3_kernel(y_ref, gh_ref, gl_ref, ch_ref, cl_ref, sh_ref, sl_ref, w_ref, o_ref, *, pb, n2, norm):
    vr, vi = [], []
    for p in range(pb):
        rhs = jnp.concatenate([y_ref[0, p], y_ref[1, p]], axis=0)
        v = _dot3_const(gh_ref[p], gl_ref[p], rhs)
        vr.append(v[:n2])
        vi.append(v[n2:])
    vr = jnp.concatenate(vr, axis=0)
    vi = jnp.concatenate(vi, axis=0)
    vrh, vrl = _split(vr)
    vih, vil = _split(vi)
    y = (_dot(vrh, ch_ref[...]) + (_dot(vrl, ch_ref[...]) + _dot(vrh, cl_ref[...]))
         + _dot(vih, sh_ref[...]) + (_dot(vil, sh_ref[...]) + _dot(vih, sl_ref[...]))) * norm
    o = _dot(y.astype(BF16), w_ref[...])
    o_ref[...] = o.reshape(pb, n2, GROUP_W).astype(BF16)


def _ft_stage2(y1, g_hi, g_lo, chan, ft_w, n_batch, n1, n2, pb):
    ch, cl, sh, sl = chan
    full = lambda b, i: (0, 0)
    return pl.pallas_call(
        functools.partial(_ft3_kernel, pb=pb, n2=n2, norm=1.0 / math.sqrt(n1 * n2 * FT_DIM)),
        grid=(n_batch, n1 // pb),
        in_specs=[
            pl.BlockSpec((None, 2, pb, n2, GROUP_W), lambda b, i: (b, 0, i, 0, 0)),
            pl.BlockSpec((pb, 2 * n2, 2 * n2), lambda b, i: (i, 0, 0)),
            pl.BlockSpec((pb, 2 * n2, 2 * n2), lambda b, i: (i, 0, 0)),
            pl.BlockSpec((GROUP_W, GROUP_W), full), pl.BlockSpec((GROUP_W, GROUP_W), full),
            pl.BlockSpec((GROUP_W, GROUP_W), full), pl.BlockSpec((GROUP_W, GROUP_W), full),
            pl.BlockSpec((GROUP_W, GROUP_W), full),
        ],
        out_specs=pl.BlockSpec((None, pb, n2, GROUP_W), lambda b, i: (b, i, 0, 0)),
        out_shape=jax.ShapeDtypeStruct((n_batch, n1, n2, GROUP_W), BF16),
        compiler_params=_cparams("parallel", "parallel"),
        name="ft_stage2",
    )(y1, g_hi, g_lo, ch, cl, sh, sl, ft_w)


def _np_split(a):
    hi = jnp.asarray(a, F32).astype(BF16)
    lo = (jnp.asarray(a, F32) - hi.astype(F32)).astype(BF16)
    return hi, lo


@functools.lru_cache(maxsize=None)
def _ft_tables(n1, n2):
    n = n1 * n2
    p = np.arange(n1)
    ang1 = 2.0 * np.pi * ((p[:, None] * p[None, :]) % n1) / n1
    f1 = np.concatenate([np.cos(ang1), -np.sin(ang1)], axis=0)
    q = np.arange(n2)
    b = np.arange(n2)
    phase = (b[None, None, :] * (p[:, None, None] + n1 * q[None, :, None])) % n
    psi = 2.0 * np.pi * phase / n
    gc, gs = np.cos(psi), np.sin(psi)
    g = np.concatenate([np.concatenate([gc, gs], axis=2), np.concatenate([-gs, gc], axis=2)], axis=1)
    c = np.arange(FT_DIM)
    angc = 2.0 * np.pi * ((c[:, None] * c[None, :]) % FT_DIM) / FT_DIM
    eye = np.eye(GROUP_W // FT_DIM)
    cc, sc = np.kron(eye, np.cos(angc)), np.kron(eye, np.sin(angc))
    return f1.astype(np.float32), g.astype(np.float32), cc.astype(np.float32), sc.astype(np.float32)


def _out_kernel(x_ref, m_ref, na_ref, of_ref, ob_ref, gate_ref, dng_ref, df_ref, ft_ref, w_ref, g2_ref, wr_ref,
                xo_ref, h_ref, aff_ref):
    rows = x_ref.shape[0] // 2
    xs, hs, affs = [], [], []
    for part in range(2):
        r = slice(part * rows, (part + 1) * rows)
        o = of_ref[r, :] + ob_ref[r, :]
        parts = []
        for h in range(GROUP_HEADS):
            oh = o[:, h * HEAD_DIM:(h + 1) * HEAD_DIM]
            parts.append(oh * lax.rsqrt(jnp.mean(oh * oh, axis=-1, keepdims=True) + RMS_EPS))
        y_dn = jnp.concatenate(parts, axis=-1) * dng_ref[...] * _silu(gate_ref[r, :])
        mix = jnp.concatenate([na_ref[r, :], y_dn.astype(BF16), df_ref[r, :], ft_ref[r, :]], axis=-1)
        x = x_ref[r, :] + m_ref[2] * _dot(mix, w_ref[...])
        xs.append(x)
        h2 = _norm_mod(x, g2_ref[...], m_ref[3], m_ref[4])
        bits = pltpu.bitcast(h2.astype(BF16).astype(F32), jnp.uint32)
        half = D_MODEL // 2
        hs.append(bits[:, half:] | (bits[:, :half] >> 16))
        logits = _dot3(h2, wr_ref[...])
        lane = lax.broadcasted_iota(jnp.int32, logits.shape, 1)
        logits = jnp.where(lane < N_EXPERTS, logits, NEG)
        e = jnp.exp(logits - jnp.max(logits, axis=-1, keepdims=True))
        affs.append(e / jnp.sum(e, axis=-1, keepdims=True))
    xo_ref[...] = jnp.concatenate(xs, axis=0)
    h_ref[...] = jnp.concatenate(hs, axis=0)
    aff_ref[...] = jnp.concatenate(affs, axis=0)


def _out_proj(x, mods, o_na, o_f, o_b, pf, dn_g, o_df, o_ft, w_out, g2, w_router, n_batch, n_tok):
    nt = n_tok // ROW_TILE
    ctx_tile = nt - 1
    row = lambda b, i: (b, i, 0)
    blk = lambda: pl.BlockSpec((None, ROW_TILE, GROUP_W), row)
    vec = lambda n: pl.BlockSpec((1, n), lambda b, i: (0, 0))
    return pl.pallas_call(
        _out_kernel,
        grid=(n_batch, nt),
        in_specs=[
            pl.BlockSpec((None, ROW_TILE, D_MODEL), row),
            pl.BlockSpec((None, 6, 1, D_MODEL), lambda b, i: (jnp.where(i == ctx_tile, n_batch, b), 0, 0, 0)),
            blk(), blk(), blk(),
            pl.BlockSpec((None, ROW_TILE, GROUP_W), lambda b, i: (b, i, 3)),
            vec(GROUP_W), blk(), blk(),
            pl.BlockSpec((D_MODEL, D_MODEL), lambda b, i: (0, 0)),
            vec(D_MODEL),
            pl.BlockSpec((D_MODEL, 128), lambda b, i: (0, 0)),
        ],
        out_specs=[
            pl.BlockSpec((None, ROW_TILE, D_MODEL), row),
            pl.BlockSpec((None, ROW_TILE, D_MODEL // 2), row),
            pl.BlockSpec((None, ROW_TILE, 128), row),
        ],
        out_shape=[
            jax.ShapeDtypeStruct((n_batch, n_tok, D_MODEL), F32),
            jax.ShapeDtypeStruct((n_batch, n_tok, D_MODEL // 2), jnp.uint32),
            jax.ShapeDtypeStruct((n_batch, n_tok, 128), F32),
        ],
        compiler_params=_cparams("parallel", "parallel"),
        name="out_proj_router",
    )(x, mods, o_na, o_f, o_b, pf, dn_g, o_df, o_ft, w_out, g2.reshape(1, D_MODEL), w_router)


MOE_TF = 256
GATHER_UNROLL = 8


def _gather_kernel(idx_ref, h_ref, o_ref, *, cap, n_e):
    base = (pl.program_id(0) * n_e + pl.program_id(1)) * cap

    def body(j, carry):
        for u in range(GATHER_UNROLL):
            o_ref[j, u:u + 1, :] = h_ref[pl.ds(idx_ref[base + j * GATHER_UNROLL + u], 1), :]
        return carry

    lax.fori_loop(0, cap // GATHER_UNROLL, body, 0)


def _moe_gather(hp, idx):
    n_batch, n_tok, width = hp.shape
    _, n_e, cap = idx.shape
    assert cap % GATHER_UNROLL == 0 and GATHER_UNROLL == 8
    groups = cap // GATHER_UNROLL
    out = pl.pallas_call(
        functools.partial(_gather_kernel, cap=cap, n_e=n_e),
        grid_spec=pltpu.PrefetchScalarGridSpec(
            num_scalar_prefetch=1,
            grid=(n_batch, n_e),
            in_specs=[pl.BlockSpec((None, n_tok, width), lambda b, e, idx_ref: (b, 0, 0))],
            out_specs=pl.BlockSpec((None, groups, GATHER_UNROLL, width), lambda b, e, idx_ref: (e, b, 0, 0)),
        ),
        out_shape=jax.ShapeDtypeStruct((n_e, n_batch * groups, GATHER_UNROLL, width), jnp.uint32),
        compiler_params=_cparams("parallel", "arbitrary"),
        name="moe_gather",
    )(idx.reshape(-1), hp)
    return out.reshape(n_e, n_batch * cap, width)


def _combine_kernel(idx_ref, y_ref, o_ref, *, cap, n_e):
    e = pl.program_id(1)

    @pl.when(e == 0)
    def _():
        o_ref[...] = jnp.zeros_like(o_ref)

    base = (pl.program_id(0) * n_e + e) * cap

    def body(j, carry):
        first = j * GATHER_UNROLL
        toks = [idx_ref[base + first + u] for u in range(GATHER_UNROLL)]
        rows = [o_ref[pl.ds(t, 1), :] + y_ref[j, u:u + 1, :] for u, t in enumerate(toks)]
        for t, row in zip(toks, rows):
            o_ref[pl.ds(t, 1), :] = row
        return carry

    lax.fori_loop(0, cap // GATHER_UNROLL, body, 0)


def _moe_combine(y, idx, n_tok):
    n_batch, n_e, cap = idx.shape
    d = y.shape[2]
    groups = cap // GATHER_UNROLL
    return pl.pallas_call(
        functools.partial(_combine_kernel, cap=cap, n_e=n_e),
        grid_spec=pltpu.PrefetchScalarGridSpec(
            num_scalar_prefetch=1,
            grid=(n_batch, n_e),
            in_specs=[pl.BlockSpec((None, groups, GATHER_UNROLL, d), lambda b, e, idx_ref: (e, b, 0, 0))],
            out_specs=pl.BlockSpec((None, n_tok, d), lambda b, e, idx_ref: (b, 0, 0)),
        ),
        out_shape=jax.ShapeDtypeStruct((n_batch, n_tok, d), F32),
        compiler_params=_cparams("parallel", "arbitrary"),
        name="moe_combine",
    )(idx.reshape(-1), y.reshape(n_e, n_batch * groups, GATHER_UNROLL, d))


def _moe_kernel(x_ref, wg_ref, wu_ref, wd_ref, gate_ref, o_ref, xb_ref):
    f = pl.program_id(1)

    @pl.when(f == 0)
    def _():
        packed = x_ref[...]
        xb_ref[...] = jnp.concatenate([pltpu.bitcast(packed << 16, F32),
                                       pltpu.bitcast(packed & jnp.uint32(0xFFFF0000), F32)], axis=1).astype(BF16)
        o_ref[...] = jnp.zeros_like(o_ref)

    x = xb_ref[...]
    a = _dot(x, wg_ref[...].astype(BF16))
    u = _dot(x, wu_ref[...].astype(BF16))
    hid = (_silu(a) * u).astype(BF16)
    o_ref[...] += _dot(hid, wd_ref[...].astype(BF16))

    @pl.when(f == pl.num_programs(1) - 1)
    def _():
        o_ref[...] = o_ref[...] * gate_ref[...]


def _moe_ffn(xg, w_gate, w_up, w_down, gate, layer):
    n_e, m, _ = xg.shape
    return pl.pallas_call(
        _moe_kernel,
        grid=(n_e, D_EXPERT // MOE_TF),
        in_specs=[
            pl.BlockSpec((None, m, D_MODEL // 2), lambda e, f: (e, 0, 0)),
            pl.BlockSpec((None, None, D_MODEL, MOE_TF), lambda e, f: (layer, e, 0, f)),
            pl.BlockSpec((None, None, D_MODEL, MOE_TF), lambda e, f: (layer, e, 0, f)),
            pl.BlockSpec((None, None, MOE_TF, D_MODEL), lambda e, f: (layer, e, f, 0)),
            pl.BlockSpec((None, m, 1), lambda e, f: (e, 0, 0)),
        ],
        out_specs=pl.BlockSpec((None, m, D_MODEL), lambda e, f: (e, 0, 0)),
        out_shape=jax.ShapeDtypeStruct((n_e, m, D_MODEL), F32),
        scratch_shapes=[pltpu.VMEM((m, D_MODEL), BF16)],
        compiler_params=_cparams("parallel", "arbitrary"),
        name="moe_ffn",
    )(xg, w_gate, w_up, w_down, gate)


def _final_kernel(x_ref, moe_ref, m_ref, g_ref, o_ref):
    x = x_ref[...] + m_ref[5] * moe_ref[...]
    o_ref[...] = x * lax.rsqrt(jnp.mean(x * x, axis=-1, keepdims=True) + RMS_EPS) * g_ref[...]


def _final_norm(x, moe, mods, g, n_batch, n_lat):
    row = lambda b, i: (b, i, 0)
    blk = pl.BlockSpec((None, ROW_TILE, D_MODEL), row)
    return pl.pallas_call(
        _final_kernel,
        grid=(n_batch, n_lat // ROW_TILE),
        in_specs=[blk, blk, pl.BlockSpec((None, 6, 1, D_MODEL), lambda b, i: (b, 0, 0, 0)),
                  pl.BlockSpec((1, D_MODEL), lambda b, i: (0, 0))],
        out_specs=blk,
        out_shape=jax.ShapeDtypeStruct((n_batch, n_lat, D_MODEL), F32),
        compiler_params=_cparams("parallel", "parallel"),
        name="final_norm",
    )(x, moe, mods, g.reshape(1, D_MODEL))


def _reorder_in_columns(w):
    gw = GROUP_W
    o = 6 * gw + 4 * GROUP_HEADS
    pad = jnp.zeros((w.shape[0], 128 - 4 * GROUP_HEADS), w.dtype)
    parts = [w[:, 0:3 * gw], w[:, o + gw:o + 4 * gw], w[:, 3 * gw:6 * gw], w[:, o:o + gw],
             w[:, o + 4 * gw:o + 5 * gw], w[:, 6 * gw:o], pad]
    return jnp.concatenate(parts, axis=1).astype(BF16)


@functools.lru_cache(maxsize=None)
def _rope_tables(n_lat, n_ctx):
    t = np.arange(n_lat)
    pos = np.stack([t // GRID_W, t % GRID_W], axis=-1).astype(np.float32)
    n_freq = DIFF_DIM // 4
    inv = (ROPE_BASE ** (-np.arange(n_freq, dtype=np.float32) / n_freq)).astype(np.float32)
    ang = (pos[:, :, None] * inv).reshape(n_lat, 2 * n_freq)
    lane = np.arange(GROUP_W)
    idx = (lane % DIFF_DIM) // 2
    sign = np.where(lane % 2 == 0, -1.0, 1.0)
    cos = np.concatenate([np.cos(ang)[:, idx], np.ones((n_ctx, GROUP_W))], axis=0)
    sin = np.concatenate([np.sin(ang)[:, idx] * sign, np.zeros((n_ctx, GROUP_W))], axis=0)
    return cos.astype(np.float32), sin.astype(np.float32)


def _lane_vec(v, n=128):
    v = v.reshape(-1).astype(F32)
    return jnp.zeros((1, n), F32).at[0, :v.shape[0]].set(v)


def _route(aff, cap):
    gate, idx = lax.top_k(jnp.swapaxes(aff, 1, 2), cap)
    return gate, idx


def kernel(x, c, ctx, c_ctx, w_mod, b_mod, norm1_g, w_in, na_rpb, dn_conv_w, dn_a_log, dn_dt_bias, dn_norm_g, df_lambda, df_norm_g, ft_w, w_out, norm2_g, w_router, w_gate, w_up, w_down, final_norm_g):
    n_batch, n_lat, _ = x.shape
    n_ctx = ctx.shape[1]
    n_tok = n_lat + n_ctx
    depth = w_mod.shape[0]
    assert n_batch + 1 <= 8 and n_ctx == ROW_TILE and n_lat % ROW_TILE == 0

    xs = jnp.concatenate([x, ctx], axis=1)
    cc = jnp.zeros((8, D_MODEL), F32).at[:n_batch].set(c).at[n_batch].set(c_ctx)
    m_all = _modulation(cc, w_mod, b_mod)

    cos_t, sin_t = (jnp.asarray(a) for a in _rope_tables(n_lat, n_ctx))
    n1 = n_lat // FT_N2
    f1, g_lat, cc_m, sc_m = _ft_tables(n1, FT_N2)
    _, g_ctx, _, _ = _ft_tables(1, n_ctx)
    f1h, f1l = _np_split(f1)
    glh, gll = _np_split(g_lat)
    gch, gcl = _np_split(g_ctx)
    chan = _np_split(cc_m) + _np_split(sc_m)
    cap_lat = EC_FACTOR * n_lat // N_EXPERTS
    cap_ctx = EC_FACTOR * n_ctx // N_EXPERTS

    prev = None
    for l in range(depth):
        lam_init = 0.8 - 0.6 * math.exp(-0.3 * l)
        mods = m_all[l, :n_batch + 1].reshape(n_batch + 1, 6, 1, D_MODEL)
        xs, pbf, pf, pu = _in_proj(xs, prev, norm1_g[l], mods, _reorder_in_columns(w_in[l]), cos_t, sin_t, n_batch, n_tok)

        o_na = _na_attention(pbf, _na_bias_table(na_rpb[l]), n_batch, n_lat, n_ctx)

        qkv = _dn_prep(pf, dn_conv_w[l], n_batch, n_lat, n_ctx)
        o_f, o_b = _dn_scan(qkv, pf, _lane_vec(dn_a_log[l]), _lane_vec(dn_dt_bias[l]), n_batch, n_lat, n_ctx)

        o_df = _df_attention(pbf, df_lambda[l], jnp.tile(df_norm_g[l], GROUP_HEADS).reshape(1, GROUP_W),
                             lam_init, n_batch, n_lat, n_ctx)

        ftw = ft_w[l].astype(BF16)
        y1 = _ft_stage1(pu.reshape(n_batch, n_tok // FT_N2, FT_N2 * GROUP_W), f1h, f1l, n_batch, n1)
        o_lat = _ft_stage2(y1.reshape(n_batch, 2, n1, FT_N2, GROUP_W), glh, gll, chan, ftw, n_batch, n1, FT_N2, 8)
        o_lat = jnp.swapaxes(o_lat, 1, 2).reshape(n_batch, n_lat, GROUP_W)
        u_ctx = pu[:, n_lat:]
        y1c = jnp.stack([u_ctx, jnp.zeros_like(u_ctx)], axis=1)[:, :, None]
        o_ctx = _ft_stage2(y1c, gch, gcl, chan, ftw, n_batch, 1, n_ctx, 1).reshape(n_batch, n_ctx, GROUP_W)
        o_ft = jnp.concatenate([o_lat, o_ctx], axis=1)

        wr = jnp.pad(w_router[l], ((0, 0), (0, 128 - N_EXPERTS)))
        xs, h2, aff = _out_proj(xs, mods, o_na, o_f, o_b, pf, jnp.tile(dn_norm_g[l], GROUP_HEADS).reshape(1, GROUP_W),
                                o_df, o_ft, w_out[l].astype(BF16), norm2_g[l], wr, n_batch, n_tok)

        gate_l, idx_l = _route(aff[:, :n_lat, :N_EXPERTS], cap_lat)
        gate_c, idx_c = _route(aff[:, n_lat:, :N_EXPERTS], cap_ctx)
        idx = jnp.concatenate([idx_l, idx_c + n_lat], axis=2)
        gate = jnp.concatenate([gate_l, gate_c], axis=2)
        cap = cap_lat + cap_ctx
        gate_col = jnp.swapaxes(gate, 0, 1).reshape(N_EXPERTS, n_batch * cap, 1)
        y = _moe_ffn(_moe_gather(h2, idx), w_gate, w_up, w_down, gate_col, l)
        prev = (_moe_combine(y, idx, n_tok), mods)

    return _final_norm(xs, prev[0], prev[1], final_norm_g, n_batch, n_lat)
```

```python
import functools
import math

import numpy as np
import jax
import jax.numpy as jnp
from jax import lax
from jax.experimental import pallas as pl
from jax.experimental.pallas import tpu as pltpu

F32 = jnp.float32
BF16 = jnp.bfloat16

D_MODEL = 1024
DEPTH = 4
GRID_W = 64
GROUP_W = 256
GROUP_HEADS = 4
HEAD_DIM = 64
NA_WIN_H = 8
NA_WIN_W = 16
DN_CONV_K = 5
DN_CHUNK = 64
DIFF_DIM = 32
FT_DIM = 64
N_EXPERTS = 16
EC_FACTOR = 2
D_EXPERT = 2 * D_MODEL
ROPE_BASE = 10000.0
RMS_EPS = 1e-6
NEG = -1e30

ROW_TILE = 256
N_BF = 6 * GROUP_W
N_F32 = 4 * GROUP_W + 128
VMEM_LIMIT = 56 * 1024 * 1024


def _cparams(*sem):
    return pltpu.CompilerParams(dimension_semantics=sem, vmem_limit_bytes=VMEM_LIMIT)


def _split(x):
    hi = x.astype(BF16)
    lo = (x - hi.astype(F32)).astype(BF16)
    return hi, lo


def _dot(a, b):
    return jnp.dot(a, b, preferred_element_type=F32)


def _dot_nt(a, b):
    return lax.dot_general(a, b, (((1,), (1,)), ((), ())), preferred_element_type=F32)


def _dot_tn(a, b):
    return lax.dot_general(a, b, (((0,), (0,)), ((), ())), preferred_element_type=F32)


def _dot3(a, b):
    ah, al = _split(a)
    bh, bl = _split(b)
    return _dot(ah, bh) + (_dot(al, bh) + _dot(ah, bl))


def _dot3_const(ah, al, b):
    bh, bl = _split(b)
    return _dot(ah, bh) + (_dot(al, bh) + _dot(ah, bl))


def _silu(x):
    return x * jax.nn.sigmoid(x)


def _mod_kernel(s_ref, w_ref, b_ref, o_ref):
    s = _silu(s_ref[...])
    o_ref[...] = _dot(s.astype(BF16), w_ref[...].astype(BF16)) + b_ref[...]


def _modulation(cc, w_mod, b_mod):
    depth = w_mod.shape[0]
    nt = 6 * D_MODEL // 1024
    return pl.pallas_call(
        _mod_kernel,
        grid=(depth, nt),
        in_specs=[
            pl.BlockSpec((8, D_MODEL), lambda l, j: (0, 0)),
            pl.BlockSpec((None, D_MODEL, 1024), lambda l, j: (l, 0, j)),
            pl.BlockSpec((None, 1, 1024), lambda l, j: (l, 0, j)),
        ],
        out_specs=pl.BlockSpec((None, 8, 1024), lambda l, j: (l, 0, j)),
        out_shape=jax.ShapeDtypeStruct((depth, 8, 6 * D_MODEL), F32),
        compiler_params=_cparams("parallel", "parallel"),
        name="modulation",
    )(cc, w_mod, b_mod.reshape(depth, 1, 6 * D_MODEL))


def _norm_mod(x, g, shift, scale):
    y = x * lax.rsqrt(jnp.mean(x * x, axis=-1, keepdims=True) + RMS_EPS) * g
    return y * (1.0 + scale) + shift


def _in_kernel(*refs, residual):
    if residual:
        x_ref, moe_ref, mprev_ref, g_ref, m_ref, w_ref, cos_ref, sin_ref, xo_ref, obf_ref, of_ref, ou_ref = refs
        x = x_ref[...] + mprev_ref[5] * moe_ref[...]
        xo_ref[...] = x
    else:
        x_ref, g_ref, m_ref, w_ref, cos_ref, sin_ref, obf_ref, of_ref, ou_ref = refs
        x = x_ref[...]
    h = _norm_mod(x, g_ref[...], m_ref[0], m_ref[1])
    p = _dot(h.astype(BF16), w_ref[...])
    gw = GROUP_W
    lane = lax.broadcasted_iota(jnp.int32, (1, gw), 1)
    even = (lane % 2) == 0
    cos, sin = cos_ref[...], sin_ref[...]

    def rope(t):
        sw = jnp.where(even, pltpu.roll(t, gw - 1, 1), pltpu.roll(t, 1, 1))
        return t * cos + sw * sin

    obf_ref[:, 0:gw] = (p[:, 0:gw] * HEAD_DIM ** -0.5).astype(BF16)
    obf_ref[:, gw:3 * gw] = p[:, gw:3 * gw].astype(BF16)
    obf_ref[:, 3 * gw:4 * gw] = (rope(p[:, 3 * gw:4 * gw]) * DIFF_DIM ** -0.5).astype(BF16)
    obf_ref[:, 4 * gw:5 * gw] = rope(p[:, 4 * gw:5 * gw]).astype(BF16)
    obf_ref[:, 5 * gw:6 * gw] = p[:, 5 * gw:6 * gw].astype(BF16)
    of_ref[:, 0:4 * gw] = p[:, 6 * gw:10 * gw]
    of_ref[:, 4 * gw:4 * gw + 128] = p[:, 11 * gw:11 * gw + 128]
    ou_ref[...] = p[:, 10 * gw:11 * gw]


def _in_proj(x, prev, g, mods, w_perm, cos_t, sin_t, n_batch, n_tok):
    nt = n_tok // ROW_TILE
    ctx_tile = nt - 1
    n_out = w_perm.shape[1]
    row = lambda b, i: (b, i, 0)
    x_spec = pl.BlockSpec((None, ROW_TILE, D_MODEL), row)
    mod_spec = pl.BlockSpec((None, 6, 1, D_MODEL), lambda b, i: (jnp.where(i == ctx_tile, n_batch, b), 0, 0, 0))
    in_specs = [
        pl.BlockSpec((1, D_MODEL), lambda b, i: (0, 0)),
        mod_spec,
        pl.BlockSpec((D_MODEL, n_out), lambda b, i: (0, 0)),
        pl.BlockSpec((ROW_TILE, GROUP_W), lambda b, i: (i, 0)),
        pl.BlockSpec((ROW_TILE, GROUP_W), lambda b, i: (i, 0)),
    ]
    out_specs = [
        pl.BlockSpec((None, ROW_TILE, N_BF), row),
        pl.BlockSpec((None, ROW_TILE, N_F32), row),
        pl.BlockSpec((None, ROW_TILE, GROUP_W), row),
    ]
    out_shape = [
        jax.ShapeDtypeStruct((n_batch, n_tok, N_BF), BF16),
        jax.ShapeDtypeStruct((n_batch, n_tok, N_F32), F32),
        jax.ShapeDtypeStruct((n_batch, n_tok, GROUP_W), F32),
    ]
    args = (g.reshape(1, D_MODEL), mods, w_perm, cos_t, sin_t)
    if prev is None:
        in_specs, args = [x_spec] + in_specs, (x,) + args
    else:
        in_specs, args = [x_spec, x_spec, mod_spec] + in_specs, (x,) + tuple(prev) + args
        out_specs = [x_spec] + out_specs
        out_shape = [jax.ShapeDtypeStruct((n_batch, n_tok, D_MODEL), F32)] + out_shape
    outs = pl.pallas_call(
        functools.partial(_in_kernel, residual=prev is not None),
        grid=(n_batch, nt),
        in_specs=in_specs,
        out_specs=out_specs,
        out_shape=out_shape,
        compiler_params=_cparams("parallel", "parallel"),
        name="in_proj",
    )(*args)
    return tuple(outs) if prev is not None else (x,) + tuple(outs)


def _na_kernel(q_ref, k_ref, v_ref, bias_ref, o_ref, *, n_lat, n_ctx):
    i = pl.program_id(1)
    rows = n_lat // GRID_W
    wh = min(NA_WIN_H, rows)
    n_win = wh * GRID_W
    per = ROW_TILE // GRID_W
    kc = k_ref[n_lat:n_lat + n_ctx, :]
    vc = v_ref[n_lat:n_lat + n_ctx, :]
    lane = lax.broadcasted_iota(jnp.int32, (GRID_W, GROUP_W), 1)
    masks = [lane // HEAD_DIM == h for h in range(GROUP_HEADS)]
    outs = []
    for u in range(per):
        r = i * per + u
        is_lat = r < rows
        base = jnp.where(is_lat, jnp.clip(r - wh // 2, 0, rows - wh), 0)
        slab = jnp.where(is_lat, base - r + NA_WIN_H - 1, NA_WIN_H)
        start = pl.multiple_of(base * GRID_W, GRID_W)
        q = q_ref[u * GRID_W:(u + 1) * GRID_W, :]
        qm = jnp.concatenate([jnp.where(m, q, jnp.zeros_like(q)) for m in masks], axis=0)
        keys = jnp.concatenate([k_ref[pl.ds(start, n_win), :], kc], axis=0)
        vals = jnp.concatenate([v_ref[pl.ds(start, n_win), :], vc], axis=0)
        s = _dot_nt(qm, keys)
        sw = s[:, :n_win] + bias_ref[slab].reshape(GROUP_HEADS * GRID_W, n_win)
        sc = s[:, n_win:]
        m = jnp.maximum(jnp.max(sw, axis=-1, keepdims=True), jnp.max(sc, axis=-1, keepdims=True))
        pw = jnp.exp(sw - m)
        pc = jnp.exp(sc - m)
        l = jnp.sum(pw, axis=-1, keepdims=True) + jnp.sum(pc, axis=-1, keepdims=True)
        res = _dot(jnp.concatenate([pw, pc], axis=1).astype(BF16), vals) / l
        o = jnp.where(masks[0], res[0:GRID_W], 0.0)
        for h in range(1, GROUP_HEADS):
            o = o + jnp.where(masks[h], res[h * GRID_W:(h + 1) * GRID_W], 0.0)
        outs.append(o)
    o_ref[...] = jnp.concatenate(outs, axis=0).astype(BF16)


def _na_attention(pbf, bias_tab, n_batch, n_lat, n_ctx):
    n_tok = n_lat + n_ctx
    return pl.pallas_call(
        functools.partial(_na_kernel, n_lat=n_lat, n_ctx=n_ctx),
        grid=(n_batch, n_tok // ROW_TILE),
        in_specs=[
            pl.BlockSpec((None, ROW_TILE, GROUP_W), lambda b, i: (b, i, 0)),
            pl.BlockSpec((None, n_tok, GROUP_W), lambda b, i: (b, 0, 1)),
            pl.BlockSpec((None, n_tok, GROUP_W), lambda b, i: (b, 0, 2)),
            pl.BlockSpec(bias_tab.shape, lambda b, i: (0, 0, 0, 0)),
        ],
        out_specs=pl.BlockSpec((None, ROW_TILE, GROUP_W), lambda b, i: (b, i, 0)),
        out_shape=jax.ShapeDtypeStruct((n_batch, n_tok, GROUP_W), BF16),
        compiler_params=_cparams("parallel", "arbitrary"),
        name="na_attention",
    )(pbf, pbf, pbf, bias_tab)


def _na_bias_table(rpb):
    wh = NA_WIN_H
    cq = np.arange(GRID_W)
    c0 = np.clip(cq - NA_WIN_W // 2, 0, GRID_W - NA_WIN_W)
    in_win = (cq[None, :] >= c0[:, None]) & (cq[None, :] < c0[:, None] + NA_WIN_W)
    dx = np.clip(cq[None, :] - cq[:, None], 1 - NA_WIN_W, NA_WIN_W - 1) + NA_WIN_W - 1
    dy = np.arange(wh)[:, None] + np.arange(wh)[None, :]
    pick_x = (dx[:, :, None] == np.arange(2 * NA_WIN_W - 1)).astype(np.float32)
    pick_y = (dy[:, :, None] == np.arange(2 * NA_WIN_H - 1)).astype(np.float32)
    hi = lax.Precision.HIGHEST
    cols = jnp.einsum('hyx,qkx->hyqk', rpb.astype(F32), pick_x, precision=hi)
    tab = jnp.einsum('swy,hyqk->shqwk', pick_y, cols, precision=hi)
    tab = jnp.where(in_win[None, None, :, None, :], tab, NEG)
    tab = tab.reshape(wh, GROUP_HEADS, GRID_W, wh * GRID_W)
    return jnp.concatenate([tab, jnp.full((1,) + tab.shape[1:], NEG, F32)], axis=0)


DF_KV = 512


def _df_kernel(lam_ref, q_ref, k_ref, v_ref, g_ref, o_ref, s_ref, *, chunks, lam_init):
    lq1, lk1, lq2, lk2 = lam_ref[0:1, :], lam_ref[1:2, :], lam_ref[2:3, :], lam_ref[3:4, :]
    lam = (jnp.exp(jnp.sum(lq1 * lk1, axis=-1, keepdims=True))
           - jnp.exp(jnp.sum(lq2 * lk2, axis=-1, keepdims=True)) + lam_init)
    q = q_ref[...]
    tq, gw = q.shape
    lane = lax.broadcasted_iota(jnp.int32, (tq, gw), 1)
    row2 = lax.broadcasted_iota(jnp.int32, (2 * tq, 1), 0)
    origin = chunks[0][0]
    zero = jnp.zeros_like(q)

    def head(h, out, slot):
        qm = [jnp.where(lane // DIFF_DIM == 2 * h + mp, q, zero) for mp in range(2)]
        m = [jnp.full((tq, 128), NEG, F32) for _ in range(2)]
        for st, sz in chunks:
            k = k_ref[st:st + sz, :]
            for mp in range(2):
                s = _dot_nt(qm[mp], k)
                s_ref[(slot + mp) * tq:(slot + mp + 1) * tq, st - origin:st - origin + sz] = s
                for t in range(sz // 128):
                    m[mp] = jnp.maximum(m[mp], s[:, t * 128:(t + 1) * 128])
        m = [jnp.max(x, axis=-1, keepdims=True) for x in m]
        ones_lane = ((h + 1) % GROUP_HEADS) * HEAD_DIM
        acc = [jnp.zeros((tq, gw), F32) for _ in range(2)]
        for st, sz in chunks:
            v = v_ref[st:st + sz, :]
            lane_v = lax.broadcasted_iota(jnp.int32, v.shape, 1)
            v = jnp.where(lane_v == ones_lane, jnp.ones_like(v), v)
            for mp in range(2):
                e = jnp.exp((s_ref[(slot + mp) * tq:(slot + mp + 1) * tq, st - origin:st - origin + sz] - m[mp]).astype(BF16))
                acc[mp] = acc[mp] + _dot(e, v)
        w = []
        for mp in range(2):
            l = jnp.sum(jnp.where(lane == ones_lane, acc[mp], 0.0), axis=-1, keepdims=True)
            w.append(acc[mp] * ((1.0 if mp == 0 else lam) / l))
        return out + jnp.where(lane // HEAD_DIM == h, w[0] - w[1], 0.0)

    def head_pair(i, out):
        return head(2 * i + 1, head(2 * i, out, 0), 2)

    o = lax.fori_loop(0, GROUP_HEADS // 2, head_pair, jnp.zeros((tq, gw), F32))
    sq = o * o
    scale = jnp.zeros((tq, gw), F32)
    for h in range(GROUP_HEADS):
        in_head = lane // HEAD_DIM == h
        ms = jnp.sum(jnp.where(in_head, sq, 0.0), axis=-1, keepdims=True) * (1.0 / HEAD_DIM)
        scale = jnp.where(in_head, lax.rsqrt(ms + RMS_EPS), scale)
    o_ref[...] = (o * scale * g_ref[...] * (1.0 - lam_init)).astype(BF16)


def _df_call(pbf, df_lambda, g_tiled, lam_init, n_batch, n_tok, first_tile, n_tiles, chunks, name):
    n_keys = sum(sz for _, sz in chunks)
    return pl.pallas_call(
        functools.partial(_df_kernel, chunks=chunks, lam_init=lam_init),
        grid=(n_batch, n_tiles),
        in_specs=[
            pl.BlockSpec((4, DIFF_DIM), lambda b, i: (0, 0)),
            pl.BlockSpec((None, ROW_TILE, GROUP_W), lambda b, i: (b, first_tile + i, 3)),
            pl.BlockSpec((None, n_tok, GROUP_W), lambda b, i: (b, 0, 4)),
            pl.BlockSpec((None, n_tok, GROUP_W), lambda b, i: (b, 0, 5)),
            pl.BlockSpec((1, GROUP_W), lambda b, i: (0, 0)),
        ],
        out_specs=pl.BlockSpec((None, ROW_TILE, GROUP_W), lambda b, i: (b, i, 0)),
        out_shape=jax.ShapeDtypeStruct((n_batch, n_tiles * ROW_TILE, GROUP_W), BF16),
        scratch_shapes=[pltpu.VMEM((4 * ROW_TILE, n_keys), F32)],
        compiler_params=_cparams("parallel", "arbitrary"),
        name=name,
    )(df_lambda, pbf, pbf, pbf, g_tiled)


def _df_attention(pbf, df_lambda, g_tiled, lam_init, n_batch, n_lat, n_ctx):
    n_tok = n_lat + n_ctx
    lat_chunks = tuple((st, min(DF_KV, n_tok - st)) for st in range(0, n_tok, DF_KV))
    o_lat = _df_call(pbf, df_lambda, g_tiled, lam_init, n_batch, n_tok, 0, n_lat // ROW_TILE, lat_chunks, "df_attention")
    o_ctx = _df_call(pbf, df_lambda, g_tiled, lam_init, n_batch, n_tok, n_lat // ROW_TILE, n_ctx // ROW_TILE,
                     ((n_lat, n_ctx),), "df_attention_ctx")
    return jnp.concatenate([o_lat, o_ctx], axis=1)


CONV_PAD = 8


def _dn_prep_kernel(u_ref, w_ref, o_ref, pad_ref, *, n_lat, n_ctx):
    j = pl.program_id(1)
    zeros = jnp.zeros((CONV_PAD, GROUP_W), F32)
    lat0 = CONV_PAD
    ctx0 = 2 * CONV_PAD + n_lat
    pad_ref[0:CONV_PAD, :] = zeros
    pad_ref[lat0 + n_lat:ctx0, :] = zeros
    pad_ref[ctx0 + n_ctx:ctx0 + n_ctx + CONV_PAD, :] = zeros
    pad_ref[lat0:lat0 + n_lat, :] = u_ref[0:n_lat, :]
    pad_ref[ctx0:ctx0 + n_ctx, :] = u_ref[n_lat:n_lat + n_ctx, :]
    qscale = jnp.where(j == 0, HEAD_DIM ** -0.5, 1.0)
    half = DN_CONV_K // 2
    for (src0, dst0, n) in ((lat0, 0, n_lat), (ctx0, n_lat, n_ctx)):
        for t0 in range(0, n, ROW_TILE):
            acc = jnp.zeros((ROW_TILE, GROUP_W), F32)
            for tap in range(DN_CONV_K):
                a = src0 + t0 + tap - half
                acc = acc + pad_ref[a:a + ROW_TILE, :] * w_ref[tap:tap + 1, :]
            y = _silu(acc)
            parts = []
            for h in range(GROUP_HEADS):
                yh = y[:, h * HEAD_DIM:(h + 1) * HEAD_DIM]
                nrm = lax.rsqrt(jnp.sum(yh * yh, axis=-1, keepdims=True) + 1e-6) * qscale
                parts.append(yh * jnp.where(j == 2, 1.0, nrm))
            o_ref[dst0 + t0:dst0 + t0 + ROW_TILE, :] = jnp.concatenate(parts, axis=-1)


def _dn_prep(pf, conv_w, n_batch, n_lat, n_ctx):
    n_tok = n_lat + n_ctx
    return pl.pallas_call(
        functools.partial(_dn_prep_kernel, n_lat=n_lat, n_ctx=n_ctx),
        grid=(n_batch, 3),
        in_specs=[
            pl.BlockSpec((None, n_tok, GROUP_W), lambda b, j: (b, 0, j)),
            pl.BlockSpec((DN_CONV_K, GROUP_W), lambda b, j: (0, j)),
        ],
        out_specs=pl.BlockSpec((None, n_tok, GROUP_W), lambda b, j: (b, 0, j)),
        out_shape=jax.ShapeDtypeStruct((n_batch, n_tok, 3 * GROUP_W), F32),
        scratch_shapes=[pltpu.VMEM((n_tok + 3 * CONV_PAD, GROUP_W), F32)],
        compiler_params=_cparams("parallel", "arbitrary"),
        name="dn_prep",
    )(pf, conv_w)


DN_BLOCK = 4 * DN_CHUNK


def _split3(x):
    x1 = x.astype(BF16)
    r = x - x1.astype(F32)
    x2 = r.astype(BF16)
    return x1, x2, (r - x2.astype(F32)).astype(BF16)


def _block_diag(x, masks):
    zero = jnp.zeros_like(x)
    return jnp.concatenate([jnp.where(m, x, zero) for m in masks], axis=0)


def _mm3_heads(lhs, rhs, masks):
    lh, ll = _split(lhs)
    rh, rl = _split(rhs)
    bh, bl = _block_diag(rh, masks), _block_diag(rl, masks)
    return _dot(jnp.concatenate([lh, ll, lh], axis=1), jnp.concatenate([bh, bh, bl], axis=0))


def _dn_kernel(qf_ref, kf_ref, vf_ref, abf_ref, qb_ref, kb_ref, vb_ref, abb_ref, av_ref, dt_ref, exp_ref,
               of_ref, ob_ref, s_ref):
    @pl.when(pl.program_id(1) == 0)
    def _():
        s_ref[...] = jnp.zeros_like(s_ref)

    c, nb, gw = DN_CHUNK, DN_BLOCK, GROUP_W
    n_sub = nb // c
    lane4 = lax.broadcasted_iota(jnp.int32, (c, gw), 1)
    masks = [(lane4 // HEAD_DIM) == h for h in range(GROUP_HEADS)]
    ri = lax.broadcasted_iota(jnp.int32, (c, gw), 0)
    cj = lane4 % HEAD_DIM
    eye = (ri == cj).astype(F32)
    bi = lax.broadcasted_iota(jnp.int32, (nb, nb), 0)
    bj = lax.broadcasted_iota(jnp.int32, (nb, nb), 1)
    same_chunk = (bi // c) == (bj // c)
    lane1 = lax.broadcasted_iota(jnp.int32, (nb, 128), 1)
    dirs = ((qf_ref, kf_ref, vf_ref, abf_ref), (qb_ref, kb_ref, vb_ref, abb_ref))

    chains = []
    for d, (q_ref, k_ref, v_ref, ab_ref) in enumerate(dirs):
        q_all, k_all, v_all = q_ref[...], k_ref[...], v_ref[...]
        for j in range(n_sub):
            rows = slice(j * c, (j + 1) * c)
            q, k = q_all[rows], k_all[rows]
            kq = _dot_nt(jnp.concatenate([k, q], axis=0).astype(BF16), _block_diag(k.astype(BF16), masks))
            chains.append(dict(d=d, rows=rows, q=q, k=k, v=v_all[rows], kq=kq))
    gates = []
    for d, (q_ref, k_ref, v_ref, ab_ref) in enumerate(dirs):
        ab = ab_ref[...]
        x = ab + dt_ref[...]
        softplus = jnp.maximum(x, 0.0) + jnp.log1p(jnp.exp(-jnp.abs(x)))
        g = -jnp.exp(av_ref[...]) * softplus
        tri = (same_chunk & ((bi >= bj) if d == 0 else (bi <= bj))).astype(BF16)
        g1, g2, g3 = _split3(g)
        gcs = _dot(tri, g1) + (_dot(tri, g2) + _dot(tri, g3))
        gates.append(jnp.where(lane1 < 2 * GROUP_HEADS, gcs, jax.nn.sigmoid(ab)))
    spread = []
    for d in range(2):
        x1, x2, x3 = _split3(gates[d])
        e = exp_ref[d]
        spread.append(_dot(x1, e) + (_dot(x2, e) + _dot(x3, e)))
    for ch in chains:
        d, q, k, v, kq = ch["d"], ch["q"], ch["k"], ch["v"], ch["kq"]
        incl = (ri >= cj) if d == 0 else (ri <= cj)
        strict = (ri > cj) if d == 0 else (ri < cj)
        last = c - 1 if d == 0 else 0
        gc, beta = spread[d][ch["rows"], :gw], spread[d][ch["rows"], gw:]
        gc_row = jnp.sum(gc * eye, axis=0, keepdims=True)
        decay = jnp.exp(jnp.where(incl, gc - gc_row, NEG))
        eg = jnp.exp(gc)
        g_last = gc[last:last + 1, :]
        ch.update(
            lm=jnp.where(strict, kq[:c] * beta * decay, 0.0),
            qk=jnp.where(incl, kq[c:] * decay, 0.0).astype(BF16),
            rhs=jnp.concatenate([_block_diag((v * beta).astype(BF16), masks),
                                 _block_diag((k * beta * eg).astype(BF16), masks)], axis=1),
            q_dec=q * eg,
            k_dec=(k * jnp.exp(g_last - gc)).astype(BF16),
            g_end=jnp.exp(g_last))

    def neumann_level(first):
        for ch in chains:
            if first:
                ch["p"] = eye - ch["lm"]
                ch["sq"] = _mm3_heads(ch["lm"], ch["lm"], masks)
            else:
                r = _mm3_heads(jnp.concatenate([ch["sq"], ch["p"]], axis=0), ch["sq"], masks)
                ch["sq"], ch["p"] = r[:c], ch["p"] + r[c:]

    def head_blocks(full):
        out = jnp.where(masks[0], full[0:c], 0.0)
        for h in range(1, GROUP_HEADS):
            out = out + jnp.where(masks[h], full[h * c:(h + 1) * c], 0.0)
        return out

    n_levels = int(math.log2(c)) - 1
    for level in range(n_levels):
        neumann_level(level == 0)
    t_inv = [ch["p"] + _mm3_heads(ch["p"], ch["sq"], masks) for ch in chains]
    uw = [_dot(t.astype(BF16), ch["rhs"]).astype(BF16) for t, ch in zip(t_inv, chains)]
    kd_uw = [_dot_tn(ch["k_dec"], x) for x, ch in zip(uw, chains)]
    qk_uw = [_dot(ch["qk"], jnp.concatenate([_block_diag(x[:, :gw], masks), _block_diag(x[:, gw:], masks)], axis=1))
             for x, ch in zip(uw, chains)]
    solved = [dict(ms=jnp.concatenate([head_blocks(kd[:, gw:]), ch["q_dec"] - qq[:, gw:]], axis=0).astype(BF16),
                   b=head_blocks(kd[:, :gw]), d=qq[:, :gw], g_end=ch["g_end"])
              for kd, qq, ch in zip(kd_uw, qk_uw, chains)]

    states = [s_ref[0], s_ref[1]]
    outs = [[None] * n_sub, [None] * n_sub]
    for step in range(n_sub):
        for d in range(2):
            j = step if d == 0 else n_sub - 1 - step
            ch = solved[d * n_sub + j]
            s = states[d]
            res = _dot(ch["ms"], _block_diag(s.astype(BF16), masks))
            outs[d][j] = res[c:] + ch["d"]
            states[d] = s * ch["g_end"] - res[:c] + ch["b"]
    of_ref[...] = jnp.concatenate(outs[0], axis=0)
    ob_ref[...] = jnp.concatenate(outs[1], axis=0)
    s_ref[...] = jnp.stack(states, axis=0)


def _dn_scan(qkv, pf, a_vec, dt_vec, n_batch, n_lat, n_ctx):
    n_tok = n_lat + n_ctx
    assert n_ctx == DN_BLOCK and n_lat % DN_BLOCK == 0
    nblk = n_tok // DN_BLOCK
    ab_blk = 4 * GROUP_W // 128

    def fwd(n):
        return jnp.where(n == 0, nblk - 1, n - 1)

    def bwd(n):
        return jnp.where(n == 0, nblk - 1, nblk - 1 - n)

    def spec(order, blk, width):
        return pl.BlockSpec((None, DN_BLOCK, width), lambda b, n: (b, order(n), blk))

    spread = np.zeros((2, 128, 2 * GROUP_W), np.float32)
    for d in range(2):
        for h in range(GROUP_HEADS):
            spread[d, GROUP_HEADS * d + h, h * HEAD_DIM:(h + 1) * HEAD_DIM] = 1.0
            spread[d, 2 * GROUP_HEADS + GROUP_HEADS * d + h, GROUP_W + h * HEAD_DIM:GROUP_W + (h + 1) * HEAD_DIM] = 1.0

    lane = pl.BlockSpec((1, 128), lambda b, n: (0, 0))
    out = jax.ShapeDtypeStruct((n_batch, n_tok, GROUP_W), F32)
    return pl.pallas_call(
        _dn_kernel,
        grid=(n_batch, nblk),
        in_specs=[spec(fwd, 0, GROUP_W), spec(fwd, 1, GROUP_W), spec(fwd, 2, GROUP_W), spec(fwd, ab_blk, 128),
                  spec(bwd, 0, GROUP_W), spec(bwd, 1, GROUP_W), spec(bwd, 2, GROUP_W), spec(bwd, ab_blk, 128),
                  lane, lane, pl.BlockSpec((2, 128, 2 * GROUP_W), lambda b, n: (0, 0, 0))],
        out_specs=[spec(fwd, 0, GROUP_W), spec(bwd, 0, GROUP_W)],
        out_shape=[out, out],
        scratch_shapes=[pltpu.VMEM((2, HEAD_DIM, GROUP_W), F32)],
        compiler_params=_cparams("parallel", "arbitrary"),
        name="dn_scan",
    )(qkv, qkv, qkv, pf, qkv, qkv, qkv, pf, a_vec, dt_vec, jnp.asarray(spread, BF16))


FT_N2 = 64


def _ft1_kernel(x_ref, fh_ref, fl_ref, o_ref):
    o_ref[...] = _dot3_const(fh_ref[...], fl_ref[...], x_ref[...])


def _ft_stage1(u_rows, f_hi, f_lo, n_batch, n1, tn=2048):
    width = u_rows.shape[2]
    return pl.pallas_call(
        _ft1_kernel,
        grid=(n_batch, width // tn),
        in_specs=[
            pl.BlockSpec((None, n1, tn), lambda b, j: (b, 0, j)),
            pl.BlockSpec((2 * n1, n1), lambda b, j: (0, 0)),
            pl.BlockSpec((2 * n1, n1), lambda b, j: (0, 0)),
        ],
        out_specs=pl.BlockSpec((None, 2 * n1, tn), lambda b, j: (b, 0, j)),
        out_shape=jax.ShapeDtypeStruct((n_batch, 2 * n1, width), F32),
        compiler_params=_cparams("parallel", "parallel"),
        name="ft_stage1",
    )(u_rows, f_hi, f_lo)


def _ft3_kernel(y_ref, gh_ref, gl_ref, ch_ref, cl_ref, sh_ref, sl_ref, w_ref, o_ref, *, pb, n2, norm):
    vr, vi = [], []
    for p in range(pb):
        rhs = jnp.concatenate([y_ref[0, p], y_ref[1, p]], axis=0)
        v = _dot3_const(gh_ref[p], gl_ref[p], rhs)
        vr.append(v[:n2])
        vi.append(v[n2:])
    vr = jnp.concatenate(vr, axis=0)
    vi = jnp.concatenate(vi, axis=0)
    vrh, vrl = _split(vr)
    vih, vil = _split(vi)
    y = (_dot(vrh, ch_ref[...]) + (_dot(vrl, ch_ref[...]) + _dot(vrh, cl_ref[...]))
         + _dot(vih, sh_ref[...]) + (_dot(vil, sh_ref[...]) + _dot(vih, sl_ref[...]))) * norm
    o = _dot(y.astype(BF16), w_ref[...])
    for p in range(pb):
        o_ref[:, p, :] = o[p * n2:(p + 1) * n2, :]


def _ft_stage2(y1, g_hi, g_lo, chan, ft_w, n_batch, n1, n2, pb):
    ch, cl, sh, sl = chan
    full = lambda b, i: (0, 0)
    return pl.pallas_call(
        functools.partial(_ft3_kernel, pb=pb, n2=n2, norm=1.0 / math.sqrt(n1 * n2 * FT_DIM)),
        grid=(n_batch, n1 // pb),
        in_specs=[
            pl.BlockSpec((None, 2, pb, n2, GROUP_W), lambda b, i: (b, 0, i, 0, 0)),
            pl.BlockSpec((pb, 2 * n2, 2 * n2), lambda b, i: (i, 0, 0)),
            pl.BlockSpec((pb, 2 * n2, 2 * n2), lambda b, i: (i, 0, 0)),
            pl.BlockSpec((GROUP_W, GROUP_W), full), pl.BlockSpec((GROUP_W, GROUP_W), full),
            pl.BlockSpec((GROUP_W, GROUP_W), full), pl.BlockSpec((GROUP_W, GROUP_W), full),
            pl.BlockSpec((GROUP_W, GROUP_W), full),
        ],
        out_specs=pl.BlockSpec((None, n2, pb, GROUP_W), lambda b, i: (b, 0, i, 0)),
        out_shape=jax.ShapeDtypeStruct((n_batch, n2, n1, GROUP_W), F32),
        compiler_params=_cparams("parallel", "parallel"),
        name="ft_stage2",
    )(y1, g_hi, g_lo, ch, cl, sh, sl, ft_w)


def _np_split(a):
    hi = jnp.asarray(a, F32).astype(BF16)
    lo = (jnp.asarray(a, F32) - hi.astype(F32)).astype(BF16)
    return hi, lo


@functools.lru_cache(maxsize=None)
def _ft_tables(n1, n2):
    n = n1 * n2
    p = np.arange(n1)
    ang1 = 2.0 * np.pi * ((p[:, None] * p[None, :]) % n1) / n1
    f1 = np.concatenate([np.cos(ang1), -np.sin(ang1)], axis=0)
    q = np.arange(n2)
    b = np.arange(n2)
    phase = (b[None, None, :] * (p[:, None, None] + n1 * q[None, :, None])) % n
    psi = 2.0 * np.pi * phase / n
    gc, gs = np.cos(psi), np.sin(psi)
    g = np.concatenate([np.concatenate([gc, gs], axis=2), np.concatenate([-gs, gc], axis=2)], axis=1)
    c = np.arange(FT_DIM)
    angc = 2.0 * np.pi * ((c[:, None] * c[None, :]) % FT_DIM) / FT_DIM
    eye = np.eye(GROUP_W // FT_DIM)
    cc, sc = np.kron(eye, np.cos(angc)), np.kron(eye, np.sin(angc))
    return f1.astype(np.float32), g.astype(np.float32), cc.astype(np.float32), sc.astype(np.float32)


def _out_kernel(x_ref, m_ref, na_ref, of_ref, ob_ref, gate_ref, dng_ref, df_ref, ft_ref, w_ref, g2_ref, wr_ref,
                xo_ref, h_ref, aff_ref):
    o = of_ref[...] + ob_ref[...]
    parts = []
    for h in range(GROUP_HEADS):
        oh = o[:, h * HEAD_DIM:(h + 1) * HEAD_DIM]
        parts.append(oh * lax.rsqrt(jnp.mean(oh * oh, axis=-1, keepdims=True) + RMS_EPS))
    y_dn = jnp.concatenate(parts, axis=-1) * dng_ref[...] * _silu(gate_ref[...])
    mix = jnp.concatenate([na_ref[...], y_dn.astype(BF16), df_ref[...], ft_ref[...].astype(BF16)], axis=-1)
    x = x_ref[...] + m_ref[2] * _dot(mix, w_ref[...])
    xo_ref[...] = x
    h2 = _norm_mod(x, g2_ref[...], m_ref[3], m_ref[4])
    bits = pltpu.bitcast(h2.astype(BF16).astype(F32), jnp.uint32)
    half = D_MODEL // 2
    h_ref[...] = bits[:, half:] | (bits[:, :half] >> 16)
    logits = _dot3(h2, wr_ref[...])
    lane = lax.broadcasted_iota(jnp.int32, logits.shape, 1)
    logits = jnp.where(lane < N_EXPERTS, logits, NEG)
    e = jnp.exp(logits - jnp.max(logits, axis=-1, keepdims=True))
    aff_ref[...] = e / jnp.sum(e, axis=-1, keepdims=True)


def _out_proj(x, mods, o_na, o_f, o_b, pf, dn_g, o_df, o_ft, w_out, g2, w_router, n_batch, n_tok):
    nt = n_tok // ROW_TILE
    ctx_tile = nt - 1
    row = lambda b, i: (b, i, 0)
    blk = lambda: pl.BlockSpec((None, ROW_TILE, GROUP_W), row)
    vec = lambda n: pl.BlockSpec((1, n), lambda b, i: (0, 0))
    return pl.pallas_call(
        _out_kernel,
        grid=(n_batch, nt),
        in_specs=[
            pl.BlockSpec((None, ROW_TILE, D_MODEL), row),
            pl.BlockSpec((None, 6, 1, D_MODEL), lambda b, i: (jnp.where(i == ctx_tile, n_batch, b), 0, 0, 0)),
            blk(), blk(), blk(),
            pl.BlockSpec((None, ROW_TILE, GROUP_W), lambda b, i: (b, i, 3)),
            vec(GROUP_W), blk(), blk(),
            pl.BlockSpec((D_MODEL, D_MODEL), lambda b, i: (0, 0)),
            vec(D_MODEL),
            pl.BlockSpec((D_MODEL, 128), lambda b, i: (0, 0)),
        ],
        out_specs=[
            pl.BlockSpec((None, ROW_TILE, D_MODEL), row),
            pl.BlockSpec((None, ROW_TILE, D_MODEL // 2), row),
            pl.BlockSpec((None, ROW_TILE, 128), row),
        ],
        out_shape=[
            jax.ShapeDtypeStruct((n_batch, n_tok, D_MODEL), F32),
            jax.ShapeDtypeStruct((n_batch, n_tok, D_MODEL // 2), jnp.uint32),
            jax.ShapeDtypeStruct((n_batch, n_tok, 128), F32),
        ],
        compiler_params=_cparams("parallel", "parallel"),
        name="out_proj_router",
    )(x, mods, o_na, o_f, o_b, pf, dn_g, o_df, o_ft, w_out, g2.reshape(1, D_MODEL), w_router)


MOE_TF = 256
GATHER_UNROLL = 8


def _gather_kernel(idx_ref, h_ref, o_ref, *, cap, n_e):
    base = (pl.program_id(0) * n_e + pl.program_id(1)) * cap

    def body(j, carry):
        for u in range(GATHER_UNROLL):
            o_ref[j, u:u + 1, :] = h_ref[pl.ds(idx_ref[base + j * GATHER_UNROLL + u], 1), :]
        return carry

    lax.fori_loop(0, cap // GATHER_UNROLL, body, 0)


def _moe_gather(hp, idx):
    n_batch, n_tok, width = hp.shape
    _, n_e, cap = idx.shape
    assert cap % GATHER_UNROLL == 0 and GATHER_UNROLL == 8
    groups = cap // GATHER_UNROLL
    out = pl.pallas_call(
        functools.partial(_gather_kernel, cap=cap, n_e=n_e),
        grid_spec=pltpu.PrefetchScalarGridSpec(
            num_scalar_prefetch=1,
            grid=(n_batch, n_e),
            in_specs=[pl.BlockSpec((None, n_tok, width), lambda b, e, idx_ref: (b, 0, 0))],
            out_specs=pl.BlockSpec((None, groups, GATHER_UNROLL, width), lambda b, e, idx_ref: (e, b, 0, 0)),
        ),
        out_shape=jax.ShapeDtypeStruct((n_e, n_batch * groups, GATHER_UNROLL, width), jnp.uint32),
        compiler_params=_cparams("parallel", "arbitrary"),
        name="moe_gather",
    )(idx.reshape(-1), hp)
    return out.reshape(n_e, n_batch * cap, width)


def _combine_kernel(idx_ref, y_ref, o_ref, *, cap, n_e):
    e = pl.program_id(1)

    @pl.when(e == 0)
    def _():
        o_ref[...] = jnp.zeros_like(o_ref)

    base = (pl.program_id(0) * n_e + e) * cap

    def body(j, carry):
        first = j * GATHER_UNROLL
        toks = [idx_ref[base + first + u] for u in range(GATHER_UNROLL)]
        rows = [o_ref[pl.ds(t, 1), :] + y_ref[j, u:u + 1, :] for u, t in enumerate(toks)]
        for t, row in zip(toks, rows):
            o_ref[pl.ds(t, 1), :] = row
        return carry

    lax.fori_loop(0, cap // GATHER_UNROLL, body, 0)


def _moe_combine(y, idx, n_tok):
    n_batch, n_e, cap = idx.shape
    d = y.shape[2]
    groups = cap // GATHER_UNROLL
    return pl.pallas_call(
        functools.partial(_combine_kernel, cap=cap, n_e=n_e),
        grid_spec=pltpu.PrefetchScalarGridSpec(
            num_scalar_prefetch=1,
            grid=(n_batch, n_e),
            in_specs=[pl.BlockSpec((None, groups, GATHER_UNROLL, d), lambda b, e, idx_ref: (e, b, 0, 0))],
            out_specs=pl.BlockSpec((None, n_tok, d), lambda b, e, idx_ref: (b, 0, 0)),
        ),
        out_shape=jax.ShapeDtypeStruct((n_batch, n_tok, d), F32),
        compiler_params=_cparams("parallel", "arbitrary"),
        name="moe_combine",
    )(idx.reshape(-1), y.reshape(n_e, n_batch * groups, GATHER_UNROLL, d))


def _moe_kernel(x_ref, wg_ref, wu_ref, wd_ref, gate_ref, o_ref, xb_ref):
    f = pl.program_id(1)

    @pl.when(f == 0)
    def _():
        packed = x_ref[...]
        xb_ref[...] = jnp.concatenate([pltpu.bitcast(packed << 16, F32),
                                       pltpu.bitcast(packed & jnp.uint32(0xFFFF0000), F32)], axis=1).astype(BF16)
        o_ref[...] = jnp.zeros_like(o_ref)

    x = xb_ref[...]
    a = _dot(x, wg_ref[...].astype(BF16))
    u = _dot(x, wu_ref[...].astype(BF16))
    hid = (_silu(a) * u).astype(BF16)
    o_ref[...] += _dot(hid, wd_ref[...].astype(BF16))

    @pl.when(f == pl.num_programs(1) - 1)
    def _():
        o_ref[...] = o_ref[...] * gate_ref[...]


def _moe_ffn(xg, w_gate, w_up, w_down, gate, layer):
    n_e, m, _ = xg.shape
    return pl.pallas_call(
        _moe_kernel,
        grid=(n_e, D_EXPERT // MOE_TF),
        in_specs=[
            pl.BlockSpec((None, m, D_MODEL // 2), lambda e, f: (e, 0, 0)),
            pl.BlockSpec((None, None, D_MODEL, MOE_TF), lambda e, f: (layer, e, 0, f)),
            pl.BlockSpec((None, None, D_MODEL, MOE_TF), lambda e, f: (layer, e, 0, f)),
            pl.BlockSpec((None, None, MOE_TF, D_MODEL), lambda e, f: (layer, e, f, 0)),
            pl.BlockSpec((None, m, 1), lambda e, f: (e, 0, 0)),
        ],
        out_specs=pl.BlockSpec((None, m, D_MODEL), lambda e, f: (e, 0, 0)),
        out_shape=jax.ShapeDtypeStruct((n_e, m, D_MODEL), F32),
        scratch_shapes=[pltpu.VMEM((m, D_MODEL), BF16)],
        compiler_params=_cparams("parallel", "arbitrary"),
        name="moe_ffn",
    )(xg, w_gate, w_up, w_down, gate)


def _final_kernel(x_ref, moe_ref, m_ref, g_ref, o_ref):
    x = x_ref[...] + m_ref[5] * moe_ref[...]
    o_ref[...] = x * lax.rsqrt(jnp.mean(x * x, axis=-1, keepdims=True) + RMS_EPS) * g_ref[...]


def _final_norm(x, moe, mods, g, n_batch, n_lat):
    row = lambda b, i: (b, i, 0)
    blk = pl.BlockSpec((None, ROW_TILE, D_MODEL), row)
    return pl.pallas_call(
        _final_kernel,
        grid=(n_batch, n_lat // ROW_TILE),
        in_specs=[blk, blk, pl.BlockSpec((None, 6, 1, D_MODEL), lambda b, i: (b, 0, 0, 0)),
                  pl.BlockSpec((1, D_MODEL), lambda b, i: (0, 0))],
        out_specs=blk,
        out_shape=jax.ShapeDtypeStruct((n_batch, n_lat, D_MODEL), F32),
        compiler_params=_cparams("parallel", "parallel"),
        name="final_norm",
    )(x, moe, mods, g.reshape(1, D_MODEL))


def _reorder_in_columns(w):
    gw = GROUP_W
    o = 6 * gw + 4 * GROUP_HEADS
    pad = jnp.zeros((w.shape[0], 128 - 4 * GROUP_HEADS), w.dtype)
    parts = [w[:, 0:3 * gw], w[:, o + gw:o + 4 * gw], w[:, 3 * gw:6 * gw], w[:, o:o + gw],
             w[:, o + 4 * gw:o + 5 * gw], w[:, 6 * gw:o], pad]
    return jnp.concatenate(parts, axis=1).astype(BF16)


@functools.lru_cache(maxsize=None)
def _rope_tables(n_lat, n_ctx):
    t = np.arange(n_lat)
    pos = np.stack([t // GRID_W, t % GRID_W], axis=-1).astype(np.float32)
    n_freq = DIFF_DIM // 4
    inv = (ROPE_BASE ** (-np.arange(n_freq, dtype=np.float32) / n_freq)).astype(np.float32)
    ang = (pos[:, :, None] * inv).reshape(n_lat, 2 * n_freq)
    lane = np.arange(GROUP_W)
    idx = (lane % DIFF_DIM) // 2
    sign = np.where(lane % 2 == 0, -1.0, 1.0)
    cos = np.concatenate([np.cos(ang)[:, idx], np.ones((n_ctx, GROUP_W))], axis=0)
    sin = np.concatenate([np.sin(ang)[:, idx] * sign, np.zeros((n_ctx, GROUP_W))], axis=0)
    return cos.astype(np.float32), sin.astype(np.float32)


def _lane_vec(v, n=128):
    v = v.reshape(-1).astype(F32)
    return jnp.zeros((1, n), F32).at[0, :v.shape[0]].set(v)


def _route(aff, cap):
    gate, idx = lax.top_k(jnp.swapaxes(aff, 1, 2), cap)
    return gate, idx


def kernel(x, c, ctx, c_ctx, w_mod, b_mod, norm1_g, w_in, na_rpb, dn_conv_w, dn_a_log, dn_dt_bias, dn_norm_g, df_lambda, df_norm_g, ft_w, w_out, norm2_g, w_router, w_gate, w_up, w_down, final_norm_g):
    n_batch, n_lat, _ = x.shape
    n_ctx = ctx.shape[1]
    n_tok = n_lat + n_ctx
    depth = w_mod.shape[0]
    assert n_batch + 1 <= 8 and n_ctx == ROW_TILE and n_lat % ROW_TILE == 0

    xs = jnp.concatenate([x, ctx], axis=1)
    cc = jnp.zeros((8, D_MODEL), F32).at[:n_batch].set(c).at[n_batch].set(c_ctx)
    m_all = _modulation(cc, w_mod, b_mod)

    cos_t, sin_t = (jnp.asarray(a) for a in _rope_tables(n_lat, n_ctx))
    n1 = n_lat // FT_N2
    f1, g_lat, cc_m, sc_m = _ft_tables(n1, FT_N2)
    _, g_ctx, _, _ = _ft_tables(1, n_ctx)
    f1h, f1l = _np_split(f1)
    glh, gll = _np_split(g_lat)
    gch, gcl = _np_split(g_ctx)
    chan = _np_split(cc_m) + _np_split(sc_m)
    cap_lat = EC_FACTOR * n_lat // N_EXPERTS
    cap_ctx = EC_FACTOR * n_ctx // N_EXPERTS

    prev = None
    for l in range(depth):
        lam_init = 0.8 - 0.6 * math.exp(-0.3 * l)
        mods = m_all[l, :n_batch + 1].reshape(n_batch + 1, 6, 1, D_MODEL)
        xs, pbf, pf, pu = _in_proj(xs, prev, norm1_g[l], mods, _reorder_in_columns(w_in[l]), cos_t, sin_t, n_batch, n_tok)

        o_na = _na_attention(pbf, _na_bias_table(na_rpb[l]), n_batch, n_lat, n_ctx)

        qkv = _dn_prep(pf, dn_conv_w[l], n_batch, n_lat, n_ctx)
        o_f, o_b = _dn_scan(qkv, pf, _lane_vec(dn_a_log[l]), _lane_vec(dn_dt_bias[l]), n_batch, n_lat, n_ctx)

        o_df = _df_attention(pbf, df_lambda[l], jnp.tile(df_norm_g[l], GROUP_HEADS).reshape(1, GROUP_W),
                             lam_init, n_batch, n_lat, n_ctx)

        ftw = ft_w[l].astype(BF16)
        y1 = _ft_stage1(pu.reshape(n_batch, n_tok // FT_N2, FT_N2 * GROUP_W), f1h, f1l, n_batch, n1)
        o_lat = _ft_stage2(y1.reshape(n_batch, 2, n1, FT_N2, GROUP_W), glh, gll, chan, ftw, n_batch, n1, FT_N2, 8)
        o_lat = o_lat.reshape(n_batch, n_lat, GROUP_W)
        u_ctx = pu[:, n_lat:]
        y1c = jnp.stack([u_ctx, jnp.zeros_like(u_ctx)], axis=1)[:, :, None]
        o_ctx = _ft_stage2(y1c, gch, gcl, chan, ftw, n_batch, 1, n_ctx, 1).reshape(n_batch, n_ctx, GROUP_W)
        o_ft = jnp.concatenate([o_lat, o_ctx], axis=1)

        wr = jnp.pad(w_router[l], ((0, 0), (0, 128 - N_EXPERTS)))
        xs, h2, aff = _out_proj(xs, mods, o_na, o_f, o_b, pf, jnp.tile(dn_norm_g[l], GROUP_HEADS).reshape(1, GROUP_W),
                                o_df, o_ft, w_out[l].astype(BF16), norm2_g[l], wr, n_batch, n_tok)

        gate_l, idx_l = _route(aff[:, :n_lat, :N_EXPERTS], cap_lat)
        gate_c, idx_c = _route(aff[:, n_lat:, :N_EXPERTS], cap_ctx)
        idx = jnp.concatenate([idx_l, idx_c + n_lat], axis=2)
        gate = jnp.concatenate([gate_l, gate_c], axis=2)
        cap = cap_lat + cap_ctx
        gate_col = jnp.swapaxes(gate, 0, 1).reshape(N_EXPERTS, n_batch * cap, 1)
        y = _moe_ffn(_moe_gather(h2, idx), w_gate, w_up, w_down, gate_col, l)
        prev = (_moe_combine(y, idx, n_tok), mods)

    return _final_norm(xs, prev[0], prev[1], final_norm_g, n_batch, n_lat)
```

```python
import functools
import math

import numpy as np
import jax
import jax.numpy as jnp
from jax import lax
from jax.experimental import pallas as pl
from jax.experimental.pallas import tpu as pltpu

F32 = jnp.float32
BF16 = jnp.bfloat16

D_MODEL = 1024
DEPTH = 4
GRID_W = 64
GROUP_W = 256
GROUP_HEADS = 4
HEAD_DIM = 64
NA_WIN_H = 8
NA_WIN_W = 16
DN_CONV_K = 5
DN_CHUNK = 64
DIFF_DIM = 32
FT_DIM = 64
N_EXPERTS = 16
EC_FACTOR = 2
D_EXPERT = 2 * D_MODEL
ROPE_BASE = 10000.0
RMS_EPS = 1e-6
NEG = -1e30

ROW_TILE = 256
N_BF = 6 * GROUP_W
N_F32 = 4 * GROUP_W + 128
VMEM_LIMIT = 56 * 1024 * 1024


def _cparams(*sem):
    return pltpu.CompilerParams(dimension_semantics=sem, vmem_limit_bytes=VMEM_LIMIT)


def _split(x):
    hi = x.astype(BF16)
    lo = (x - hi.astype(F32)).astype(BF16)
    return hi, lo


def _dot(a, b):
    return jnp.dot(a, b, preferred_element_type=F32)


def _dot_nt(a, b):
    return lax.dot_general(a, b, (((1,), (1,)), ((), ())), preferred_element_type=F32)


def _dot_tn(a, b):
    return lax.dot_general(a, b, (((0,), (0,)), ((), ())), preferred_element_type=F32)


def _dot3(a, b):
    ah, al = _split(a)
    bh, bl = _split(b)
    return _dot(ah, bh) + (_dot(al, bh) + _dot(ah, bl))


def _dot3_const(ah, al, b):
    bh, bl = _split(b)
    return _dot(ah, bh) + (_dot(al, bh) + _dot(ah, bl))


def _silu(x):
    return x * jax.nn.sigmoid(x)


def _mod_kernel(s_ref, w_ref, b_ref, o_ref):
    s = _silu(s_ref[...])
    o_ref[...] = _dot(s.astype(BF16), w_ref[...].astype(BF16)) + b_ref[...]


def _modulation(cc, w_mod, b_mod):
    depth = w_mod.shape[0]
    nt = 6 * D_MODEL // 1024
    return pl.pallas_call(
        _mod_kernel,
        grid=(depth, nt),
        in_specs=[
            pl.BlockSpec((8, D_MODEL), lambda l, j: (0, 0)),
            pl.BlockSpec((None, D_MODEL, 1024), lambda l, j: (l, 0, j)),
            pl.BlockSpec((None, 1, 1024), lambda l, j: (l, 0, j)),
        ],
        out_specs=pl.BlockSpec((None, 8, 1024), lambda l, j: (l, 0, j)),
        out_shape=jax.ShapeDtypeStruct((depth, 8, 6 * D_MODEL), F32),
        compiler_params=_cparams("parallel", "parallel"),
        name="modulation",
    )(cc, w_mod, b_mod.reshape(depth, 1, 6 * D_MODEL))


def _norm_mod(x, g, shift, scale):
    y = x * lax.rsqrt(jnp.mean(x * x, axis=-1, keepdims=True) + RMS_EPS) * g
    return y * (1.0 + scale) + shift


def _in_kernel(*refs, residual):
    if residual:
        x_ref, moe_ref, mprev_ref, g_ref, m_ref, w_ref, cos_ref, sin_ref, xo_ref, obf_ref, of_ref, ou_ref = refs
        x = x_ref[...] + mprev_ref[5] * moe_ref[...]
        xo_ref[...] = x
    else:
        x_ref, g_ref, m_ref, w_ref, cos_ref, sin_ref, obf_ref, of_ref, ou_ref = refs
        x = x_ref[...]
    h = _norm_mod(x, g_ref[...], m_ref[0], m_ref[1])
    p = _dot(h.astype(BF16), w_ref[...])
    gw = GROUP_W
    lane = lax.broadcasted_iota(jnp.int32, (1, gw), 1)
    even = (lane % 2) == 0
    cos, sin = cos_ref[...], sin_ref[...]

    def rope(t):
        sw = jnp.where(even, pltpu.roll(t, gw - 1, 1), pltpu.roll(t, 1, 1))
        return t * cos + sw * sin

    obf_ref[:, 0:gw] = (p[:, 0:gw] * HEAD_DIM ** -0.5).astype(BF16)
    obf_ref[:, gw:3 * gw] = p[:, gw:3 * gw].astype(BF16)
    obf_ref[:, 3 * gw:4 * gw] = (rope(p[:, 3 * gw:4 * gw]) * DIFF_DIM ** -0.5).astype(BF16)
    obf_ref[:, 4 * gw:5 * gw] = rope(p[:, 4 * gw:5 * gw]).astype(BF16)
    obf_ref[:, 5 * gw:6 * gw] = p[:, 5 * gw:6 * gw].astype(BF16)
    of_ref[:, 0:4 * gw] = p[:, 6 * gw:10 * gw]
    of_ref[:, 4 * gw:4 * gw + 128] = p[:, 11 * gw:11 * gw + 128]
    ou_ref[...] = p[:, 10 * gw:11 * gw]


def _in_proj(x, prev, g, mods, w_perm, cos_t, sin_t, n_batch, n_tok):
    nt = n_tok // ROW_TILE
    ctx_tile = nt - 1
    n_out = w_perm.shape[1]
    row = lambda b, i: (b, i, 0)
    x_spec = pl.BlockSpec((None, ROW_TILE, D_MODEL), row)
    mod_spec = pl.BlockSpec((None, 6, 1, D_MODEL), lambda b, i: (jnp.where(i == ctx_tile, n_batch, b), 0, 0, 0))
    in_specs = [
        pl.BlockSpec((1, D_MODEL), lambda b, i: (0, 0)),
        mod_spec,
        pl.BlockSpec((D_MODEL, n_out), lambda b, i: (0, 0)),
        pl.BlockSpec((ROW_TILE, GROUP_W), lambda b, i: (i, 0)),
        pl.BlockSpec((ROW_TILE, GROUP_W), lambda b, i: (i, 0)),
    ]
    out_specs = [
        pl.BlockSpec((None, ROW_TILE, N_BF), row),
        pl.BlockSpec((None, ROW_TILE, N_F32), row),
        pl.BlockSpec((None, ROW_TILE, GROUP_W), row),
    ]
    out_shape = [
        jax.ShapeDtypeStruct((n_batch, n_tok, N_BF), BF16),
        jax.ShapeDtypeStruct((n_batch, n_tok, N_F32), F32),
        jax.ShapeDtypeStruct((n_batch, n_tok, GROUP_W), F32),
    ]
    args = (g.reshape(1, D_MODEL), mods, w_perm, cos_t, sin_t)
    if prev is None:
        in_specs, args = [x_spec] + in_specs, (x,) + args
    else:
        in_specs, args = [x_spec, x_spec, mod_spec] + in_specs, (x,) + tuple(prev) + args
        out_specs = [x_spec] + out_specs
        out_shape = [jax.ShapeDtypeStruct((n_batch, n_tok, D_MODEL), F32)] + out_shape
    outs = pl.pallas_call(
        functools.partial(_in_kernel, residual=prev is not None),
        grid=(n_batch, nt),
        in_specs=in_specs,
        out_specs=out_specs,
        out_shape=out_shape,
        compiler_params=_cparams("parallel", "parallel"),
        name="in_proj",
    )(*args)
    return tuple(outs) if prev is not None else (x,) + tuple(outs)


def _na_kernel(q_ref, k_ref, v_ref, bias_ref, o_ref, *, n_lat, n_ctx):
    i = pl.program_id(1)
    rows = n_lat // GRID_W
    wh = min(NA_WIN_H, rows)
    n_win = wh * GRID_W
    per = ROW_TILE // GRID_W
    kc = k_ref[n_lat:n_lat + n_ctx, :]
    vc = v_ref[n_lat:n_lat + n_ctx, :]
    lane = lax.broadcasted_iota(jnp.int32, (GRID_W, GROUP_W), 1)
    masks = [lane // HEAD_DIM == h for h in range(GROUP_HEADS)]
    outs = []
    for u in range(per):
        r = i * per + u
        is_lat = r < rows
        base = jnp.where(is_lat, jnp.clip(r - wh // 2, 0, rows - wh), 0)
        slab = jnp.where(is_lat, base - r + NA_WIN_H - 1, NA_WIN_H)
        start = pl.multiple_of(base * GRID_W, GRID_W)
        q = q_ref[u * GRID_W:(u + 1) * GRID_W, :]
        qm = jnp.concatenate([jnp.where(m, q, jnp.zeros_like(q)) for m in masks], axis=0)
        keys = jnp.concatenate([k_ref[pl.ds(start, n_win), :], kc], axis=0)
        vals = jnp.concatenate([v_ref[pl.ds(start, n_win), :], vc], axis=0)
        s = _dot_nt(qm, keys)
        sw = s[:, :n_win] + bias_ref[slab].reshape(GROUP_HEADS * GRID_W, n_win)
        sc = s[:, n_win:]
        m = jnp.maximum(jnp.max(sw, axis=-1, keepdims=True), jnp.max(sc, axis=-1, keepdims=True))
        pw = jnp.exp(sw - m)
        pc = jnp.exp(sc - m)
        l = jnp.sum(pw, axis=-1, keepdims=True) + jnp.sum(pc, axis=-1, keepdims=True)
        res = _dot(jnp.concatenate([pw, pc], axis=1).astype(BF16), vals) / l
        o = jnp.where(masks[0], res[0:GRID_W], 0.0)
        for h in range(1, GROUP_HEADS):
            o = o + jnp.where(masks[h], res[h * GRID_W:(h + 1) * GRID_W], 0.0)
        outs.append(o)
    o_ref[...] = jnp.concatenate(outs, axis=0).astype(BF16)


def _na_attention(pbf, bias_tab, n_batch, n_lat, n_ctx):
    n_tok = n_lat + n_ctx
    return pl.pallas_call(
        functools.partial(_na_kernel, n_lat=n_lat, n_ctx=n_ctx),
        grid=(n_batch, n_tok // ROW_TILE),
        in_specs=[
            pl.BlockSpec((None, ROW_TILE, GROUP_W), lambda b, i: (b, i, 0)),
            pl.BlockSpec((None, n_tok, GROUP_W), lambda b, i: (b, 0, 1)),
            pl.BlockSpec((None, n_tok, GROUP_W), lambda b, i: (b, 0, 2)),
            pl.BlockSpec(bias_tab.shape, lambda b, i: (0, 0, 0, 0)),
        ],
        out_specs=pl.BlockSpec((None, ROW_TILE, GROUP_W), lambda b, i: (b, i, 0)),
        out_shape=jax.ShapeDtypeStruct((n_batch, n_tok, GROUP_W), BF16),
        compiler_params=_cparams("parallel", "arbitrary"),
        name="na_attention",
    )(pbf, pbf, pbf, bias_tab)


def _na_bias_table(rpb):
    wh = NA_WIN_H
    cq = np.arange(GRID_W)
    c0 = np.clip(cq - NA_WIN_W // 2, 0, GRID_W - NA_WIN_W)
    in_win = (cq[None, :] >= c0[:, None]) & (cq[None, :] < c0[:, None] + NA_WIN_W)
    dx = np.clip(cq[None, :] - cq[:, None], 1 - NA_WIN_W, NA_WIN_W - 1) + NA_WIN_W - 1
    dy = np.arange(wh)[:, None] + np.arange(wh)[None, :]
    pick_x = (dx[:, :, None] == np.arange(2 * NA_WIN_W - 1)).astype(np.float32)
    pick_y = (dy[:, :, None] == np.arange(2 * NA_WIN_H - 1)).astype(np.float32)
    hi = lax.Precision.HIGHEST
    cols = jnp.einsum('hyx,qkx->hyqk', rpb.astype(F32), pick_x, precision=hi)
    tab = jnp.einsum('swy,hyqk->shqwk', pick_y, cols, precision=hi)
    tab = jnp.where(in_win[None, None, :, None, :], tab, NEG)
    tab = tab.reshape(wh, GROUP_HEADS, GRID_W, wh * GRID_W)
    return jnp.concatenate([tab, jnp.full((1,) + tab.shape[1:], NEG, F32)], axis=0)


DF_KV = 512


def _df_kernel(lam_ref, q_ref, k_ref, v_ref, g_ref, o_ref, s_ref, *, chunks, lam_init):
    lq1, lk1, lq2, lk2 = lam_ref[0:1, :], lam_ref[1:2, :], lam_ref[2:3, :], lam_ref[3:4, :]
    lam = (jnp.exp(jnp.sum(lq1 * lk1, axis=-1, keepdims=True))
           - jnp.exp(jnp.sum(lq2 * lk2, axis=-1, keepdims=True)) + lam_init)
    q = q_ref[...]
    tq, gw = q.shape
    lane = lax.broadcasted_iota(jnp.int32, (tq, gw), 1)
    row2 = lax.broadcasted_iota(jnp.int32, (2 * tq, 1), 0)
    origin = chunks[0][0]
    zero = jnp.zeros_like(q)

    def head(h, out, slot):
        qm = [jnp.where(lane // DIFF_DIM == 2 * h + mp, q, zero) for mp in range(2)]
        m = [jnp.full((tq, 128), NEG, F32) for _ in range(2)]
        for st, sz in chunks:
            k = k_ref[st:st + sz, :]
            for mp in range(2):
                s = _dot_nt(qm[mp], k)
                s_ref[(slot + mp) * tq:(slot + mp + 1) * tq, st - origin:st - origin + sz] = s
                for t in range(sz // 128):
                    m[mp] = jnp.maximum(m[mp], s[:, t * 128:(t + 1) * 128])
        m = [jnp.max(x, axis=-1, keepdims=True) for x in m]
        ones_lane = ((h + 1) % GROUP_HEADS) * HEAD_DIM
        acc = [jnp.zeros((tq, gw), F32) for _ in range(2)]
        for st, sz in chunks:
            v = v_ref[st:st + sz, :]
            lane_v = lax.broadcasted_iota(jnp.int32, v.shape, 1)
            v = jnp.where(lane_v == ones_lane, jnp.ones_like(v), v)
            for mp in range(2):
                e = jnp.exp((s_ref[(slot + mp) * tq:(slot + mp + 1) * tq, st - origin:st - origin + sz] - m[mp]).astype(BF16))
                acc[mp] = acc[mp] + _dot(e, v)
        w = []
        for mp in range(2):
            l = jnp.sum(jnp.where(lane == ones_lane, acc[mp], 0.0), axis=-1, keepdims=True)
            w.append(acc[mp] * ((1.0 if mp == 0 else lam) / l))
        return out + jnp.where(lane // HEAD_DIM == h, w[0] - w[1], 0.0)

    def head_pair(i, out):
        return head(2 * i + 1, head(2 * i, out, 0), 2)

    o = lax.fori_loop(0, GROUP_HEADS // 2, head_pair, jnp.zeros((tq, gw), F32))
    sq = o * o
    scale = jnp.zeros((tq, gw), F32)
    for h in range(GROUP_HEADS):
        in_head = lane // HEAD_DIM == h
        ms = jnp.sum(jnp.where(in_head, sq, 0.0), axis=-1, keepdims=True) * (1.0 / HEAD_DIM)
        scale = jnp.where(in_head, lax.rsqrt(ms + RMS_EPS), scale)
    o_ref[...] = (o * scale * g_ref[...] * (1.0 - lam_init)).astype(BF16)


def _df_call(pbf, df_lambda, g_tiled, lam_init, n_batch, n_tok, first_tile, n_tiles, chunks, name):
    n_keys = sum(sz for _, sz in chunks)
    return pl.pallas_call(
        functools.partial(_df_kernel, chunks=chunks, lam_init=lam_init),
        grid=(n_batch, n_tiles),
        in_specs=[
            pl.BlockSpec((4, DIFF_DIM), lambda b, i: (0, 0)),
            pl.BlockSpec((None, ROW_TILE, GROUP_W), lambda b, i: (b, first_tile + i, 3)),
            pl.BlockSpec((None, n_tok, GROUP_W), lambda b, i: (b, 0, 4)),
            pl.BlockSpec((None, n_tok, GROUP_W), lambda b, i: (b, 0, 5)),
            pl.BlockSpec((1, GROUP_W), lambda b, i: (0, 0)),
        ],
        out_specs=pl.BlockSpec((None, ROW_TILE, GROUP_W), lambda b, i: (b, i, 0)),
        out_shape=jax.ShapeDtypeStruct((n_batch, n_tiles * ROW_TILE, GROUP_W), BF16),
        scratch_shapes=[pltpu.VMEM((4 * ROW_TILE, n_keys), F32)],
        compiler_params=_cparams("parallel", "arbitrary"),
        name=name,
    )(df_lambda, pbf, pbf, pbf, g_tiled)


def _df_attention(pbf, df_lambda, g_tiled, lam_init, n_batch, n_lat, n_ctx):
    n_tok = n_lat + n_ctx
    lat_chunks = tuple((st, min(DF_KV, n_tok - st)) for st in range(0, n_tok, DF_KV))
    o_lat = _df_call(pbf, df_lambda, g_tiled, lam_init, n_batch, n_tok, 0, n_lat // ROW_TILE, lat_chunks, "df_attention")
    o_ctx = _df_call(pbf, df_lambda, g_tiled, lam_init, n_batch, n_tok, n_lat // ROW_TILE, n_ctx // ROW_TILE,
                     ((n_lat, n_ctx),), "df_attention_ctx")
    return jnp.concatenate([o_lat, o_ctx], axis=1)


CONV_PAD = 8


def _dn_prep_kernel(u_ref, w_ref, o_ref, pad_ref, *, n_lat, n_ctx):
    j = pl.program_id(1)
    zeros = jnp.zeros((CONV_PAD, GROUP_W), F32)
    lat0 = CONV_PAD
    ctx0 = 2 * CONV_PAD + n_lat
    pad_ref[0:CONV_PAD, :] = zeros
    pad_ref[lat0 + n_lat:ctx0, :] = zeros
    pad_ref[ctx0 + n_ctx:ctx0 + n_ctx + CONV_PAD, :] = zeros
    pad_ref[lat0:lat0 + n_lat, :] = u_ref[0:n_lat, :]
    pad_ref[ctx0:ctx0 + n_ctx, :] = u_ref[n_lat:n_lat + n_ctx, :]
    qscale = jnp.where(j == 0, HEAD_DIM ** -0.5, 1.0)
    half = DN_CONV_K // 2
    for (src0, dst0, n) in ((lat0, 0, n_lat), (ctx0, n_lat, n_ctx)):
        for t0 in range(0, n, ROW_TILE):
            acc = jnp.zeros((ROW_TILE, GROUP_W), F32)
            for tap in range(DN_CONV_K):
                a = src0 + t0 + tap - half
                acc = acc + pad_ref[a:a + ROW_TILE, :] * w_ref[tap:tap + 1, :]
            y = _silu(acc)
            parts = []
            for h in range(GROUP_HEADS):
                yh = y[:, h * HEAD_DIM:(h + 1) * HEAD_DIM]
                nrm = lax.rsqrt(jnp.sum(yh * yh, axis=-1, keepdims=True) + 1e-6) * qscale
                parts.append(yh * jnp.where(j == 2, 1.0, nrm))
            o_ref[dst0 + t0:dst0 + t0 + ROW_TILE, :] = jnp.concatenate(parts, axis=-1)


def _dn_prep(pf, conv_w, n_batch, n_lat, n_ctx):
    n_tok = n_lat + n_ctx
    return pl.pallas_call(
        functools.partial(_dn_prep_kernel, n_lat=n_lat, n_ctx=n_ctx),
        grid=(n_batch, 3),
        in_specs=[
            pl.BlockSpec((None, n_tok, GROUP_W), lambda b, j: (b, 0, j)),
            pl.BlockSpec((DN_CONV_K, GROUP_W), lambda b, j: (0, j)),
        ],
        out_specs=pl.BlockSpec((None, n_tok, GROUP_W), lambda b, j: (b, 0, j)),
        out_shape=jax.ShapeDtypeStruct((n_batch, n_tok, 3 * GROUP_W), F32),
        scratch_shapes=[pltpu.VMEM((n_tok + 3 * CONV_PAD, GROUP_W), F32)],
        compiler_params=_cparams("parallel", "arbitrary"),
        name="dn_prep",
    )(pf, conv_w)


DN_BLOCK = 4 * DN_CHUNK
DN_SAMPLES = 2


def _split3(x):
    x1 = x.astype(BF16)
    r = x - x1.astype(F32)
    x2 = r.astype(BF16)
    return x1, x2, (r - x2.astype(F32)).astype(BF16)


def _block_diag(x, masks):
    zero = jnp.zeros_like(x)
    return jnp.concatenate([jnp.where(m, x, zero) for m in masks], axis=0)


def _mm3_heads(lhs, rhs, masks):
    lh, ll = _split(lhs)
    rh, rl = _split(rhs)
    bh, bl = _block_diag(rh, masks), _block_diag(rl, masks)
    return _dot(jnp.concatenate([lh, ll, lh], axis=1), jnp.concatenate([bh, bh, bl], axis=0))


def _dn_kernel(qf_ref, kf_ref, vf_ref, abf_ref, qb_ref, kb_ref, vb_ref, abb_ref, av_ref, dt_ref, exp_ref,
               of_ref, ob_ref, s_ref):
    @pl.when(pl.program_id(1) == 0)
    def _():
        s_ref[...] = jnp.zeros_like(s_ref)

    c, nb, gw = DN_CHUNK, DN_BLOCK, GROUP_W
    n_sub = nb // c
    lane4 = lax.broadcasted_iota(jnp.int32, (c, gw), 1)
    masks = [(lane4 // HEAD_DIM) == h for h in range(GROUP_HEADS)]
    ri = lax.broadcasted_iota(jnp.int32, (c, gw), 0)
    cj = lane4 % HEAD_DIM
    eye = (ri == cj).astype(F32)
    bi = lax.broadcasted_iota(jnp.int32, (nb, nb), 0)
    bj = lax.broadcasted_iota(jnp.int32, (nb, nb), 1)
    same_chunk = (bi // c) == (bj // c)
    lane1 = lax.broadcasted_iota(jnp.int32, (nb, 128), 1)
    n_samples = qf_ref.shape[0]
    streams = []
    for bb in range(n_samples):
        streams.append((0, qf_ref.at[bb], kf_ref.at[bb], vf_ref.at[bb], abf_ref.at[bb]))
        streams.append((1, qb_ref.at[bb], kb_ref.at[bb], vb_ref.at[bb], abb_ref.at[bb]))

    chains = []
    for sidx, (d, q_ref, k_ref, v_ref, ab_ref) in enumerate(streams):
        q_all, k_all, v_all = q_ref[...], k_ref[...], v_ref[...]
        for j in range(n_sub):
            rows = slice(j * c, (j + 1) * c)
            q, k = q_all[rows], k_all[rows]
            kq = _dot_nt(jnp.concatenate([k, q], axis=0).astype(BF16), _block_diag(k.astype(BF16), masks))
            chains.append(dict(s=sidx, d=d, rows=rows, q=q, k=k, v=v_all[rows], kq=kq))
    gates = []
    for d, q_ref, k_ref, v_ref, ab_ref in streams:
        ab = ab_ref[...]
        x = ab + dt_ref[...]
        softplus = jnp.maximum(x, 0.0) + jnp.log1p(jnp.exp(-jnp.abs(x)))
        g = -jnp.exp(av_ref[...]) * softplus
        tri = (same_chunk & ((bi >= bj) if d == 0 else (bi <= bj))).astype(BF16)
        g1, g2, g3 = _split3(g)
        gcs = _dot(tri, g1) + (_dot(tri, g2) + _dot(tri, g3))
        gates.append(jnp.where(lane1 < 2 * GROUP_HEADS, gcs, jax.nn.sigmoid(ab)))
    spread = []
    for sidx, stream in enumerate(streams):
        x1, x2, x3 = _split3(gates[sidx])
        e = exp_ref[stream[0]]
        spread.append(_dot(x1, e) + (_dot(x2, e) + _dot(x3, e)))
    for ch in chains:
        d, q, k, v, kq = ch["d"], ch["q"], ch["k"], ch["v"], ch["kq"]
        incl = (ri >= cj) if d == 0 else (ri <= cj)
        strict = (ri > cj) if d == 0 else (ri < cj)
        last = c - 1 if d == 0 else 0
        gc, beta = spread[ch["s"]][ch["rows"], :gw], spread[ch["s"]][ch["rows"], gw:]
        gc_row = jnp.sum(gc * eye, axis=0, keepdims=True)
        decay = jnp.exp(jnp.where(incl, gc - gc_row, NEG))
        eg = jnp.exp(gc)
        g_last = gc[last:last + 1, :]
        ch.update(
            lm=jnp.where(strict, kq[:c] * beta * decay, 0.0),
            qk=jnp.where(incl, kq[c:] * decay, 0.0).astype(BF16),
            rhs=jnp.concatenate([_block_diag((v * beta).astype(BF16), masks),
                                 _block_diag((k * beta * eg).astype(BF16), masks)], axis=1),
            q_dec=q * eg,
            k_dec=(k * jnp.exp(g_last - gc)).astype(BF16),
            g_end=jnp.exp(g_last))

    def neumann_level(first):
        for ch in chains:
            if first:
                ch["p"] = eye - ch["lm"]
                ch["sq"] = _mm3_heads(ch["lm"], ch["lm"], masks)
            else:
                r = _mm3_heads(jnp.concatenate([ch["sq"], ch["p"]], axis=0), ch["sq"], masks)
                ch["sq"], ch["p"] = r[:c], ch["p"] + r[c:]

    def head_blocks(full):
        out = jnp.where(masks[0], full[0:c], 0.0)
        for h in range(1, GROUP_HEADS):
            out = out + jnp.where(masks[h], full[h * c:(h + 1) * c], 0.0)
        return out

    n_levels = int(math.log2(c)) - 1
    for level in range(n_levels):
        neumann_level(level == 0)
    t_inv = [ch["p"] + _mm3_heads(ch["p"], ch["sq"], masks) for ch in chains]
    uw = [_dot(t.astype(BF16), ch["rhs"]).astype(BF16) for t, ch in zip(t_inv, chains)]
    kd_uw = [_dot_tn(ch["k_dec"], x) for x, ch in zip(uw, chains)]
    qk_uw = [_dot(ch["qk"], jnp.concatenate([_block_diag(x[:, :gw], masks), _block_diag(x[:, gw:], masks)], axis=1))
             for x, ch in zip(uw, chains)]
    solved = [dict(ms=jnp.concatenate([head_blocks(kd[:, gw:]), ch["q_dec"] - qq[:, gw:]], axis=0).astype(BF16),
                   b=head_blocks(kd[:, :gw]), d=qq[:, :gw], g_end=ch["g_end"])
              for kd, qq, ch in zip(kd_uw, qk_uw, chains)]

    states = [s_ref[sidx] for sidx in range(len(streams))]
    outs = [[None] * n_sub for _ in streams]
    for step in range(n_sub):
        for sidx, stream in enumerate(streams):
            j = step if stream[0] == 0 else n_sub - 1 - step
            ch = solved[sidx * n_sub + j]
            s = states[sidx]
            res = _dot(ch["ms"], _block_diag(s.astype(BF16), masks))
            outs[sidx][j] = res[c:] + ch["d"]
            states[sidx] = s * ch["g_end"] - res[:c] + ch["b"]
    of_ref[...] = jnp.stack([jnp.concatenate(outs[2 * bb], axis=0) for bb in range(n_samples)], axis=0)
    ob_ref[...] = jnp.stack([jnp.concatenate(outs[2 * bb + 1], axis=0) for bb in range(n_samples)], axis=0)
    s_ref[...] = jnp.stack(states, axis=0)


def _dn_scan(qkv, pf, a_vec, dt_vec, n_batch, n_lat, n_ctx):
    n_tok = n_lat + n_ctx
    assert n_ctx == DN_BLOCK and n_lat % DN_BLOCK == 0
    nblk = n_tok // DN_BLOCK
    ab_blk = 4 * GROUP_W // 128

    def fwd(n):
        return jnp.where(n == 0, nblk - 1, n - 1)

    def bwd(n):
        return jnp.where(n == 0, nblk - 1, nblk - 1 - n)

    assert n_batch % DN_SAMPLES == 0

    def spec(order, blk, width):
        return pl.BlockSpec((DN_SAMPLES, DN_BLOCK, width), lambda b, n: (b, order(n), blk))

    spread = np.zeros((2, 128, 2 * GROUP_W), np.float32)
    for d in range(2):
        for h in range(GROUP_HEADS):
            spread[d, GROUP_HEADS * d + h, h * HEAD_DIM:(h + 1) * HEAD_DIM] = 1.0
            spread[d, 2 * GROUP_HEADS + GROUP_HEADS * d + h, GROUP_W + h * HEAD_DIM:GROUP_W + (h + 1) * HEAD_DIM] = 1.0

    lane = pl.BlockSpec((1, 128), lambda b, n: (0, 0))
    out = jax.ShapeDtypeStruct((n_batch, n_tok, GROUP_W), F32)
    return pl.pallas_call(
        _dn_kernel,
        grid=(n_batch // DN_SAMPLES, nblk),
        in_specs=[spec(fwd, 0, GROUP_W), spec(fwd, 1, GROUP_W), spec(fwd, 2, GROUP_W), spec(fwd, ab_blk, 128),
                  spec(bwd, 0, GROUP_W), spec(bwd, 1, GROUP_W), spec(bwd, 2, GROUP_W), spec(bwd, ab_blk, 128),
                  lane, lane, pl.BlockSpec((2, 128, 2 * GROUP_W), lambda b, n: (0, 0, 0))],
        out_specs=[spec(fwd, 0, GROUP_W), spec(bwd, 0, GROUP_W)],
        out_shape=[out, out],
        scratch_shapes=[pltpu.VMEM((2 * DN_SAMPLES, HEAD_DIM, GROUP_W), F32)],
        compiler_params=_cparams("parallel", "arbitrary"),
        name="dn_scan",
    )(qkv, qkv, qkv, pf, qkv, qkv, qkv, pf, a_vec, dt_vec, jnp.asarray(spread, BF16))


FT_N2 = 64


def _ft1_kernel(x_ref, fh_ref, fl_ref, o_ref):
    o_ref[...] = _dot3_const(fh_ref[...], fl_ref[...], x_ref[...])


def _ft_stage1(u_rows, f_hi, f_lo, n_batch, n1, tn=2048):
    width = u_rows.shape[2]
    return pl.pallas_call(
        _ft1_kernel,
        grid=(n_batch, width // tn),
        in_specs=[
            pl.BlockSpec((None, n1, tn), lambda b, j: (b, 0, j)),
            pl.BlockSpec((2 * n1, n1), lambda b, j: (0, 0)),
            pl.BlockSpec((2 * n1, n1), lambda b, j: (0, 0)),
        ],
        out_specs=pl.BlockSpec((None, 2 * n1, tn), lambda b, j: (b, 0, j)),
        out_shape=jax.ShapeDtypeStruct((n_batch, 2 * n1, width), F32),
        compiler_params=_cparams("parallel", "parallel"),
        name="ft_stage1",
    )(u_rows, f_hi, f_lo)


def _ft3_kernel(y_ref, gh_ref, gl_ref, ch_ref, cl_ref, sh_ref, sl_ref, w_ref, o_ref, *, pb, n2, norm):
    vr, vi = [], []
    for p in range(pb):
        rhs = jnp.concatenate([y_ref[0, p], y_ref[1, p]], axis=0)
        v = _dot3_const(gh_ref[p], gl_ref[p], rhs)
        vr.append(v[:n2])
        vi.append(v[n2:])
    vr = jnp.concatenate(vr, axis=0)
    vi = jnp.concatenate(vi, axis=0)
    vrh, vrl = _split(vr)
    vih, vil = _split(vi)
    y = (_dot(vrh, ch_ref[...]) + (_dot(vrl, ch_ref[...]) + _dot(vrh, cl_ref[...]))
         + _dot(vih, sh_ref[...]) + (_dot(vil, sh_ref[...]) + _dot(vih, sl_ref[...]))) * norm
    o = _dot(y.astype(BF16), w_ref[...])
    for p in range(pb):
        o_ref[:, p, :] = o[p * n2:(p + 1) * n2, :]


def _ft_stage2(y1, g_hi, g_lo, chan, ft_w, n_batch, n1, n2, pb):
    ch, cl, sh, sl = chan
    full = lambda b, i: (0, 0)
    return pl.pallas_call(
        functools.partial(_ft3_kernel, pb=pb, n2=n2, norm=1.0 / math.sqrt(n1 * n2 * FT_DIM)),
        grid=(n_batch, n1 // pb),
        in_specs=[
            pl.BlockSpec((None, 2, pb, n2, GROUP_W), lambda b, i: (b, 0, i, 0, 0)),
            pl.BlockSpec((pb, 2 * n2, 2 * n2), lambda b, i: (i, 0, 0)),
            pl.BlockSpec((pb, 2 * n2, 2 * n2), lambda b, i: (i, 0, 0)),
            pl.BlockSpec((GROUP_W, GROUP_W), full), pl.BlockSpec((GROUP_W, GROUP_W), full),
            pl.BlockSpec((GROUP_W, GROUP_W), full), pl.BlockSpec((GROUP_W, GROUP_W), full),
            pl.BlockSpec((GROUP_W, GROUP_W), full),
        ],
        out_specs=pl.BlockSpec((None, n2, pb, GROUP_W), lambda b, i: (b, 0, i, 0)),
        out_shape=jax.ShapeDtypeStruct((n_batch, n2, n1, GROUP_W), F32),
        compiler_params=_cparams("parallel", "parallel"),
        name="ft_stage2",
    )(y1, g_hi, g_lo, ch, cl, sh, sl, ft_w)


def _np_split(a):
    hi = jnp.asarray(a, F32).astype(BF16)
    lo = (jnp.asarray(a, F32) - hi.astype(F32)).astype(BF16)
    return hi, lo


@functools.lru_cache(maxsize=None)
def _ft_tables(n1, n2):
    n = n1 * n2
    p = np.arange(n1)
    ang1 = 2.0 * np.pi * ((p[:, None] * p[None, :]) % n1) / n1
    f1 = np.concatenate([np.cos(ang1), -np.sin(ang1)], axis=0)
    q = np.arange(n2)
    b = np.arange(n2)
    phase = (b[None, None, :] * (p[:, None, None] + n1 * q[None, :, None])) % n
    psi = 2.0 * np.pi * phase / n
    gc, gs = np.cos(psi), np.sin(psi)
    g = np.concatenate([np.concatenate([gc, gs], axis=2), np.concatenate([-gs, gc], axis=2)], axis=1)
    c = np.arange(FT_DIM)
    angc = 2.0 * np.pi * ((c[:, None] * c[None, :]) % FT_DIM) / FT_DIM
    eye = np.eye(GROUP_W // FT_DIM)
    cc, sc = np.kron(eye, np.cos(angc)), np.kron(eye, np.sin(angc))
    return f1.astype(np.float32), g.astype(np.float32), cc.astype(np.float32), sc.astype(np.float32)


def _out_kernel(x_ref, m_ref, na_ref, of_ref, ob_ref, gate_ref, dng_ref, df_ref, ft_ref, w_ref, g2_ref, wr_ref,
                xo_ref, h_ref, aff_ref):
    o = of_ref[...] + ob_ref[...]
    parts = []
    for h in range(GROUP_HEADS):
        oh = o[:, h * HEAD_DIM:(h + 1) * HEAD_DIM]
        parts.append(oh * lax.rsqrt(jnp.mean(oh * oh, axis=-1, keepdims=True) + RMS_EPS))
    y_dn = jnp.concatenate(parts, axis=-1) * dng_ref[...] * _silu(gate_ref[...])
    mix = jnp.concatenate([na_ref[...], y_dn.astype(BF16), df_ref[...], ft_ref[...].astype(BF16)], axis=-1)
    x = x_ref[...] + m_ref[2] * _dot(mix, w_ref[...])
    xo_ref[...] = x
    h2 = _norm_mod(x, g2_ref[...], m_ref[3], m_ref[4])
    bits = pltpu.bitcast(h2.astype(BF16).astype(F32), jnp.uint32)
    half = D_MODEL // 2
    h_ref[...] = bits[:, half:] | (bits[:, :half] >> 16)
    logits = _dot3(h2, wr_ref[...])
    lane = lax.broadcasted_iota(jnp.int32, logits.shape, 1)
    logits = jnp.where(lane < N_EXPERTS, logits, NEG)
    e = jnp.exp(logits - jnp.max(logits, axis=-1, keepdims=True))
    aff_ref[...] = e / jnp.sum(e, axis=-1, keepdims=True)


def _out_proj(x, mods, o_na, o_f, o_b, pf, dn_g, o_df, o_ft, w_out, g2, w_router, n_batch, n_tok):
    nt = n_tok // ROW_TILE
    ctx_tile = nt - 1
    row = lambda b, i: (b, i, 0)
    blk = lambda: pl.BlockSpec((None, ROW_TILE, GROUP_W), row)
    vec = lambda n: pl.BlockSpec((1, n), lambda b, i: (0, 0))
    return pl.pallas_call(
        _out_kernel,
        grid=(n_batch, nt),
        in_specs=[
            pl.BlockSpec((None, ROW_TILE, D_MODEL), row),
            pl.BlockSpec((None, 6, 1, D_MODEL), lambda b, i: (jnp.where(i == ctx_tile, n_batch, b), 0, 0, 0)),
            blk(), blk(), blk(),
            pl.BlockSpec((None, ROW_TILE, GROUP_W), lambda b, i: (b, i, 3)),
            vec(GROUP_W), blk(), blk(),
            pl.BlockSpec((D_MODEL, D_MODEL), lambda b, i: (0, 0)),
            vec(D_MODEL),
            pl.BlockSpec((D_MODEL, 128), lambda b, i: (0, 0)),
        ],
        out_specs=[
            pl.BlockSpec((None, ROW_TILE, D_MODEL), row),
            pl.BlockSpec((None, ROW_TILE, D_MODEL // 2), row),
            pl.BlockSpec((None, ROW_TILE, 128), row),
        ],
        out_shape=[
            jax.ShapeDtypeStruct((n_batch, n_tok, D_MODEL), F32),
            jax.ShapeDtypeStruct((n_batch, n_tok, D_MODEL // 2), jnp.uint32),
            jax.ShapeDtypeStruct((n_batch, n_tok, 128), F32),
        ],
        compiler_params=_cparams("parallel", "parallel"),
        name="out_proj_router",
    )(x, mods, o_na, o_f, o_b, pf, dn_g, o_df, o_ft, w_out, g2.reshape(1, D_MODEL), w_router)


MOE_TF = 256
GATHER_UNROLL = 8


def _gather_kernel(idx_ref, h_ref, o_ref, *, cap, n_e):
    base = (pl.program_id(0) * n_e + pl.program_id(1)) * cap

    def body(j, carry):
        for u in range(GATHER_UNROLL):
            o_ref[j, u:u + 1, :] = h_ref[pl.ds(idx_ref[base + j * GATHER_UNROLL + u], 1), :]
        return carry

    lax.fori_loop(0, cap // GATHER_UNROLL, body, 0)


def _moe_gather(hp, idx):
    n_batch, n_tok, width = hp.shape
    _, n_e, cap = idx.shape
    assert cap % GATHER_UNROLL == 0 and GATHER_UNROLL == 8
    groups = cap // GATHER_UNROLL
    out = pl.pallas_call(
        functools.partial(_gather_kernel, cap=cap, n_e=n_e),
        grid_spec=pltpu.PrefetchScalarGridSpec(
            num_scalar_prefetch=1,
            grid=(n_batch, n_e),
            in_specs=[pl.BlockSpec((None, n_tok, width), lambda b, e, idx_ref: (b, 0, 0))],
            out_specs=pl.BlockSpec((None, groups, GATHER_UNROLL, width), lambda b, e, idx_ref: (e, b, 0, 0)),
        ),
        out_shape=jax.ShapeDtypeStruct((n_e, n_batch * groups, GATHER_UNROLL, width), jnp.uint32),
        compiler_params=_cparams("parallel", "arbitrary"),
        name="moe_gather",
    )(idx.reshape(-1), hp)
    return out.reshape(n_e, n_batch * cap, width)


def _combine_kernel(idx_ref, y_ref, o_ref, *, cap, n_e):
    e = pl.program_id(1)

    @pl.when(e == 0)
    def _():
        o_ref[...] = jnp.zeros_like(o_ref)

    base = (pl.program_id(0) * n_e + e) * cap

    def body(j, carry):
        first = j * GATHER_UNROLL
        toks = [idx_ref[base + first + u] for u in range(GATHER_UNROLL)]
        rows = [o_ref[pl.ds(t, 1), :] + y_ref[j, u:u + 1, :] for u, t in enumerate(toks)]
        for t, row in zip(toks, rows):
            o_ref[pl.ds(t, 1), :] = row
        return carry

    lax.fori_loop(0, cap // GATHER_UNROLL, body, 0)


def _moe_combine(y, idx, n_tok):
    n_batch, n_e, cap = idx.shape
    d = y.shape[2]
    groups = cap // GATHER_UNROLL
    return pl.pallas_call(
        functools.partial(_combine_kernel, cap=cap, n_e=n_e),
        grid_spec=pltpu.PrefetchScalarGridSpec(
            num_scalar_prefetch=1,
            grid=(n_batch, n_e),
            in_specs=[pl.BlockSpec((None, groups, GATHER_UNROLL, d), lambda b, e, idx_ref: (e, b, 0, 0))],
            out_specs=pl.BlockSpec((None, n_tok, d), lambda b, e, idx_ref: (b, 0, 0)),
        ),
        out_shape=jax.ShapeDtypeStruct((n_batch, n_tok, d), F32),
        compiler_params=_cparams("parallel", "arbitrary"),
        name="moe_combine",
    )(idx.reshape(-1), y.reshape(n_e, n_batch * groups, GATHER_UNROLL, d))


def _moe_kernel(x_ref, wg_ref, wu_ref, wd_ref, gate_ref, o_ref, xb_ref):
    f = pl.program_id(1)

    @pl.when(f == 0)
    def _():
        packed = x_ref[...]
        xb_ref[...] = jnp.concatenate([pltpu.bitcast(packed << 16, F32),
                                       pltpu.bitcast(packed & jnp.uint32(0xFFFF0000), F32)], axis=1).astype(BF16)
        o_ref[...] = jnp.zeros_like(o_ref)

    x = xb_ref[...]
    a = _dot(x, wg_ref[...].astype(BF16))
    u = _dot(x, wu_ref[...].astype(BF16))
    hid = (_silu(a) * u).astype(BF16)
    o_ref[...] += _dot(hid, wd_ref[...].astype(BF16))

    @pl.when(f == pl.num_programs(1) - 1)
    def _():
        o_ref[...] = o_ref[...] * gate_ref[...]


def _moe_ffn(xg, w_gate, w_up, w_down, gate, layer):
    n_e, m, _ = xg.shape
    return pl.pallas_call(
        _moe_kernel,
        grid=(n_e, D_EXPERT // MOE_TF),
        in_specs=[
            pl.BlockSpec((None, m, D_MODEL // 2), lambda e, f: (e, 0, 0)),
            pl.BlockSpec((None, None, D_MODEL, MOE_TF), lambda e, f: (layer, e, 0, f)),
            pl.BlockSpec((None, None, D_MODEL, MOE_TF), lambda e, f: (layer, e, 0, f)),
            pl.BlockSpec((None, None, MOE_TF, D_MODEL), lambda e, f: (layer, e, f, 0)),
            pl.BlockSpec((None, m, 1), lambda e, f: (e, 0, 0)),
        ],
        out_specs=pl.BlockSpec((None, m, D_MODEL), lambda e, f: (e, 0, 0)),
        out_shape=jax.ShapeDtypeStruct((n_e, m, D_MODEL), F32),
        scratch_shapes=[pltpu.VMEM((m, D_MODEL), BF16)],
        compiler_params=_cparams("parallel", "arbitrary"),
        name="moe_ffn",
    )(xg, w_gate, w_up, w_down, gate)


def _final_kernel(x_ref, moe_ref, m_ref, g_ref, o_ref):
    x = x_ref[...] + m_ref[5] * moe_ref[...]
    o_ref[...] = x * lax.rsqrt(jnp.mean(x * x, axis=-1, keepdims=True) + RMS_EPS) * g_ref[...]


def _final_norm(x, moe, mods, g, n_batch, n_lat):
    row = lambda b, i: (b, i, 0)
    blk = pl.BlockSpec((None, ROW_TILE, D_MODEL), row)
    return pl.pallas_call(
        _final_kernel,
        grid=(n_batch, n_lat // ROW_TILE),
        in_specs=[blk, blk, pl.BlockSpec((None, 6, 1, D_MODEL), lambda b, i: (b, 0, 0, 0)),
                  pl.BlockSpec((1, D_MODEL), lambda b, i: (0, 0))],
        out_specs=blk,
        out_shape=jax.ShapeDtypeStruct((n_batch, n_lat, D_MODEL), F32),
        compiler_params=_cparams("parallel", "parallel"),
        name="final_norm",
    )(x, moe, mods, g.reshape(1, D_MODEL))


def _reorder_in_columns(w):
    gw = GROUP_W
    o = 6 * gw + 4 * GROUP_HEADS
    pad = jnp.zeros((w.shape[0], 128 - 4 * GROUP_HEADS), w.dtype)
    parts = [w[:, 0:3 * gw], w[:, o + gw:o + 4 * gw], w[:, 3 * gw:6 * gw], w[:, o:o + gw],
             w[:, o + 4 * gw:o + 5 * gw], w[:, 6 * gw:o], pad]
    return jnp.concatenate(parts, axis=1).astype(BF16)


@functools.lru_cache(maxsize=None)
def _rope_tables(n_lat, n_ctx):
    t = np.arange(n_lat)
    pos = np.stack([t // GRID_W, t % GRID_W], axis=-1).astype(np.float32)
    n_freq = DIFF_DIM // 4
    inv = (ROPE_BASE ** (-np.arange(n_freq, dtype=np.float32) / n_freq)).astype(np.float32)
    ang = (pos[:, :, None] * inv).reshape(n_lat, 2 * n_freq)
    lane = np.arange(GROUP_W)
    idx = (lane % DIFF_DIM) // 2
    sign = np.where(lane % 2 == 0, -1.0, 1.0)
    cos = np.concatenate([np.cos(ang)[:, idx], np.ones((n_ctx, GROUP_W))], axis=0)
    sin = np.concatenate([np.sin(ang)[:, idx] * sign, np.zeros((n_ctx, GROUP_W))], axis=0)
    return cos.astype(np.float32), sin.astype(np.float32)


def _lane_vec(v, n=128):
    v = v.reshape(-1).astype(F32)
    return jnp.zeros((1, n), F32).at[0, :v.shape[0]].set(v)


def _route(aff, cap):
    gate, idx = lax.top_k(jnp.swapaxes(aff, 1, 2), cap)
    return gate, idx


def kernel(x, c, ctx, c_ctx, w_mod, b_mod, norm1_g, w_in, na_rpb, dn_conv_w, dn_a_log, dn_dt_bias, dn_norm_g, df_lambda, df_norm_g, ft_w, w_out, norm2_g, w_router, w_gate, w_up, w_down, final_norm_g):
    n_batch, n_lat, _ = x.shape
    n_ctx = ctx.shape[1]
    n_tok = n_lat + n_ctx
    depth = w_mod.shape[0]
    assert n_batch + 1 <= 8 and n_ctx == ROW_TILE and n_lat % ROW_TILE == 0

    xs = jnp.concatenate([x, ctx], axis=1)
    cc = jnp.zeros((8, D_MODEL), F32).at[:n_batch].set(c).at[n_batch].set(c_ctx)
    m_all = _modulation(cc, w_mod, b_mod)

    cos_t, sin_t = (jnp.asarray(a) for a in _rope_tables(n_lat, n_ctx))
    n1 = n_lat // FT_N2
    f1, g_lat, cc_m, sc_m = _ft_tables(n1, FT_N2)
    _, g_ctx, _, _ = _ft_tables(1, n_ctx)
    f1h, f1l = _np_split(f1)
    glh, gll = _np_split(g_lat)
    gch, gcl = _np_split(g_ctx)
    chan = _np_split(cc_m) + _np_split(sc_m)
    cap_lat = EC_FACTOR * n_lat // N_EXPERTS
    cap_ctx = EC_FACTOR * n_ctx // N_EXPERTS

    prev = None
    for l in range(depth):
        lam_init = 0.8 - 0.6 * math.exp(-0.3 * l)
        mods = m_all[l, :n_batch + 1].reshape(n_batch + 1, 6, 1, D_MODEL)
        xs, pbf, pf, pu = _in_proj(xs, prev, norm1_g[l], mods, _reorder_in_columns(w_in[l]), cos_t, sin_t, n_batch, n_tok)

        o_na = _na_attention(pbf, _na_bias_table(na_rpb[l]), n_batch, n_lat, n_ctx)

        qkv = _dn_prep(pf, dn_conv_w[l], n_batch, n_lat, n_ctx)
        o_f, o_b = _dn_scan(qkv, pf, _lane_vec(dn_a_log[l]), _lane_vec(dn_dt_bias[l]), n_batch, n_lat, n_ctx)

        o_df = _df_attention(pbf, df_lambda[l], jnp.tile(df_norm_g[l], GROUP_HEADS).reshape(1, GROUP_W),
                             lam_init, n_batch, n_lat, n_ctx)

        ftw = ft_w[l].astype(BF16)
        y1 = _ft_stage1(pu.reshape(n_batch, n_tok // FT_N2, FT_N2 * GROUP_W), f1h, f1l, n_batch, n1)
        o_lat = _ft_stage2(y1.reshape(n_batch, 2, n1, FT_N2, GROUP_W), glh, gll, chan, ftw, n_batch, n1, FT_N2, 8)
        o_lat = o_lat.reshape(n_batch, n_lat, GROUP_W)
        u_ctx = pu[:, n_lat:]
        y1c = jnp.stack([u_ctx, jnp.zeros_like(u_ctx)], axis=1)[:, :, None]
        o_ctx = _ft_stage2(y1c, gch, gcl, chan, ftw, n_batch, 1, n_ctx, 1).reshape(n_batch, n_ctx, GROUP_W)
        o_ft = jnp.concatenate([o_lat, o_ctx], axis=1)

        wr = jnp.pad(w_router[l], ((0, 0), (0, 128 - N_EXPERTS)))
        xs, h2, aff = _out_proj(xs, mods, o_na, o_f, o_b, pf, jnp.tile(dn_norm_g[l], GROUP_HEADS).reshape(1, GROUP_W),
                                o_df, o_ft, w_out[l].astype(BF16), norm2_g[l], wr, n_batch, n_tok)

        gate_l, idx_l = _route(aff[:, :n_lat, :N_EXPERTS], cap_lat)
        gate_c, idx_c = _route(aff[:, n_lat:, :N_EXPERTS], cap_ctx)
        idx = jnp.concatenate([idx_l, idx_c + n_lat], axis=2)
        gate = jnp.concatenate([gate_l, gate_c], axis=2)
        cap = cap_lat + cap_ctx
        gate_col = jnp.swapaxes(gate, 0, 1).reshape(N_EXPERTS, n_batch * cap, 1)
        y = _moe_ffn(_moe_gather(h2, idx), w_gate, w_up, w_down, gate_col, l)
        prev = (_moe_combine(y, idx, n_tok), mods)

    return _final_norm(xs, prev[0], prev[1], final_norm_g, n_batch, n_lat)
```

```python
import functools
import math

import numpy as np
import jax
import jax.numpy as jnp
from jax import lax
from jax.experimental import pallas as pl
from jax.experimental.pallas import tpu as pltpu

F32 = jnp.float32
BF16 = jnp.bfloat16

D_MODEL = 1024
DEPTH = 4
GRID_W = 64
GROUP_W = 256
GROUP_HEADS = 4
HEAD_DIM = 64
NA_WIN_H = 8
NA_WIN_W = 16
DN_CONV_K = 5
DN_CHUNK = 64
DIFF_DIM = 32
FT_DIM = 64
N_EXPERTS = 16
EC_FACTOR = 2
D_EXPERT = 2 * D_MODEL
ROPE_BASE = 10000.0
RMS_EPS = 1e-6
NEG = -1e30

ROW_TILE = 256
N_BF = 6 * GROUP_W
N_F32 = 4 * GROUP_W + 128
VMEM_LIMIT = 56 * 1024 * 1024


def _cparams(*sem):
    return pltpu.CompilerParams(dimension_semantics=sem, vmem_limit_bytes=VMEM_LIMIT)


def _split(x):
    hi = x.astype(BF16)
    lo = (x - hi.astype(F32)).astype(BF16)
    return hi, lo


def _dot(a, b):
    return jnp.dot(a, b, preferred_element_type=F32)


def _dot_nt(a, b):
    return lax.dot_general(a, b, (((1,), (1,)), ((), ())), preferred_element_type=F32)


def _dot_tn(a, b):
    return lax.dot_general(a, b, (((0,), (0,)), ((), ())), preferred_element_type=F32)


def _dot3(a, b):
    ah, al = _split(a)
    bh, bl = _split(b)
    return _dot(ah, bh) + (_dot(al, bh) + _dot(ah, bl))


def _dot3_const(ah, al, b):
    bh, bl = _split(b)
    return _dot(ah, bh) + (_dot(al, bh) + _dot(ah, bl))


def _silu(x):
    return x * jax.nn.sigmoid(x)


def _mod_kernel(s_ref, w_ref, b_ref, o_ref):
    s = _silu(s_ref[...])
    o_ref[...] = _dot(s.astype(BF16), w_ref[...].astype(BF16)) + b_ref[...]


def _modulation(cc, w_mod, b_mod):
    depth = w_mod.shape[0]
    nt = 6 * D_MODEL // 1024
    return pl.pallas_call(
        _mod_kernel,
        grid=(depth, nt),
        in_specs=[
            pl.BlockSpec((8, D_MODEL), lambda l, j: (0, 0)),
            pl.BlockSpec((None, D_MODEL, 1024), lambda l, j: (l, 0, j)),
            pl.BlockSpec((None, 1, 1024), lambda l, j: (l, 0, j)),
        ],
        out_specs=pl.BlockSpec((None, 8, 1024), lambda l, j: (l, 0, j)),
        out_shape=jax.ShapeDtypeStruct((depth, 8, 6 * D_MODEL), F32),
        compiler_params=_cparams("parallel", "parallel"),
        name="modulation",
    )(cc, w_mod, b_mod.reshape(depth, 1, 6 * D_MODEL))


def _norm_mod(x, g, shift, scale):
    y = x * lax.rsqrt(jnp.mean(x * x, axis=-1, keepdims=True) + RMS_EPS) * g
    return y * (1.0 + scale) + shift


def _in_kernel(*refs, residual):
    if residual:
        x_ref, moe_ref, mprev_ref, g_ref, m_ref, w_ref, cos_ref, sin_ref, xo_ref, obf_ref, of_ref, ou_ref = refs
        x = x_ref[...] + mprev_ref[5] * moe_ref[...]
        xo_ref[...] = x
    else:
        x_ref, g_ref, m_ref, w_ref, cos_ref, sin_ref, obf_ref, of_ref, ou_ref = refs
        x = x_ref[...]
    h = _norm_mod(x, g_ref[...], m_ref[0], m_ref[1])
    p = _dot(h.astype(BF16), w_ref[...])
    gw = GROUP_W
    lane = lax.broadcasted_iota(jnp.int32, (1, gw), 1)
    even = (lane % 2) == 0
    cos, sin = cos_ref[...], sin_ref[...]

    def rope(t):
        sw = jnp.where(even, pltpu.roll(t, gw - 1, 1), pltpu.roll(t, 1, 1))
        return t * cos + sw * sin

    obf_ref[:, 0:gw] = (p[:, 0:gw] * HEAD_DIM ** -0.5).astype(BF16)
    obf_ref[:, gw:3 * gw] = p[:, gw:3 * gw].astype(BF16)
    obf_ref[:, 3 * gw:4 * gw] = (rope(p[:, 3 * gw:4 * gw]) * DIFF_DIM ** -0.5).astype(BF16)
    obf_ref[:, 4 * gw:5 * gw] = rope(p[:, 4 * gw:5 * gw]).astype(BF16)
    obf_ref[:, 5 * gw:6 * gw] = p[:, 5 * gw:6 * gw].astype(BF16)
    of_ref[:, 0:4 * gw] = p[:, 6 * gw:10 * gw]
    of_ref[:, 4 * gw:4 * gw + 128] = p[:, 11 * gw:11 * gw + 128]
    ou_ref[...] = p[:, 10 * gw:11 * gw]


def _in_proj(x, prev, g, mods, w_perm, cos_t, sin_t, n_batch, n_tok):
    nt = n_tok // ROW_TILE
    ctx_tile = nt - 1
    n_out = w_perm.shape[1]
    row = lambda b, i: (b, i, 0)
    x_spec = pl.BlockSpec((None, ROW_TILE, D_MODEL), row)
    mod_spec = pl.BlockSpec((None, 6, 1, D_MODEL), lambda b, i: (jnp.where(i == ctx_tile, n_batch, b), 0, 0, 0))
    in_specs = [
        pl.BlockSpec((1, D_MODEL), lambda b, i: (0, 0)),
        mod_spec,
        pl.BlockSpec((D_MODEL, n_out), lambda b, i: (0, 0)),
        pl.BlockSpec((ROW_TILE, GROUP_W), lambda b, i: (i, 0)),
        pl.BlockSpec((ROW_TILE, GROUP_W), lambda b, i: (i, 0)),
    ]
    out_specs = [
        pl.BlockSpec((None, ROW_TILE, N_BF), row),
        pl.BlockSpec((None, ROW_TILE, N_F32), row),
        pl.BlockSpec((None, ROW_TILE, GROUP_W), row),
    ]
    out_shape = [
        jax.ShapeDtypeStruct((n_batch, n_tok, N_BF), BF16),
        jax.ShapeDtypeStruct((n_batch, n_tok, N_F32), F32),
        jax.ShapeDtypeStruct((n_batch, n_tok, GROUP_W), F32),
    ]
    args = (g.reshape(1, D_MODEL), mods, w_perm, cos_t, sin_t)
    if prev is None:
        in_specs, args = [x_spec] + in_specs, (x,) + args
    else:
        in_specs, args = [x_spec, x_spec, mod_spec] + in_specs, (x,) + tuple(prev) + args
        out_specs = [x_spec] + out_specs
        out_shape = [jax.ShapeDtypeStruct((n_batch, n_tok, D_MODEL), F32)] + out_shape
    outs = pl.pallas_call(
        functools.partial(_in_kernel, residual=prev is not None),
        grid=(n_batch, nt),
        in_specs=in_specs,
        out_specs=out_specs,
        out_shape=out_shape,
        compiler_params=_cparams("parallel", "parallel"),
        name="in_proj",
    )(*args)
    return tuple(outs) if prev is not None else (x,) + tuple(outs)


def _na_kernel(q_ref, k_ref, v_ref, bias_ref, o_ref, *, n_lat, n_ctx):
    i = pl.program_id(1)
    rows = n_lat // GRID_W
    wh = min(NA_WIN_H, rows)
    n_win = wh * GRID_W
    per = ROW_TILE // GRID_W
    kc = k_ref[n_lat:n_lat + n_ctx, :]
    vc = v_ref[n_lat:n_lat + n_ctx, :]
    lane = lax.broadcasted_iota(jnp.int32, (GRID_W, GROUP_W), 1)
    masks = [lane // HEAD_DIM == h for h in range(GROUP_HEADS)]
    scores, vals, slabs = [], [], []
    for u in range(per):
        r = i * per + u
        is_lat = r < rows
        base = jnp.where(is_lat, jnp.clip(r - wh // 2, 0, rows - wh), 0)
        slabs.append(jnp.where(is_lat, base - r + NA_WIN_H - 1, NA_WIN_H))
        start = pl.multiple_of(base * GRID_W, GRID_W)
        q = q_ref[u * GRID_W:(u + 1) * GRID_W, :]
        qm = jnp.concatenate([jnp.where(m, q, jnp.zeros_like(q)) for m in masks], axis=0)
        keys = jnp.concatenate([k_ref[pl.ds(start, n_win), :], kc], axis=0)
        vals.append(jnp.concatenate([v_ref[pl.ds(start, n_win), :], vc], axis=0))
        scores.append(_dot_nt(qm, keys))
    probs, denoms = [], []
    for s, slab in zip(scores, slabs):
        sw = s[:, :n_win] + bias_ref[slab].reshape(GROUP_HEADS * GRID_W, n_win)
        sc = s[:, n_win:]
        m = jnp.maximum(jnp.max(sw, axis=-1, keepdims=True), jnp.max(sc, axis=-1, keepdims=True))
        pw = jnp.exp(sw - m)
        pc = jnp.exp(sc - m)
        denoms.append(jnp.sum(pw, axis=-1, keepdims=True) + jnp.sum(pc, axis=-1, keepdims=True))
        probs.append(jnp.concatenate([pw, pc], axis=1).astype(BF16))
    outs = []
    for p, v, l in zip(probs, vals, denoms):
        res = _dot(p, v) / l
        o = jnp.where(masks[0], res[0:GRID_W], 0.0)
        for h in range(1, GROUP_HEADS):
            o = o + jnp.where(masks[h], res[h * GRID_W:(h + 1) * GRID_W], 0.0)
        outs.append(o)
    o_ref[...] = jnp.concatenate(outs, axis=0).astype(BF16)


def _na_attention(pbf, bias_tab, n_batch, n_lat, n_ctx):
    n_tok = n_lat + n_ctx
    return pl.pallas_call(
        functools.partial(_na_kernel, n_lat=n_lat, n_ctx=n_ctx),
        grid=(n_batch, n_tok // ROW_TILE),
        in_specs=[
            pl.BlockSpec((None, ROW_TILE, GROUP_W), lambda b, i: (b, i, 0)),
            pl.BlockSpec((None, n_tok, GROUP_W), lambda b, i: (b, 0, 1)),
            pl.BlockSpec((None, n_tok, GROUP_W), lambda b, i: (b, 0, 2)),
            pl.BlockSpec(bias_tab.shape, lambda b, i: (0, 0, 0, 0)),
        ],
        out_specs=pl.BlockSpec((None, ROW_TILE, GROUP_W), lambda b, i: (b, i, 0)),
        out_shape=jax.ShapeDtypeStruct((n_batch, n_tok, GROUP_W), BF16),
        compiler_params=_cparams("parallel", "arbitrary"),
        name="na_attention",
    )(pbf, pbf, pbf, bias_tab)


def _na_bias_table(rpb):
    wh = NA_WIN_H
    cq = np.arange(GRID_W)
    c0 = np.clip(cq - NA_WIN_W // 2, 0, GRID_W - NA_WIN_W)
    in_win = (cq[None, :] >= c0[:, None]) & (cq[None, :] < c0[:, None] + NA_WIN_W)
    dx = np.clip(cq[None, :] - cq[:, None], 1 - NA_WIN_W, NA_WIN_W - 1) + NA_WIN_W - 1
    dy = np.arange(wh)[:, None] + np.arange(wh)[None, :]
    pick_x = (dx[:, :, None] == np.arange(2 * NA_WIN_W - 1)).astype(np.float32)
    pick_y = (dy[:, :, None] == np.arange(2 * NA_WIN_H - 1)).astype(np.float32)
    hi = lax.Precision.HIGHEST
    cols = jnp.einsum('hyx,qkx->hyqk', rpb.astype(F32), pick_x, precision=hi)
    tab = jnp.einsum('swy,hyqk->shqwk', pick_y, cols, precision=hi)
    tab = jnp.where(in_win[None, None, :, None, :], tab, NEG)
    tab = tab.reshape(wh, GROUP_HEADS, GRID_W, wh * GRID_W)
    return jnp.concatenate([tab, jnp.full((1,) + tab.shape[1:], NEG, F32)], axis=0)


DF_KV = 512


def _df_kernel(lam_ref, q_ref, k_ref, v_ref, g_ref, o_ref, s_ref, *, chunks, lam_init):
    lq1, lk1, lq2, lk2 = lam_ref[0:1, :], lam_ref[1:2, :], lam_ref[2:3, :], lam_ref[3:4, :]
    lam = (jnp.exp(jnp.sum(lq1 * lk1, axis=-1, keepdims=True))
           - jnp.exp(jnp.sum(lq2 * lk2, axis=-1, keepdims=True)) + lam_init)
    q = q_ref[...]
    tq, gw = q.shape
    lane = lax.broadcasted_iota(jnp.int32, (tq, gw), 1)
    origin = chunks[0][0]
    zero = jnp.zeros_like(q)

    def head(h, out, slot):
        qm = [jnp.where(lane // DIFF_DIM == 2 * h + mp, q, zero) for mp in range(2)]
        m = [jnp.full((tq, 128), NEG, F32) for _ in range(2)]
        for st, sz in chunks:
            k = k_ref[st:st + sz, :]
            for mp in range(2):
                s = _dot_nt(qm[mp], k)
                s_ref[(slot + mp) * tq:(slot + mp + 1) * tq, st - origin:st - origin + sz] = s
                for t in range(sz // 128):
                    m[mp] = jnp.maximum(m[mp], s[:, t * 128:(t + 1) * 128])
        m = [jnp.max(x, axis=-1, keepdims=True) for x in m]
        ones_lane = ((h + 1) % GROUP_HEADS) * HEAD_DIM
        acc = [jnp.zeros((tq, gw), F32) for _ in range(2)]
        for st, sz in chunks:
            v = v_ref[st:st + sz, :]
            lane_v = lax.broadcasted_iota(jnp.int32, v.shape, 1)
            v = jnp.where(lane_v == ones_lane, jnp.ones_like(v), v)
            for mp in range(2):
                e = jnp.exp((s_ref[(slot + mp) * tq:(slot + mp + 1) * tq, st - origin:st - origin + sz] - m[mp]).astype(BF16))
                acc[mp] = acc[mp] + _dot(e, v)
        w = []
        for mp in range(2):
            l = jnp.sum(jnp.where(lane == ones_lane, acc[mp], 0.0), axis=-1, keepdims=True)
            w.append(acc[mp] * ((1.0 if mp == 0 else lam) / l))
        return out + jnp.where(lane // HEAD_DIM == h, w[0] - w[1], 0.0)

    def head_pair(i, out):
        return head(2 * i + 1, head(2 * i, out, 0), 2)

    o = lax.fori_loop(0, GROUP_HEADS // 2, head_pair, jnp.zeros((tq, gw), F32))
    sq = o * o
    scale = jnp.zeros((tq, gw), F32)
    for h in range(GROUP_HEADS):
        in_head = lane // HEAD_DIM == h
        ms = jnp.sum(jnp.where(in_head, sq, 0.0), axis=-1, keepdims=True) * (1.0 / HEAD_DIM)
        scale = jnp.where(in_head, lax.rsqrt(ms + RMS_EPS), scale)
    o_ref[...] = (o * scale * g_ref[...] * (1.0 - lam_init)).astype(BF16)


def _df_call(pbf, df_lambda, g_tiled, lam_init, n_batch, n_tok, first_tile, n_tiles, chunks, name):
    n_keys = sum(sz for _, sz in chunks)
    return pl.pallas_call(
        functools.partial(_df_kernel, chunks=chunks, lam_init=lam_init),
        grid=(n_batch, n_tiles),
        in_specs=[
            pl.BlockSpec((4, DIFF_DIM), lambda b, i: (0, 0)),
            pl.BlockSpec((None, ROW_TILE, GROUP_W), lambda b, i: (b, first_tile + i, 3)),
            pl.BlockSpec((None, n_tok, GROUP_W), lambda b, i: (b, 0, 4)),
            pl.BlockSpec((None, n_tok, GROUP_W), lambda b, i: (b, 0, 5)),
            pl.BlockSpec((1, GROUP_W), lambda b, i: (0, 0)),
        ],
        out_specs=pl.BlockSpec((None, ROW_TILE, GROUP_W), lambda b, i: (b, i, 0)),
        out_shape=jax.ShapeDtypeStruct((n_batch, n_tiles * ROW_TILE, GROUP_W), BF16),
        scratch_shapes=[pltpu.VMEM((4 * ROW_TILE, n_keys), F32)],
        compiler_params=_cparams("parallel", "arbitrary"),
        name=name,
    )(df_lambda, pbf, pbf, pbf, g_tiled)


def _df_attention(pbf, df_lambda, g_tiled, lam_init, n_batch, n_lat, n_ctx):
    n_tok = n_lat + n_ctx
    lat_chunks = tuple((st, min(DF_KV, n_tok - st)) for st in range(0, n_tok, DF_KV))
    o_lat = _df_call(pbf, df_lambda, g_tiled, lam_init, n_batch, n_tok, 0, n_lat // ROW_TILE, lat_chunks, "df_attention")
    o_ctx = _df_call(pbf, df_lambda, g_tiled, lam_init, n_batch, n_tok, n_lat // ROW_TILE, n_ctx // ROW_TILE,
                     ((n_lat, n_ctx),), "df_attention_ctx")
    return jnp.concatenate([o_lat, o_ctx], axis=1)


CONV_PAD = 8


def _dn_prep_kernel(u_ref, w_ref, o_ref, pad_ref, *, n_lat, n_ctx):
    j = pl.program_id(1)
    zeros = jnp.zeros((CONV_PAD, GROUP_W), F32)
    lat0 = CONV_PAD
    ctx0 = 2 * CONV_PAD + n_lat
    pad_ref[0:CONV_PAD, :] = zeros
    pad_ref[lat0 + n_lat:ctx0, :] = zeros
    pad_ref[ctx0 + n_ctx:ctx0 + n_ctx + CONV_PAD, :] = zeros
    pad_ref[lat0:lat0 + n_lat, :] = u_ref[0:n_lat, :]
    pad_ref[ctx0:ctx0 + n_ctx, :] = u_ref[n_lat:n_lat + n_ctx, :]
    qscale = jnp.where(j == 0, HEAD_DIM ** -0.5, 1.0)
    half = DN_CONV_K // 2
    for (src0, dst0, n) in ((lat0, 0, n_lat), (ctx0, n_lat, n_ctx)):
        for t0 in range(0, n, ROW_TILE):
            acc = jnp.zeros((ROW_TILE, GROUP_W), F32)
            for tap in range(DN_CONV_K):
                a = src0 + t0 + tap - half
                acc = acc + pad_ref[a:a + ROW_TILE, :] * w_ref[tap:tap + 1, :]
            y = _silu(acc)
            parts = []
            for h in range(GROUP_HEADS):
                yh = y[:, h * HEAD_DIM:(h + 1) * HEAD_DIM]
                nrm = lax.rsqrt(jnp.sum(yh * yh, axis=-1, keepdims=True) + 1e-6) * qscale
                parts.append(yh * jnp.where(j == 2, 1.0, nrm))
            o_ref[dst0 + t0:dst0 + t0 + ROW_TILE, :] = jnp.concatenate(parts, axis=-1)


def _dn_prep(pf, conv_w, n_batch, n_lat, n_ctx):
    n_tok = n_lat + n_ctx
    return pl.pallas_call(
        functools.partial(_dn_prep_kernel, n_lat=n_lat, n_ctx=n_ctx),
        grid=(n_batch, 3),
        in_specs=[
            pl.BlockSpec((None, n_tok, GROUP_W), lambda b, j: (b, 0, j)),
            pl.BlockSpec((DN_CONV_K, GROUP_W), lambda b, j: (0, j)),
        ],
        out_specs=pl.BlockSpec((None, n_tok, GROUP_W), lambda b, j: (b, 0, j)),
        out_shape=jax.ShapeDtypeStruct((n_batch, n_tok, 3 * GROUP_W), F32),
        scratch_shapes=[pltpu.VMEM((n_tok + 3 * CONV_PAD, GROUP_W), F32)],
        compiler_params=_cparams("parallel", "arbitrary"),
        name="dn_prep",
    )(pf, conv_w)


DN_BLOCK = 4 * DN_CHUNK
DN_SAMPLES = 2


def _split3(x):
    x1 = x.astype(BF16)
    r = x - x1.astype(F32)
    x2 = r.astype(BF16)
    return x1, x2, (r - x2.astype(F32)).astype(BF16)


def _block_diag(x, masks):
    zero = jnp.zeros_like(x)
    return jnp.concatenate([jnp.where(m, x, zero) for m in masks], axis=0)


def _mm3_heads(lhs, rhs, masks):
    lh, ll = _split(lhs)
    rh, rl = _split(rhs)
    bh, bl = _block_diag(rh, masks), _block_diag(rl, masks)
    return _dot(jnp.concatenate([lh, ll, lh], axis=1), jnp.concatenate([bh, bh, bl], axis=0))


def _dn_kernel(qf_ref, kf_ref, vf_ref, abf_ref, qb_ref, kb_ref, vb_ref, abb_ref, av_ref, dt_ref, exp_ref,
               of_ref, ob_ref, s_ref):
    @pl.when(pl.program_id(1) == 0)
    def _():
        s_ref[...] = jnp.zeros_like(s_ref)

    c, nb, gw = DN_CHUNK, DN_BLOCK, GROUP_W
    n_sub = nb // c
    lane4 = lax.broadcasted_iota(jnp.int32, (c, gw), 1)
    masks = [(lane4 // HEAD_DIM) == h for h in range(GROUP_HEADS)]
    ri = lax.broadcasted_iota(jnp.int32, (c, gw), 0)
    cj = lane4 % HEAD_DIM
    eye = (ri == cj).astype(F32)
    bi = lax.broadcasted_iota(jnp.int32, (nb, nb), 0)
    bj = lax.broadcasted_iota(jnp.int32, (nb, nb), 1)
    same_chunk = (bi // c) == (bj // c)
    lane1 = lax.broadcasted_iota(jnp.int32, (nb, 128), 1)
    n_samples = qf_ref.shape[0]
    streams = []
    for bb in range(n_samples):
        streams.append((0, qf_ref.at[bb], kf_ref.at[bb], vf_ref.at[bb], abf_ref.at[bb]))
        streams.append((1, qb_ref.at[bb], kb_ref.at[bb], vb_ref.at[bb], abb_ref.at[bb]))

    chains = []
    for sidx, (d, q_ref, k_ref, v_ref, ab_ref) in enumerate(streams):
        q_all, k_all, v_all = q_ref[...], k_ref[...], v_ref[...]
        for j in range(n_sub):
            rows = slice(j * c, (j + 1) * c)
            q, k = q_all[rows], k_all[rows]
            kq = _dot_nt(jnp.concatenate([k, q], axis=0).astype(BF16), _block_diag(k.astype(BF16), masks))
            chains.append(dict(s=sidx, d=d, rows=rows, q=q, k=k, v=v_all[rows], kq=kq))
    gates = []
    for d, q_ref, k_ref, v_ref, ab_ref in streams:
        ab = ab_ref[...]
        x = ab + dt_ref[...]
        softplus = jnp.maximum(x, 0.0) + jnp.log1p(jnp.exp(-jnp.abs(x)))
        g = -jnp.exp(av_ref[...]) * softplus
        tri = (same_chunk & ((bi >= bj) if d == 0 else (bi <= bj))).astype(BF16)
        g1, g2, g3 = _split3(g)
        gcs = _dot(tri, g1) + (_dot(tri, g2) + _dot(tri, g3))
        gates.append(jnp.where(lane1 < 2 * GROUP_HEADS, gcs, jax.nn.sigmoid(ab)))
    spread = []
    for sidx, stream in enumerate(streams):
        x1, x2, x3 = _split3(gates[sidx])
        e = exp_ref[stream[0]]
        spread.append(_dot(x1, e) + (_dot(x2, e) + _dot(x3, e)))
    for ch in chains:
        d, q, k, v, kq = ch["d"], ch["q"], ch["k"], ch["v"], ch["kq"]
        incl = (ri >= cj) if d == 0 else (ri <= cj)
        strict = (ri > cj) if d == 0 else (ri < cj)
        last = c - 1 if d == 0 else 0
        gc, beta = spread[ch["s"]][ch["rows"], :gw], spread[ch["s"]][ch["rows"], gw:]
        gc_row = jnp.sum(gc * eye, axis=0, keepdims=True)
        decay = jnp.exp(jnp.where(incl, gc - gc_row, NEG))
        eg = jnp.exp(gc)
        g_last = gc[last:last + 1, :]
        ch.update(
            lm=jnp.where(strict, kq[:c] * beta * decay, 0.0),
            qk=jnp.where(incl, kq[c:] * decay, 0.0).astype(BF16),
            rhs=jnp.concatenate([_block_diag((v * beta).astype(BF16), masks),
                                 _block_diag((k * beta * eg).astype(BF16), masks)], axis=1),
            q_dec=q * eg,
            k_dec=(k * jnp.exp(g_last - gc)).astype(BF16),
            g_end=jnp.exp(g_last))

    def neumann_level(first):
        for ch in chains:
            if first:
                ch["p"] = eye - ch["lm"]
                ch["sq"] = _mm3_heads(ch["lm"], ch["lm"], masks)
            else:
                r = _mm3_heads(jnp.concatenate([ch["sq"], ch["p"]], axis=0), ch["sq"], masks)
                ch["sq"], ch["p"] = r[:c], ch["p"] + r[c:]

    def head_blocks(full):
        out = jnp.where(masks[0], full[0:c], 0.0)
        for h in range(1, GROUP_HEADS):
            out = out + jnp.where(masks[h], full[h * c:(h + 1) * c], 0.0)
        return out

    n_levels = int(math.log2(c)) - 1
    for level in range(n_levels):
        neumann_level(level == 0)
    t_inv = [ch["p"] + _mm3_heads(ch["p"], ch["sq"], masks) for ch in chains]
    uw = [_dot(t.astype(BF16), ch["rhs"]).astype(BF16) for t, ch in zip(t_inv, chains)]
    kd_uw = [_dot_tn(ch["k_dec"], x) for x, ch in zip(uw, chains)]
    qk_uw = [_dot(ch["qk"], jnp.concatenate([_block_diag(x[:, :gw], masks), _block_diag(x[:, gw:], masks)], axis=1))
             for x, ch in zip(uw, chains)]
    solved = [dict(ms=jnp.concatenate([head_blocks(kd[:, gw:]), ch["q_dec"] - qq[:, gw:]], axis=0).astype(BF16),
                   b=head_blocks(kd[:, :gw]), d=qq[:, :gw], g_end=ch["g_end"])
              for kd, qq, ch in zip(kd_uw, qk_uw, chains)]

    states = [s_ref[sidx] for sidx in range(len(streams))]
    outs = [[None] * n_sub for _ in streams]
    for step in range(n_sub):
        for sidx, stream in enumerate(streams):
            j = step if stream[0] == 0 else n_sub - 1 - step
            ch = solved[sidx * n_sub + j]
            s = states[sidx]
            res = _dot(ch["ms"], _block_diag(s.astype(BF16), masks))
            outs[sidx][j] = res[c:] + ch["d"]
            states[sidx] = s * ch["g_end"] - res[:c] + ch["b"]
    of_ref[...] = jnp.stack([jnp.concatenate(outs[2 * bb], axis=0) for bb in range(n_samples)], axis=0)
    ob_ref[...] = jnp.stack([jnp.concatenate(outs[2 * bb + 1], axis=0) for bb in range(n_samples)], axis=0)
    s_ref[...] = jnp.stack(states, axis=0)


def _dn_scan(qkv, pf, a_vec, dt_vec, n_batch, n_lat, n_ctx):
    n_tok = n_lat + n_ctx
    assert n_ctx == DN_BLOCK and n_lat % DN_BLOCK == 0
    nblk = n_tok // DN_BLOCK
    ab_blk = 4 * GROUP_W // 128

    def fwd(n):
        return jnp.where(n == 0, nblk - 1, n - 1)

    def bwd(n):
        return jnp.where(n == 0, nblk - 1, nblk - 1 - n)

    assert n_batch % DN_SAMPLES == 0

    def spec(order, blk, width):
        return pl.BlockSpec((DN_SAMPLES, DN_BLOCK, width), lambda b, n: (b, order(n), blk))

    spread = np.zeros((2, 128, 2 * GROUP_W), np.float32)
    for d in range(2):
        for h in range(GROUP_HEADS):
            spread[d, GROUP_HEADS * d + h, h * HEAD_DIM:(h + 1) * HEAD_DIM] = 1.0
            spread[d, 2 * GROUP_HEADS + GROUP_HEADS * d + h, GROUP_W + h * HEAD_DIM:GROUP_W + (h + 1) * HEAD_DIM] = 1.0

    lane = pl.BlockSpec((1, 128), lambda b, n: (0, 0))
    out = jax.ShapeDtypeStruct((n_batch, n_tok, GROUP_W), F32)
    return pl.pallas_call(
        _dn_kernel,
        grid=(n_batch // DN_SAMPLES, nblk),
        in_specs=[spec(fwd, 0, GROUP_W), spec(fwd, 1, GROUP_W), spec(fwd, 2, GROUP_W), spec(fwd, ab_blk, 128),
                  spec(bwd, 0, GROUP_W), spec(bwd, 1, GROUP_W), spec(bwd, 2, GROUP_W), spec(bwd, ab_blk, 128),
                  lane, lane, pl.BlockSpec((2, 128, 2 * GROUP_W), lambda b, n: (0, 0, 0))],
        out_specs=[spec(fwd, 0, GROUP_W), spec(bwd, 0, GROUP_W)],
        out_shape=[out, out],
        scratch_shapes=[pltpu.VMEM((2 * DN_SAMPLES, HEAD_DIM, GROUP_W), F32)],
        compiler_params=_cparams("parallel", "arbitrary"),
        name="dn_scan",
    )(qkv, qkv, qkv, pf, qkv, qkv, qkv, pf, a_vec, dt_vec, jnp.asarray(spread, BF16))


FT_N2 = 64


def _ft1_kernel(x_ref, fh_ref, fl_ref, o_ref):
    o_ref[...] = _dot3_const(fh_ref[...], fl_ref[...], x_ref[...])


def _ft_stage1(u_rows, f_hi, f_lo, n_batch, n1, tn=2048):
    width = u_rows.shape[2]
    return pl.pallas_call(
        _ft1_kernel,
        grid=(n_batch, width // tn),
        in_specs=[
            pl.BlockSpec((None, n1, tn), lambda b, j: (b, 0, j)),
            pl.BlockSpec((2 * n1, n1), lambda b, j: (0, 0)),
            pl.BlockSpec((2 * n1, n1), lambda b, j: (0, 0)),
        ],
        out_specs=pl.BlockSpec((None, 2 * n1, tn), lambda b, j: (b, 0, j)),
        out_shape=jax.ShapeDtypeStruct((n_batch, 2 * n1, width), F32),
        compiler_params=_cparams("parallel", "parallel"),
        name="ft_stage1",
    )(u_rows, f_hi, f_lo)


def _ft3_kernel(y_ref, gh_ref, gl_ref, ch_ref, cl_ref, sh_ref, sl_ref, w_ref, o_ref, *, pb, n2, norm):
    vr, vi = [], []
    for p in range(pb):
        rhs = jnp.concatenate([y_ref[0, p], y_ref[1, p]], axis=0)
        v = _dot3_const(gh_ref[p], gl_ref[p], rhs)
        vr.append(v[:n2])
        vi.append(v[n2:])
    vr = jnp.concatenate(vr, axis=0)
    vi = jnp.concatenate(vi, axis=0)
    vrh, vrl = _split(vr)
    vih, vil = _split(vi)
    y = (_dot(vrh, ch_ref[...]) + (_dot(vrl, ch_ref[...]) + _dot(vrh, cl_ref[...]))
         + _dot(vih, sh_ref[...]) + (_dot(vil, sh_ref[...]) + _dot(vih, sl_ref[...]))) * norm
    o = _dot(y.astype(BF16), w_ref[...])
    for p in range(pb):
        o_ref[:, p, :] = o[p * n2:(p + 1) * n2, :]


def _ft_stage2(y1, g_hi, g_lo, chan, ft_w, n_batch, n1, n2, pb):
    ch, cl, sh, sl = chan
    full = lambda b, i: (0, 0)
    return pl.pallas_call(
        functools.partial(_ft3_kernel, pb=pb, n2=n2, norm=1.0 / math.sqrt(n1 * n2 * FT_DIM)),
        grid=(n_batch, n1 // pb),
        in_specs=[
            pl.BlockSpec((None, 2, pb, n2, GROUP_W), lambda b, i: (b, 0, i, 0, 0)),
            pl.BlockSpec((pb, 2 * n2, 2 * n2), lambda b, i: (i, 0, 0)),
            pl.BlockSpec((pb, 2 * n2, 2 * n2), lambda b, i: (i, 0, 0)),
            pl.BlockSpec((GROUP_W, GROUP_W), full), pl.BlockSpec((GROUP_W, GROUP_W), full),
            pl.BlockSpec((GROUP_W, GROUP_W), full), pl.BlockSpec((GROUP_W, GROUP_W), full),
            pl.BlockSpec((GROUP_W, GROUP_W), full),
        ],
        out_specs=pl.BlockSpec((None, n2, pb, GROUP_W), lambda b, i: (b, 0, i, 0)),
        out_shape=jax.ShapeDtypeStruct((n_batch, n2, n1, GROUP_W), F32),
        compiler_params=_cparams("parallel", "parallel"),
        name="ft_stage2",
    )(y1, g_hi, g_lo, ch, cl, sh, sl, ft_w)


def _np_split(a):
    hi = jnp.asarray(a, F32).astype(BF16)
    lo = (jnp.asarray(a, F32) - hi.astype(F32)).astype(BF16)
    return hi, lo


@functools.lru_cache(maxsize=None)
def _ft_tables(n1, n2):
    n = n1 * n2
    p = np.arange(n1)
    ang1 = 2.0 * np.pi * ((p[:, None] * p[None, :]) % n1) / n1
    f1 = np.concatenate([np.cos(ang1), -np.sin(ang1)], axis=0)
    q = np.arange(n2)
    b = np.arange(n2)
    phase = (b[None, None, :] * (p[:, None, None] + n1 * q[None, :, None])) % n
    psi = 2.0 * np.pi * phase / n
    gc, gs = np.cos(psi), np.sin(psi)
    g = np.concatenate([np.concatenate([gc, gs], axis=2), np.concatenate([-gs, gc], axis=2)], axis=1)
    c = np.arange(FT_DIM)
    angc = 2.0 * np.pi * ((c[:, None] * c[None, :]) % FT_DIM) / FT_DIM
    eye = np.eye(GROUP_W // FT_DIM)
    cc, sc = np.kron(eye, np.cos(angc)), np.kron(eye, np.sin(angc))
    return f1.astype(np.float32), g.astype(np.float32), cc.astype(np.float32), sc.astype(np.float32)


def _out_kernel(x_ref, m_ref, na_ref, of_ref, ob_ref, gate_ref, dng_ref, df_ref, ft_ref, w_ref, g2_ref, wr_ref,
                xo_ref, h_ref, aff_ref):
    o = of_ref[...] + ob_ref[...]
    parts = []
    for h in range(GROUP_HEADS):
        oh = o[:, h * HEAD_DIM:(h + 1) * HEAD_DIM]
        parts.append(oh * lax.rsqrt(jnp.mean(oh * oh, axis=-1, keepdims=True) + RMS_EPS))
    y_dn = jnp.concatenate(parts, axis=-1) * dng_ref[...] * _silu(gate_ref[...])
    mix = jnp.concatenate([na_ref[...], y_dn.astype(BF16), df_ref[...], ft_ref[...].astype(BF16)], axis=-1)
    x = x_ref[...] + m_ref[2] * _dot(mix, w_ref[...])
    xo_ref[...] = x
    h2 = _norm_mod(x, g2_ref[...], m_ref[3], m_ref[4])
    bits = pltpu.bitcast(h2.astype(BF16).astype(F32), jnp.uint32)
    half = D_MODEL // 2
    h_ref[...] = bits[:, half:] | (bits[:, :half] >> 16)
    logits = _dot3(h2, wr_ref[...])
    lane = lax.broadcasted_iota(jnp.int32, logits.shape, 1)
    logits = jnp.where(lane < N_EXPERTS, logits, NEG)
    e = jnp.exp(logits - jnp.max(logits, axis=-1, keepdims=True))
    aff_ref[...] = e / jnp.sum(e, axis=-1, keepdims=True)


def _out_proj(x, mods, o_na, o_f, o_b, pf, dn_g, o_df, o_ft, w_out, g2, w_router, n_batch, n_tok):
    nt = n_tok // ROW_TILE
    ctx_tile = nt - 1
    row = lambda b, i: (b, i, 0)
    blk = lambda: pl.BlockSpec((None, ROW_TILE, GROUP_W), row)
    vec = lambda n: pl.BlockSpec((1, n), lambda b, i: (0, 0))
    return pl.pallas_call(
        _out_kernel,
        grid=(n_batch, nt),
        in_specs=[
            pl.BlockSpec((None, ROW_TILE, D_MODEL), row),
            pl.BlockSpec((None, 6, 1, D_MODEL), lambda b, i: (jnp.where(i == ctx_tile, n_batch, b), 0, 0, 0)),
            blk(), blk(), blk(),
            pl.BlockSpec((None, ROW_TILE, GROUP_W), lambda b, i: (b, i, 3)),
            vec(GROUP_W), blk(), blk(),
            pl.BlockSpec((D_MODEL, D_MODEL), lambda b, i: (0, 0)),
            vec(D_MODEL),
            pl.BlockSpec((D_MODEL, 128), lambda b, i: (0, 0)),
        ],
        out_specs=[
            pl.BlockSpec((None, ROW_TILE, D_MODEL), row),
            pl.BlockSpec((None, ROW_TILE, D_MODEL // 2), row),
            pl.BlockSpec((None, ROW_TILE, 128), row),
        ],
        out_shape=[
            jax.ShapeDtypeStruct((n_batch, n_tok, D_MODEL), F32),
            jax.ShapeDtypeStruct((n_batch, n_tok, D_MODEL // 2), jnp.uint32),
            jax.ShapeDtypeStruct((n_batch, n_tok, 128), F32),
        ],
        compiler_params=_cparams("parallel", "parallel"),
        name="out_proj_router",
    )(x, mods, o_na, o_f, o_b, pf, dn_g, o_df, o_ft, w_out, g2.reshape(1, D_MODEL), w_router)


MOE_TF = 256
GATHER_UNROLL = 8


def _gather_kernel(idx_ref, h_ref, o_ref, *, cap, n_e):
    base = (pl.program_id(0) * n_e + pl.program_id(1)) * cap

    def body(j, carry):
        for u in range(GATHER_UNROLL):
            o_ref[j, u:u + 1, :] = h_ref[pl.ds(idx_ref[base + j * GATHER_UNROLL + u], 1), :]
        return carry

    lax.fori_loop(0, cap // GATHER_UNROLL, body, 0)


def _moe_gather(hp, idx):
    n_batch, n_tok, width = hp.shape
    _, n_e, cap = idx.shape
    assert cap % GATHER_UNROLL == 0 and GATHER_UNROLL == 8
    groups = cap // GATHER_UNROLL
    out = pl.pallas_call(
        functools.partial(_gather_kernel, cap=cap, n_e=n_e),
        grid_spec=pltpu.PrefetchScalarGridSpec(
            num_scalar_prefetch=1,
            grid=(n_batch, n_e),
            in_specs=[pl.BlockSpec((None, n_tok, width), lambda b, e, idx_ref: (b, 0, 0))],
            out_specs=pl.BlockSpec((None, groups, GATHER_UNROLL, width), lambda b, e, idx_ref: (e, b, 0, 0)),
        ),
        out_shape=jax.ShapeDtypeStruct((n_e, n_batch * groups, GATHER_UNROLL, width), jnp.uint32),
        compiler_params=_cparams("parallel", "arbitrary"),
        name="moe_gather",
    )(idx.reshape(-1), hp)
    return out.reshape(n_e, n_batch * cap, width)


def _combine_kernel(idx_ref, y_ref, o_ref, *, cap, n_e):
    e = pl.program_id(1)

    @pl.when(e == 0)
    def _():
        o_ref[...] = jnp.zeros_like(o_ref)

    base = (pl.program_id(0) * n_e + e) * cap

    def body(j, carry):
        first = j * GATHER_UNROLL
        toks = [idx_ref[base + first + u] for u in range(GATHER_UNROLL)]
        rows = [o_ref[pl.ds(t, 1), :] + y_ref[j, u:u + 1, :] for u, t in enumerate(toks)]
        for t, row in zip(toks, rows):
            o_ref[pl.ds(t, 1), :] = row
        return carry

    lax.fori_loop(0, cap // GATHER_UNROLL, body, 0)


def _moe_combine(y, idx, n_tok):
    n_batch, n_e, cap = idx.shape
    d = y.shape[2]
    groups = cap // GATHER_UNROLL
    return pl.pallas_call(
        functools.partial(_combine_kernel, cap=cap, n_e=n_e),
        grid_spec=pltpu.PrefetchScalarGridSpec(
            num_scalar_prefetch=1,
            grid=(n_batch, n_e),
            in_specs=[pl.BlockSpec((None, groups, GATHER_UNROLL, d), lambda b, e, idx_ref: (e, b, 0, 0))],
            out_specs=pl.BlockSpec((None, n_tok, d), lambda b, e, idx_ref: (b, 0, 0)),
        ),
        out_shape=jax.ShapeDtypeStruct((n_batch, n_tok, d), F32),
        compiler_params=_cparams("parallel", "arbitrary"),
        name="moe_combine",
    )(idx.reshape(-1), y.reshape(n_e, n_batch * groups, GATHER_UNROLL, d))


def _moe_kernel(x_ref, wg_ref, wu_ref, wd_ref, gate_ref, o_ref, xb_ref):
    f = pl.program_id(1)

    @pl.when(f == 0)
    def _():
        packed = x_ref[...]
        xb_ref[...] = jnp.concatenate([pltpu.bitcast(packed << 16, F32),
                                       pltpu.bitcast(packed & jnp.uint32(0xFFFF0000), F32)], axis=1).astype(BF16)
        o_ref[...] = jnp.zeros_like(o_ref)

    x = xb_ref[...]
    a = _dot(x, wg_ref[...].astype(BF16))
    u = _dot(x, wu_ref[...].astype(BF16))
    hid = (_silu(a) * u).astype(BF16)
    o_ref[...] += _dot(hid, wd_ref[...].astype(BF16))

    @pl.when(f == pl.num_programs(1) - 1)
    def _():
        o_ref[...] = o_ref[...] * gate_ref[...]


def _moe_ffn(xg, w_gate, w_up, w_down, gate, layer):
    n_e, m, _ = xg.shape
    return pl.pallas_call(
        _moe_kernel,
        grid=(n_e, D_EXPERT // MOE_TF),
        in_specs=[
            pl.BlockSpec((None, m, D_MODEL // 2), lambda e, f: (e, 0, 0)),
            pl.BlockSpec((None, None, D_MODEL, MOE_TF), lambda e, f: (layer, e, 0, f)),
            pl.BlockSpec((None, None, D_MODEL, MOE_TF), lambda e, f: (layer, e, 0, f)),
            pl.BlockSpec((None, None, MOE_TF, D_MODEL), lambda e, f: (layer, e, f, 0)),
            pl.BlockSpec((None, m, 1), lambda e, f: (e, 0, 0)),
        ],
        out_specs=pl.BlockSpec((None, m, D_MODEL), lambda e, f: (e, 0, 0)),
        out_shape=jax.ShapeDtypeStruct((n_e, m, D_MODEL), F32),
        scratch_shapes=[pltpu.VMEM((m, D_MODEL), BF16)],
        compiler_params=_cparams("parallel", "arbitrary"),
        name="moe_ffn",
    )(xg, w_gate, w_up, w_down, gate)


def _final_kernel(x_ref, moe_ref, m_ref, g_ref, o_ref):
    x = x_ref[...] + m_ref[5] * moe_ref[...]
    o_ref[...] = x * lax.rsqrt(jnp.mean(x * x, axis=-1, keepdims=True) + RMS_EPS) * g_ref[...]


def _final_norm(x, moe, mods, g, n_batch, n_lat):
    row = lambda b, i: (b, i, 0)
    blk = pl.BlockSpec((None, ROW_TILE, D_MODEL), row)
    return pl.pallas_call(
        _final_kernel,
        grid=(n_batch, n_lat // ROW_TILE),
        in_specs=[blk, blk, pl.BlockSpec((None, 6, 1, D_MODEL), lambda b, i: (b, 0, 0, 0)),
                  pl.BlockSpec((1, D_MODEL), lambda b, i: (0, 0))],
        out_specs=blk,
        out_shape=jax.ShapeDtypeStruct((n_batch, n_lat, D_MODEL), F32),
        compiler_params=_cparams("parallel", "parallel"),
        name="final_norm",
    )(x, moe, mods, g.reshape(1, D_MODEL))


def _reorder_in_columns(w):
    gw = GROUP_W
    o = 6 * gw + 4 * GROUP_HEADS
    pad = jnp.zeros((w.shape[0], 128 - 4 * GROUP_HEADS), w.dtype)
    parts = [w[:, 0:3 * gw], w[:, o + gw:o + 4 * gw], w[:, 3 * gw:6 * gw], w[:, o:o + gw],
             w[:, o + 4 * gw:o + 5 * gw], w[:, 6 * gw:o], pad]
    return jnp.concatenate(parts, axis=1).astype(BF16)


@functools.lru_cache(maxsize=None)
def _rope_tables(n_lat, n_ctx):
    t = np.arange(n_lat)
    pos = np.stack([t // GRID_W, t % GRID_W], axis=-1).astype(np.float32)
    n_freq = DIFF_DIM // 4
    inv = (ROPE_BASE ** (-np.arange(n_freq, dtype=np.float32) / n_freq)).astype(np.float32)
    ang = (pos[:, :, None] * inv).reshape(n_lat, 2 * n_freq)
    lane = np.arange(GROUP_W)
    idx = (lane % DIFF_DIM) // 2
    sign = np.where(lane % 2 == 0, -1.0, 1.0)
    cos = np.concatenate([np.cos(ang)[:, idx], np.ones((n_ctx, GROUP_W))], axis=0)
    sin = np.concatenate([np.sin(ang)[:, idx] * sign, np.zeros((n_ctx, GROUP_W))], axis=0)
    return cos.astype(np.float32), sin.astype(np.float32)


def _lane_vec(v, n=128):
    v = v.reshape(-1).astype(F32)
    return jnp.zeros((1, n), F32).at[0, :v.shape[0]].set(v)


def _route(aff, cap):
    gate, idx = lax.top_k(jnp.swapaxes(aff, 1, 2), cap)
    return gate, idx


def kernel(x, c, ctx, c_ctx, w_mod, b_mod, norm1_g, w_in, na_rpb, dn_conv_w, dn_a_log, dn_dt_bias, dn_norm_g, df_lambda, df_norm_g, ft_w, w_out, norm2_g, w_router, w_gate, w_up, w_down, final_norm_g):
    n_batch, n_lat, _ = x.shape
    n_ctx = ctx.shape[1]
    n_tok = n_lat + n_ctx
    depth = w_mod.shape[0]
    assert n_batch + 1 <= 8 and n_ctx == ROW_TILE and n_lat % ROW_TILE == 0

    xs = jnp.concatenate([x, ctx], axis=1)
    cc = jnp.zeros((8, D_MODEL), F32).at[:n_batch].set(c).at[n_batch].set(c_ctx)
    m_all = _modulation(cc, w_mod, b_mod)

    cos_t, sin_t = (jnp.asarray(a) for a in _rope_tables(n_lat, n_ctx))
    n1 = n_lat // FT_N2
    f1, g_lat, cc_m, sc_m = _ft_tables(n1, FT_N2)
    _, g_ctx, _, _ = _ft_tables(1, n_ctx)
    f1h, f1l = _np_split(f1)
    glh, gll = _np_split(g_lat)
    gch, gcl = _np_split(g_ctx)
    chan = _np_split(cc_m) + _np_split(sc_m)
    cap_lat = EC_FACTOR * n_lat // N_EXPERTS
    cap_ctx = EC_FACTOR * n_ctx // N_EXPERTS

    prev = None
    for l in range(depth):
        lam_init = 0.8 - 0.6 * math.exp(-0.3 * l)
        mods = m_all[l, :n_batch + 1].reshape(n_batch + 1, 6, 1, D_MODEL)
        xs, pbf, pf, pu = _in_proj(xs, prev, norm1_g[l], mods, _reorder_in_columns(w_in[l]), cos_t, sin_t, n_batch, n_tok)

        o_na = _na_attention(pbf, _na_bias_table(na_rpb[l]), n_batch, n_lat, n_ctx)

        qkv = _dn_prep(pf, dn_conv_w[l], n_batch, n_lat, n_ctx)
        o_f, o_b = _dn_scan(qkv, pf, _lane_vec(dn_a_log[l]), _lane_vec(dn_dt_bias[l]), n_batch, n_lat, n_ctx)

        o_df = _df_attention(pbf, df_lambda[l], jnp.tile(df_norm_g[l], GROUP_HEADS).reshape(1, GROUP_W),
                             lam_init, n_batch, n_lat, n_ctx)

        ftw = ft_w[l].astype(BF16)
        y1 = _ft_stage1(pu.reshape(n_batch, n_tok // FT_N2, FT_N2 * GROUP_W), f1h, f1l, n_batch, n1)
        o_lat = _ft_stage2(y1.reshape(n_batch, 2, n1, FT_N2, GROUP_W), glh, gll, chan, ftw, n_batch, n1, FT_N2, 8)
        o_lat = o_lat.reshape(n_batch, n_lat, GROUP_W)
        u_ctx = pu[:, n_lat:]
        y1c = jnp.stack([u_ctx, jnp.zeros_like(u_ctx)], axis=1)[:, :, None]
        o_ctx = _ft_stage2(y1c, gch, gcl, chan, ftw, n_batch, 1, n_ctx, 1).reshape(n_batch, n_ctx, GROUP_W)
        o_ft = jnp.concatenate([o_lat, o_ctx], axis=1)

        wr = jnp.pad(w_router[l], ((0, 0), (0, 128 - N_EXPERTS)))
        xs, h2, aff = _out_proj(xs, mods, o_na, o_f, o_b, pf, jnp.tile(dn_norm_g[l], GROUP_HEADS).reshape(1, GROUP_W),
                                o_df, o_ft, w_out[l].astype(BF16), norm2_g[l], wr, n_batch, n_tok)

        gate_l, idx_l = _route(aff[:, :n_lat, :N_EXPERTS], cap_lat)
        gate_c, idx_c = _route(aff[:, n_lat:, :N_EXPERTS], cap_ctx)
        idx = jnp.concatenate([idx_l, idx_c + n_lat], axis=2)
        gate = jnp.concatenate([gate_l, gate_c], axis=2)
        cap = cap_lat + cap_ctx
        gate_col = jnp.swapaxes(gate, 0, 1).reshape(N_EXPERTS, n_batch * cap, 1)
        y = _moe_ffn(_moe_gather(h2, idx), w_gate, w_up, w_down, gate_col, l)
        prev = (_moe_combine(y, idx, n_tok), mods)

    return _final_norm(xs, prev[0], prev[1], final_norm_g, n_batch, n_lat)
```

```python
import functools
import math

import numpy as np
import jax
import jax.numpy as jnp
from jax import lax
from jax.experimental import pallas as pl
from jax.experimental.pallas import tpu as pltpu

F32 = jnp.float32
BF16 = jnp.bfloat16

D_MODEL = 1024
DEPTH = 4
GRID_W = 64
GROUP_W = 256
GROUP_HEADS = 4
HEAD_DIM = 64
NA_WIN_H = 8
NA_WIN_W = 16
DN_CONV_K = 5
DN_CHUNK = 64
DIFF_DIM = 32
FT_DIM = 64
N_EXPERTS = 16
EC_FACTOR = 2
D_EXPERT = 2 * D_MODEL
ROPE_BASE = 10000.0
RMS_EPS = 1e-6
NEG = -1e30

ROW_TILE = 256
N_BF = 6 * GROUP_W
N_F32 = 4 * GROUP_W + 128
VMEM_LIMIT = 56 * 1024 * 1024


def _cparams(*sem):
    return pltpu.CompilerParams(dimension_semantics=sem, vmem_limit_bytes=VMEM_LIMIT)


def _split(x):
    hi = x.astype(BF16)
    lo = (x - hi.astype(F32)).astype(BF16)
    return hi, lo


def _dot(a, b):
    return jnp.dot(a, b, preferred_element_type=F32)


def _dot_nt(a, b):
    return lax.dot_general(a, b, (((1,), (1,)), ((), ())), preferred_element_type=F32)


def _dot_tn(a, b):
    return lax.dot_general(a, b, (((0,), (0,)), ((), ())), preferred_element_type=F32)


def _dot3(a, b):
    ah, al = _split(a)
    bh, bl = _split(b)
    return _dot(ah, bh) + (_dot(al, bh) + _dot(ah, bl))


def _dot3_const(ah, al, b):
    bh, bl = _split(b)
    return _dot(ah, bh) + (_dot(al, bh) + _dot(ah, bl))


def _silu(x):
    return x * jax.nn.sigmoid(x)


def _mod_kernel(s_ref, w_ref, b_ref, o_ref):
    s = _silu(s_ref[...])
    o_ref[...] = _dot(s.astype(BF16), w_ref[...].astype(BF16)) + b_ref[...]


def _modulation(cc, w_mod, b_mod):
    depth = w_mod.shape[0]
    nt = 6 * D_MODEL // 1024
    return pl.pallas_call(
        _mod_kernel,
        grid=(depth, nt),
        in_specs=[
            pl.BlockSpec((8, D_MODEL), lambda l, j: (0, 0)),
            pl.BlockSpec((None, D_MODEL, 1024), lambda l, j: (l, 0, j)),
            pl.BlockSpec((None, 1, 1024), lambda l, j: (l, 0, j)),
        ],
        out_specs=pl.BlockSpec((None, 8, 1024), lambda l, j: (l, 0, j)),
        out_shape=jax.ShapeDtypeStruct((depth, 8, 6 * D_MODEL), F32),
        compiler_params=_cparams("parallel", "parallel"),
        name="modulation",
    )(cc, w_mod, b_mod.reshape(depth, 1, 6 * D_MODEL))


def _norm_mod(x, g, shift, scale):
    y = x * lax.rsqrt(jnp.mean(x * x, axis=-1, keepdims=True) + RMS_EPS) * g
    return y * (1.0 + scale) + shift


def _in_kernel(*refs, residual):
    if residual:
        x_ref, moe_ref, mprev_ref, g_ref, m_ref, w_ref, cos_ref, sin_ref, xo_ref, obf_ref, of_ref, ou_ref = refs
        x = x_ref[...] + mprev_ref[5] * moe_ref[...]
        xo_ref[...] = x
    else:
        x_ref, g_ref, m_ref, w_ref, cos_ref, sin_ref, obf_ref, of_ref, ou_ref = refs
        x = x_ref[...]
    h = _norm_mod(x, g_ref[...], m_ref[0], m_ref[1])
    p = _dot(h.astype(BF16), w_ref[...])
    gw = GROUP_W
    lane = lax.broadcasted_iota(jnp.int32, (1, gw), 1)
    even = (lane % 2) == 0
    cos, sin = cos_ref[...], sin_ref[...]

    def rope(t):
        sw = jnp.where(even, pltpu.roll(t, gw - 1, 1), pltpu.roll(t, 1, 1))
        return t * cos + sw * sin

    obf_ref[:, 0:gw] = (p[:, 0:gw] * HEAD_DIM ** -0.5).astype(BF16)
    obf_ref[:, gw:3 * gw] = p[:, gw:3 * gw].astype(BF16)
    obf_ref[:, 3 * gw:4 * gw] = (rope(p[:, 3 * gw:4 * gw]) * DIFF_DIM ** -0.5).astype(BF16)
    obf_ref[:, 4 * gw:5 * gw] = rope(p[:, 4 * gw:5 * gw]).astype(BF16)
    obf_ref[:, 5 * gw:6 * gw] = p[:, 5 * gw:6 * gw].astype(BF16)
    of_ref[:, 0:4 * gw] = p[:, 6 * gw:10 * gw]
    of_ref[:, 4 * gw:4 * gw + 128] = p[:, 11 * gw:11 * gw + 128]
    ou_ref[...] = p[:, 10 * gw:11 * gw]


def _in_proj(x, prev, g, mods, w_perm, cos_t, sin_t, n_batch, n_tok):
    nt = n_tok // ROW_TILE
    ctx_tile = nt - 1
    n_out = w_perm.shape[1]
    row = lambda b, i: (b, i, 0)
    x_spec = pl.BlockSpec((None, ROW_TILE, D_MODEL), row)
    mod_spec = pl.BlockSpec((None, 6, 1, D_MODEL), lambda b, i: (jnp.where(i == ctx_tile, n_batch, b), 0, 0, 0))
    in_specs = [
        pl.BlockSpec((1, D_MODEL), lambda b, i: (0, 0)),
        mod_spec,
        pl.BlockSpec((D_MODEL, n_out), lambda b, i: (0, 0)),
        pl.BlockSpec((ROW_TILE, GROUP_W), lambda b, i: (i, 0)),
        pl.BlockSpec((ROW_TILE, GROUP_W), lambda b, i: (i, 0)),
    ]
    out_specs = [
        pl.BlockSpec((None, ROW_TILE, N_BF), row),
        pl.BlockSpec((None, ROW_TILE, N_F32), row),
        pl.BlockSpec((None, ROW_TILE, GROUP_W), row),
    ]
    out_shape = [
        jax.ShapeDtypeStruct((n_batch, n_tok, N_BF), BF16),
        jax.ShapeDtypeStruct((n_batch, n_tok, N_F32), F32),
        jax.ShapeDtypeStruct((n_batch, n_tok, GROUP_W), F32),
    ]
    args = (g.reshape(1, D_MODEL), mods, w_perm, cos_t, sin_t)
    if prev is None:
        in_specs, args = [x_spec] + in_specs, (x,) + args
    else:
        in_specs, args = [x_spec, x_spec, mod_spec] + in_specs, (x,) + tuple(prev) + args
        out_specs = [x_spec] + out_specs
        out_shape = [jax.ShapeDtypeStruct((n_batch, n_tok, D_MODEL), F32)] + out_shape
    outs = pl.pallas_call(
        functools.partial(_in_kernel, residual=prev is not None),
        grid=(n_batch, nt),
        in_specs=in_specs,
        out_specs=out_specs,
        out_shape=out_shape,
        compiler_params=_cparams("parallel", "parallel"),
        name="in_proj",
    )(*args)
    return tuple(outs) if prev is not None else (x,) + tuple(outs)


NA_SAMPLES = 2


def _na_kernel(q_ref, k_ref, v_ref, bias_ref, o_ref, *, n_lat, n_ctx):
    i = pl.program_id(1)
    rows = n_lat // GRID_W
    wh = min(NA_WIN_H, rows)
    n_win = wh * GRID_W
    per = ROW_TILE // GRID_W
    lane = lax.broadcasted_iota(jnp.int32, (GRID_W, GROUP_W), 1)
    masks = [lane // HEAD_DIM == h for h in range(GROUP_HEADS)]
    scores, vals, slabs = [], [], []
    for bb in range(q_ref.shape[0]):
        kc = k_ref[bb, n_lat:n_lat + n_ctx, :]
        vc = v_ref[bb, n_lat:n_lat + n_ctx, :]
        for u in range(per):
            r = i * per + u
            is_lat = r < rows
            base = jnp.where(is_lat, jnp.clip(r - wh // 2, 0, rows - wh), 0)
            slabs.append(jnp.where(is_lat, base - r + NA_WIN_H - 1, NA_WIN_H))
            start = pl.multiple_of(base * GRID_W, GRID_W)
            q = q_ref[bb, u * GRID_W:(u + 1) * GRID_W, :]
            qm = jnp.concatenate([jnp.where(m, q, jnp.zeros_like(q)) for m in masks], axis=0)
            keys = jnp.concatenate([k_ref[bb, pl.ds(start, n_win), :], kc], axis=0)
            vals.append(jnp.concatenate([v_ref[bb, pl.ds(start, n_win), :], vc], axis=0))
            scores.append(_dot_nt(qm, keys))
    probs, denoms = [], []
    for s, slab in zip(scores, slabs):
        sw = s[:, :n_win] + bias_ref[slab].reshape(GROUP_HEADS * GRID_W, n_win)
        sc = s[:, n_win:]
        m = jnp.maximum(jnp.max(sw, axis=-1, keepdims=True), jnp.max(sc, axis=-1, keepdims=True))
        pw = jnp.exp(sw - m)
        pc = jnp.exp(sc - m)
        denoms.append(jnp.sum(pw, axis=-1, keepdims=True) + jnp.sum(pc, axis=-1, keepdims=True))
        probs.append(jnp.concatenate([pw, pc], axis=1).astype(BF16))
    outs = []
    for p, v, l in zip(probs, vals, denoms):
        res = _dot(p, v) / l
        o = jnp.where(masks[0], res[0:GRID_W], 0.0)
        for h in range(1, GROUP_HEADS):
            o = o + jnp.where(masks[h], res[h * GRID_W:(h + 1) * GRID_W], 0.0)
        outs.append(o)
    o_ref[...] = jnp.concatenate(outs, axis=0).astype(BF16).reshape(o_ref.shape)


def _na_attention(pbf, bias_tab, n_batch, n_lat, n_ctx):
    n_tok = n_lat + n_ctx
    assert n_batch % NA_SAMPLES == 0
    return pl.pallas_call(
        functools.partial(_na_kernel, n_lat=n_lat, n_ctx=n_ctx),
        grid=(n_batch // NA_SAMPLES, n_tok // ROW_TILE),
        in_specs=[
            pl.BlockSpec((NA_SAMPLES, ROW_TILE, GROUP_W), lambda b, i: (b, i, 0)),
            pl.BlockSpec((NA_SAMPLES, n_tok, GROUP_W), lambda b, i: (b, 0, 1)),
            pl.BlockSpec((NA_SAMPLES, n_tok, GROUP_W), lambda b, i: (b, 0, 2)),
            pl.BlockSpec(bias_tab.shape, lambda b, i: (0, 0, 0, 0)),
        ],
        out_specs=pl.BlockSpec((NA_SAMPLES, ROW_TILE, GROUP_W), lambda b, i: (b, i, 0)),
        out_shape=jax.ShapeDtypeStruct((n_batch, n_tok, GROUP_W), BF16),
        compiler_params=_cparams("parallel", "arbitrary"),
        name="na_attention",
    )(pbf, pbf, pbf, bias_tab)


def _na_bias_table(rpb):
    wh = NA_WIN_H
    cq = np.arange(GRID_W)
    c0 = np.clip(cq - NA_WIN_W // 2, 0, GRID_W - NA_WIN_W)
    in_win = (cq[None, :] >= c0[:, None]) & (cq[None, :] < c0[:, None] + NA_WIN_W)
    dx = np.clip(cq[None, :] - cq[:, None], 1 - NA_WIN_W, NA_WIN_W - 1) + NA_WIN_W - 1
    dy = np.arange(wh)[:, None] + np.arange(wh)[None, :]
    pick_x = (dx[:, :, None] == np.arange(2 * NA_WIN_W - 1)).astype(np.float32)
    pick_y = (dy[:, :, None] == np.arange(2 * NA_WIN_H - 1)).astype(np.float32)
    hi = lax.Precision.HIGHEST
    cols = jnp.einsum('hyx,qkx->hyqk', rpb.astype(F32), pick_x, precision=hi)
    tab = jnp.einsum('swy,hyqk->shqwk', pick_y, cols, precision=hi)
    tab = jnp.where(in_win[None, None, :, None, :], tab, NEG)
    tab = tab.reshape(wh, GROUP_HEADS, GRID_W, wh * GRID_W)
    return jnp.concatenate([tab, jnp.full((1,) + tab.shape[1:], NEG, F32)], axis=0)


DF_KV = 512


def _df_kernel(lam_ref, q_ref, k_ref, v_ref, g_ref, o_ref, s_ref, *, chunks, lam_init):
    lq1, lk1, lq2, lk2 = lam_ref[0:1, :], lam_ref[1:2, :], lam_ref[2:3, :], lam_ref[3:4, :]
    lam = (jnp.exp(jnp.sum(lq1 * lk1, axis=-1, keepdims=True))
           - jnp.exp(jnp.sum(lq2 * lk2, axis=-1, keepdims=True)) + lam_init)
    q = q_ref[...]
    tq, gw = q.shape
    lane = lax.broadcasted_iota(jnp.int32, (tq, gw), 1)
    origin = chunks[0][0]
    zero = jnp.zeros_like(q)

    def head(h, out, slot):
        qm = [jnp.where(lane // DIFF_DIM == 2 * h + mp, q, zero) for mp in range(2)]
        m = [jnp.full((tq, 128), NEG, F32) for _ in range(2)]
        for st, sz in chunks:
            k = k_ref[st:st + sz, :]
            for mp in range(2):
                s = _dot_nt(qm[mp], k)
                s_ref[(slot + mp) * tq:(slot + mp + 1) * tq, st - origin:st - origin + sz] = s
                for t in range(sz // 128):
                    m[mp] = jnp.maximum(m[mp], s[:, t * 128:(t + 1) * 128])
        m = [jnp.max(x, axis=-1, keepdims=True) for x in m]
        ones_lane = ((h + 1) % GROUP_HEADS) * HEAD_DIM
        acc = [jnp.zeros((tq, gw), F32) for _ in range(2)]
        for st, sz in chunks:
            v = v_ref[st:st + sz, :]
            lane_v = lax.broadcasted_iota(jnp.int32, v.shape, 1)
            v = jnp.where(lane_v == ones_lane, jnp.ones_like(v), v)
            for mp in range(2):
                e = jnp.exp((s_ref[(slot + mp) * tq:(slot + mp + 1) * tq, st - origin:st - origin + sz] - m[mp]).astype(BF16))
                acc[mp] = acc[mp] + _dot(e, v)
        w = []
        for mp in range(2):
            l = jnp.sum(jnp.where(lane == ones_lane, acc[mp], 0.0), axis=-1, keepdims=True)
            w.append(acc[mp] * ((1.0 if mp == 0 else lam) / l))
        return out + jnp.where(lane // HEAD_DIM == h, w[0] - w[1], 0.0)

    def head_pair(i, out):
        return head(2 * i + 1, head(2 * i, out, 0), 2)

    o = lax.fori_loop(0, GROUP_HEADS // 2, head_pair, jnp.zeros((tq, gw), F32))
    sq = o * o
    scale = jnp.zeros((tq, gw), F32)
    for h in range(GROUP_HEADS):
        in_head = lane // HEAD_DIM == h
        ms = jnp.sum(jnp.where(in_head, sq, 0.0), axis=-1, keepdims=True) * (1.0 / HEAD_DIM)
        scale = jnp.where(in_head, lax.rsqrt(ms + RMS_EPS), scale)
    o_ref[...] = (o * scale * g_ref[...] * (1.0 - lam_init)).astype(BF16)


def _df_call(pbf, df_lambda, g_tiled, lam_init, n_batch, n_tok, first_tile, n_tiles, chunks, name):
    n_keys = sum(sz for _, sz in chunks)
    return pl.pallas_call(
        functools.partial(_df_kernel, chunks=chunks, lam_init=lam_init),
        grid=(n_batch, n_tiles),
        in_specs=[
            pl.BlockSpec((4, DIFF_DIM), lambda b, i: (0, 0)),
            pl.BlockSpec((None, ROW_TILE, GROUP_W), lambda b, i: (b, first_tile + i, 3)),
            pl.BlockSpec((None, n_tok, GROUP_W), lambda b, i: (b, 0, 4)),
            pl.BlockSpec((None, n_tok, GROUP_W), lambda b, i: (b, 0, 5)),
            pl.BlockSpec((1, GROUP_W), lambda b, i: (0, 0)),
        ],
        out_specs=pl.BlockSpec((None, ROW_TILE, GROUP_W), lambda b, i: (b, i, 0)),
        out_shape=jax.ShapeDtypeStruct((n_batch, n_tiles * ROW_TILE, GROUP_W), BF16),
        scratch_shapes=[pltpu.VMEM((4 * ROW_TILE, n_keys), F32)],
        compiler_params=_cparams("parallel", "arbitrary"),
        name=name,
    )(df_lambda, pbf, pbf, pbf, g_tiled)


def _df_attention(pbf, df_lambda, g_tiled, lam_init, n_batch, n_lat, n_ctx):
    n_tok = n_lat + n_ctx
    lat_chunks = tuple((st, min(DF_KV, n_tok - st)) for st in range(0, n_tok, DF_KV))
    o_lat = _df_call(pbf, df_lambda, g_tiled, lam_init, n_batch, n_tok, 0, n_lat // ROW_TILE, lat_chunks, "df_attention")
    o_ctx = _df_call(pbf, df_lambda, g_tiled, lam_init, n_batch, n_tok, n_lat // ROW_TILE, n_ctx // ROW_TILE,
                     ((n_lat, n_ctx),), "df_attention_ctx")
    return jnp.concatenate([o_lat, o_ctx], axis=1)


CONV_PAD = 8


def _dn_prep_kernel(u_ref, w_ref, o_ref, pad_ref, *, n_lat, n_ctx):
    j = pl.program_id(1)
    zeros = jnp.zeros((CONV_PAD, GROUP_W), F32)
    lat0 = CONV_PAD
    ctx0 = 2 * CONV_PAD + n_lat
    pad_ref[0:CONV_PAD, :] = zeros
    pad_ref[lat0 + n_lat:ctx0, :] = zeros
    pad_ref[ctx0 + n_ctx:ctx0 + n_ctx + CONV_PAD, :] = zeros
    pad_ref[lat0:lat0 + n_lat, :] = u_ref[0:n_lat, :]
    pad_ref[ctx0:ctx0 + n_ctx, :] = u_ref[n_lat:n_lat + n_ctx, :]
    qscale = jnp.where(j == 0, HEAD_DIM ** -0.5, 1.0)
    half = DN_CONV_K // 2
    for (src0, dst0, n) in ((lat0, 0, n_lat), (ctx0, n_lat, n_ctx)):
        for t0 in range(0, n, ROW_TILE):
            acc = jnp.zeros((ROW_TILE, GROUP_W), F32)
            for tap in range(DN_CONV_K):
                a = src0 + t0 + tap - half
                acc = acc + pad_ref[a:a + ROW_TILE, :] * w_ref[tap:tap + 1, :]
            y = _silu(acc)
            parts = []
            for h in range(GROUP_HEADS):
                yh = y[:, h * HEAD_DIM:(h + 1) * HEAD_DIM]
                nrm = lax.rsqrt(jnp.sum(yh * yh, axis=-1, keepdims=True) + 1e-6) * qscale
                parts.append(yh * jnp.where(j == 2, 1.0, nrm))
            o_ref[dst0 + t0:dst0 + t0 + ROW_TILE, :] = jnp.concatenate(parts, axis=-1)


def _dn_prep(pf, conv_w, n_batch, n_lat, n_ctx):
    n_tok = n_lat + n_ctx
    return pl.pallas_call(
        functools.partial(_dn_prep_kernel, n_lat=n_lat, n_ctx=n_ctx),
        grid=(n_batch, 3),
        in_specs=[
            pl.BlockSpec((None, n_tok, GROUP_W), lambda b, j: (b, 0, j)),
            pl.BlockSpec((DN_CONV_K, GROUP_W), lambda b, j: (0, j)),
        ],
        out_specs=pl.BlockSpec((None, n_tok, GROUP_W), lambda b, j: (b, 0, j)),
        out_shape=jax.ShapeDtypeStruct((n_batch, n_tok, 3 * GROUP_W), F32),
        scratch_shapes=[pltpu.VMEM((n_tok + 3 * CONV_PAD, GROUP_W), F32)],
        compiler_params=_cparams("parallel", "arbitrary"),
        name="dn_prep",
    )(pf, conv_w)


DN_BLOCK = 4 * DN_CHUNK
DN_SAMPLES = 2


def _split3(x):
    x1 = x.astype(BF16)
    r = x - x1.astype(F32)
    x2 = r.astype(BF16)
    return x1, x2, (r - x2.astype(F32)).astype(BF16)


def _block_diag(x, masks):
    zero = jnp.zeros_like(x)
    return jnp.concatenate([jnp.where(m, x, zero) for m in masks], axis=0)


def _mm3_heads(lhs, rhs, masks):
    lh, ll = _split(lhs)
    rh, rl = _split(rhs)
    bh, bl = _block_diag(rh, masks), _block_diag(rl, masks)
    return _dot(jnp.concatenate([lh, ll, lh], axis=1), jnp.concatenate([bh, bh, bl], axis=0))


def _dn_kernel(qf_ref, kf_ref, vf_ref, abf_ref, qb_ref, kb_ref, vb_ref, abb_ref, av_ref, dt_ref, exp_ref,
               of_ref, ob_ref, s_ref):
    @pl.when(pl.program_id(1) == 0)
    def _():
        s_ref[...] = jnp.zeros_like(s_ref)

    c, nb, gw = DN_CHUNK, DN_BLOCK, GROUP_W
    n_sub = nb // c
    lane4 = lax.broadcasted_iota(jnp.int32, (c, gw), 1)
    masks = [(lane4 // HEAD_DIM) == h for h in range(GROUP_HEADS)]
    ri = lax.broadcasted_iota(jnp.int32, (c, gw), 0)
    cj = lane4 % HEAD_DIM
    eye = (ri == cj).astype(F32)
    bi = lax.broadcasted_iota(jnp.int32, (nb, nb), 0)
    bj = lax.broadcasted_iota(jnp.int32, (nb, nb), 1)
    same_chunk = (bi // c) == (bj // c)
    lane1 = lax.broadcasted_iota(jnp.int32, (nb, 128), 1)
    n_samples = qf_ref.shape[0]
    streams = []
    for bb in range(n_samples):
        streams.append((0, qf_ref.at[bb], kf_ref.at[bb], vf_ref.at[bb], abf_ref.at[bb]))
        streams.append((1, qb_ref.at[bb], kb_ref.at[bb], vb_ref.at[bb], abb_ref.at[bb]))

    chains = []
    for sidx, (d, q_ref, k_ref, v_ref, ab_ref) in enumerate(streams):
        q_all, k_all, v_all = q_ref[...], k_ref[...], v_ref[...]
        for j in range(n_sub):
            rows = slice(j * c, (j + 1) * c)
            q, k = q_all[rows], k_all[rows]
            kq = _dot_nt(jnp.concatenate([k, q], axis=0).astype(BF16), _block_diag(k.astype(BF16), masks))
            chains.append(dict(s=sidx, d=d, rows=rows, q=q, k=k, v=v_all[rows], kq=kq))
    gates = []
    for d, q_ref, k_ref, v_ref, ab_ref in streams:
        ab = ab_ref[...]
        x = ab + dt_ref[...]
        softplus = jnp.maximum(x, 0.0) + jnp.log1p(jnp.exp(-jnp.abs(x)))
        g = -jnp.exp(av_ref[...]) * softplus
        tri = (same_chunk & ((bi >= bj) if d == 0 else (bi <= bj))).astype(BF16)
        g1, g2, g3 = _split3(g)
        gcs = _dot(tri, g1) + (_dot(tri, g2) + _dot(tri, g3))
        gates.append(jnp.where(lane1 < 2 * GROUP_HEADS, gcs, jax.nn.sigmoid(ab)))
    spread = []
    for sidx, stream in enumerate(streams):
        x1, x2, x3 = _split3(gates[sidx])
        e = exp_ref[stream[0]]
        spread.append(_dot(x1, e) + (_dot(x2, e) + _dot(x3, e)))
    for ch in chains:
        d, q, k, v, kq = ch["d"], ch["q"], ch["k"], ch["v"], ch["kq"]
        incl = (ri >= cj) if d == 0 else (ri <= cj)
        strict = (ri > cj) if d == 0 else (ri < cj)
        last = c - 1 if d == 0 else 0
        gc, beta = spread[ch["s"]][ch["rows"], :gw], spread[ch["s"]][ch["rows"], gw:]
        gc_row = jnp.sum(gc * eye, axis=0, keepdims=True)
        decay = jnp.exp(jnp.where(incl, gc - gc_row, NEG))
        eg = jnp.exp(gc)
        g_last = gc[last:last + 1, :]
        ch.update(
            lm=jnp.where(strict, kq[:c] * beta * decay, 0.0),
            qk=jnp.where(incl, kq[c:] * decay, 0.0).astype(BF16),
            rhs=jnp.concatenate([_block_diag((v * beta).astype(BF16), masks),
                                 _block_diag((k * beta * eg).astype(BF16), masks)], axis=1),
            q_dec=q * eg,
            k_dec=(k * jnp.exp(g_last - gc)).astype(BF16),
            g_end=jnp.exp(g_last))

    def neumann_level(first):
        for ch in chains:
            if first:
                ch["p"] = eye - ch["lm"]
                ch["sq"] = _mm3_heads(ch["lm"], ch["lm"], masks)
            else:
                r = _mm3_heads(jnp.concatenate([ch["sq"], ch["p"]], axis=0), ch["sq"], masks)
                ch["sq"], ch["p"] = r[:c], ch["p"] + r[c:]

    def head_blocks(full):
        out = jnp.where(masks[0], full[0:c], 0.0)
        for h in range(1, GROUP_HEADS):
            out = out + jnp.where(masks[h], full[h * c:(h + 1) * c], 0.0)
        return out

    n_levels = int(math.log2(c)) - 1
    for level in range(n_levels):
        neumann_level(level == 0)
    t_inv = [ch["p"] + _mm3_heads(ch["p"], ch["sq"], masks) for ch in chains]
    uw = [_dot(t.astype(BF16), ch["rhs"]).astype(BF16) for t, ch in zip(t_inv, chains)]
    kd_uw = [_dot_tn(ch["k_dec"], x) for x, ch in zip(uw, chains)]
    qk_uw = [_dot(ch["qk"], jnp.concatenate([_block_diag(x[:, :gw], masks), _block_diag(x[:, gw:], masks)], axis=1))
             for x, ch in zip(uw, chains)]
    solved = [dict(ms=jnp.concatenate([head_blocks(kd[:, gw:]), ch["q_dec"] - qq[:, gw:]], axis=0).astype(BF16),
                   b=head_blocks(kd[:, :gw]), d=qq[:, :gw], g_end=ch["g_end"])
              for kd, qq, ch in zip(kd_uw, qk_uw, chains)]

    states = [s_ref[sidx] for sidx in range(len(streams))]
    outs = [[None] * n_sub for _ in streams]
    for step in range(n_sub):
        for sidx, stream in enumerate(streams):
            j = step if stream[0] == 0 else n_sub - 1 - step
            ch = solved[sidx * n_sub + j]
            s = states[sidx]
            res = _dot(ch["ms"], _block_diag(s.astype(BF16), masks))
            outs[sidx][j] = res[c:] + ch["d"]
            states[sidx] = s * ch["g_end"] - res[:c] + ch["b"]
    of_ref[...] = jnp.stack([jnp.concatenate(outs[2 * bb], axis=0) for bb in range(n_samples)], axis=0)
    ob_ref[...] = jnp.stack([jnp.concatenate(outs[2 * bb + 1], axis=0) for bb in range(n_samples)], axis=0)
    s_ref[...] = jnp.stack(states, axis=0)


def _dn_scan(qkv, pf, a_vec, dt_vec, n_batch, n_lat, n_ctx):
    n_tok = n_lat + n_ctx
    assert n_ctx == DN_BLOCK and n_lat % DN_BLOCK == 0
    nblk = n_tok // DN_BLOCK
    ab_blk = 4 * GROUP_W // 128

    def fwd(n):
        return jnp.where(n == 0, nblk - 1, n - 1)

    def bwd(n):
        return jnp.where(n == 0, nblk - 1, nblk - 1 - n)

    assert n_batch % DN_SAMPLES == 0

    def spec(order, blk, width):
        return pl.BlockSpec((DN_SAMPLES, DN_BLOCK, width), lambda b, n: (b, order(n), blk))

    spread = np.zeros((2, 128, 2 * GROUP_W), np.float32)
    for d in range(2):
        for h in range(GROUP_HEADS):
            spread[d, GROUP_HEADS * d + h, h * HEAD_DIM:(h + 1) * HEAD_DIM] = 1.0
            spread[d, 2 * GROUP_HEADS + GROUP_HEADS * d + h, GROUP_W + h * HEAD_DIM:GROUP_W + (h + 1) * HEAD_DIM] = 1.0

    lane = pl.BlockSpec((1, 128), lambda b, n: (0, 0))
    out = jax.ShapeDtypeStruct((n_batch, n_tok, GROUP_W), F32)
    return pl.pallas_call(
        _dn_kernel,
        grid=(n_batch // DN_SAMPLES, nblk),
        in_specs=[spec(fwd, 0, GROUP_W), spec(fwd, 1, GROUP_W), spec(fwd, 2, GROUP_W), spec(fwd, ab_blk, 128),
                  spec(bwd, 0, GROUP_W), spec(bwd, 1, GROUP_W), spec(bwd, 2, GROUP_W), spec(bwd, ab_blk, 128),
                  lane, lane, pl.BlockSpec((2, 128, 2 * GROUP_W), lambda b, n: (0, 0, 0))],
        out_specs=[spec(fwd, 0, GROUP_W), spec(bwd, 0, GROUP_W)],
        out_shape=[out, out],
        scratch_shapes=[pltpu.VMEM((2 * DN_SAMPLES, HEAD_DIM, GROUP_W), F32)],
        compiler_params=_cparams("parallel", "arbitrary"),
        name="dn_scan",
    )(qkv, qkv, qkv, pf, qkv, qkv, qkv, pf, a_vec, dt_vec, jnp.asarray(spread, BF16))


FT_N2 = 64


def _ft1_kernel(x_ref, fh_ref, fl_ref, o_ref):
    o_ref[...] = _dot3_const(fh_ref[...], fl_ref[...], x_ref[...])


def _ft_stage1(u_rows, f_hi, f_lo, n_batch, n1, tn=2048):
    width = u_rows.shape[2]
    return pl.pallas_call(
        _ft1_kernel,
        grid=(n_batch, width // tn),
        in_specs=[
            pl.BlockSpec((None, n1, tn), lambda b, j: (b, 0, j)),
            pl.BlockSpec((2 * n1, n1), lambda b, j: (0, 0)),
            pl.BlockSpec((2 * n1, n1), lambda b, j: (0, 0)),
        ],
        out_specs=pl.BlockSpec((None, 2 * n1, tn), lambda b, j: (b, 0, j)),
        out_shape=jax.ShapeDtypeStruct((n_batch, 2 * n1, width), F32),
        compiler_params=_cparams("parallel", "parallel"),
        name="ft_stage1",
    )(u_rows, f_hi, f_lo)


def _ft3_kernel(y_ref, gh_ref, gl_ref, ch_ref, cl_ref, sh_ref, sl_ref, w_ref, o_ref, *, pb, n2, norm):
    vr, vi = [], []
    for p in range(pb):
        rhs = jnp.concatenate([y_ref[0, p], y_ref[1, p]], axis=0)
        v = _dot3_const(gh_ref[p], gl_ref[p], rhs)
        vr.append(v[:n2])
        vi.append(v[n2:])
    vr = jnp.concatenate(vr, axis=0)
    vi = jnp.concatenate(vi, axis=0)
    vrh, vrl = _split(vr)
    vih, vil = _split(vi)
    y = (_dot(vrh, ch_ref[...]) + (_dot(vrl, ch_ref[...]) + _dot(vrh, cl_ref[...]))
         + _dot(vih, sh_ref[...]) + (_dot(vil, sh_ref[...]) + _dot(vih, sl_ref[...]))) * norm
    o = _dot(y.astype(BF16), w_ref[...])
    for p in range(pb):
        o_ref[:, p, :] = o[p * n2:(p + 1) * n2, :]


def _ft_stage2(y1, g_hi, g_lo, chan, ft_w, n_batch, n1, n2, pb):
    ch, cl, sh, sl = chan
    full = lambda b, i: (0, 0)
    return pl.pallas_call(
        functools.partial(_ft3_kernel, pb=pb, n2=n2, norm=1.0 / math.sqrt(n1 * n2 * FT_DIM)),
        grid=(n_batch, n1 // pb),
        in_specs=[
            pl.BlockSpec((None, 2, pb, n2, GROUP_W), lambda b, i: (b, 0, i, 0, 0)),
            pl.BlockSpec((pb, 2 * n2, 2 * n2), lambda b, i: (i, 0, 0)),
            pl.BlockSpec((pb, 2 * n2, 2 * n2), lambda b, i: (i, 0, 0)),
            pl.BlockSpec((GROUP_W, GROUP_W), full), pl.BlockSpec((GROUP_W, GROUP_W), full),
            pl.BlockSpec((GROUP_W, GROUP_W), full), pl.BlockSpec((GROUP_W, GROUP_W), full),
            pl.BlockSpec((GROUP_W, GROUP_W), full),
        ],
        out_specs=pl.BlockSpec((None, n2, pb, GROUP_W), lambda b, i: (b, 0, i, 0)),
        out_shape=jax.ShapeDtypeStruct((n_batch, n2, n1, GROUP_W), F32),
        compiler_params=_cparams("parallel", "parallel"),
        name="ft_stage2",
    )(y1, g_hi, g_lo, ch, cl, sh, sl, ft_w)


def _np_split(a):
    hi = jnp.asarray(a, F32).astype(BF16)
    lo = (jnp.asarray(a, F32) - hi.astype(F32)).astype(BF16)
    return hi, lo


@functools.lru_cache(maxsize=None)
def _ft_tables(n1, n2):
    n = n1 * n2
    p = np.arange(n1)
    ang1 = 2.0 * np.pi * ((p[:, None] * p[None, :]) % n1) / n1
    f1 = np.concatenate([np.cos(ang1), -np.sin(ang1)], axis=0)
    q = np.arange(n2)
    b = np.arange(n2)
    phase = (b[None, None, :] * (p[:, None, None] + n1 * q[None, :, None])) % n
    psi = 2.0 * np.pi * phase / n
    gc, gs = np.cos(psi), np.sin(psi)
    g = np.concatenate([np.concatenate([gc, gs], axis=2), np.concatenate([-gs, gc], axis=2)], axis=1)
    c = np.arange(FT_DIM)
    angc = 2.0 * np.pi * ((c[:, None] * c[None, :]) % FT_DIM) / FT_DIM
    eye = np.eye(GROUP_W // FT_DIM)
    cc, sc = np.kron(eye, np.cos(angc)), np.kron(eye, np.sin(angc))
    return f1.astype(np.float32), g.astype(np.float32), cc.astype(np.float32), sc.astype(np.float32)


def _out_kernel(x_ref, m_ref, na_ref, of_ref, ob_ref, gate_ref, dng_ref, df_ref, ft_ref, w_ref, g2_ref, wr_ref,
                xo_ref, h_ref, aff_ref):
    o = of_ref[...] + ob_ref[...]
    parts = []
    for h in range(GROUP_HEADS):
        oh = o[:, h * HEAD_DIM:(h + 1) * HEAD_DIM]
        parts.append(oh * lax.rsqrt(jnp.mean(oh * oh, axis=-1, keepdims=True) + RMS_EPS))
    y_dn = jnp.concatenate(parts, axis=-1) * dng_ref[...] * _silu(gate_ref[...])
    mix = jnp.concatenate([na_ref[...], y_dn.astype(BF16), df_ref[...], ft_ref[...].astype(BF16)], axis=-1)
    x = x_ref[...] + m_ref[2] * _dot(mix, w_ref[...])
    xo_ref[...] = x
    h2 = _norm_mod(x, g2_ref[...], m_ref[3], m_ref[4])
    bits = pltpu.bitcast(h2.astype(BF16).astype(F32), jnp.uint32)
    half = D_MODEL // 2
    h_ref[...] = bits[:, half:] | (bits[:, :half] >> 16)
    logits = _dot3(h2, wr_ref[...])
    lane = lax.broadcasted_iota(jnp.int32, logits.shape, 1)
    logits = jnp.where(lane < N_EXPERTS, logits, NEG)
    e = jnp.exp(logits - jnp.max(logits, axis=-1, keepdims=True))
    aff_ref[...] = e / jnp.sum(e, axis=-1, keepdims=True)


def _out_proj(x, mods, o_na, o_f, o_b, pf, dn_g, o_df, o_ft, w_out, g2, w_router, n_batch, n_tok):
    nt = n_tok // ROW_TILE
    ctx_tile = nt - 1
    row = lambda b, i: (b, i, 0)
    blk = lambda: pl.BlockSpec((None, ROW_TILE, GROUP_W), row)
    vec = lambda n: pl.BlockSpec((1, n), lambda b, i: (0, 0))
    return pl.pallas_call(
        _out_kernel,
        grid=(n_batch, nt),
        in_specs=[
            pl.BlockSpec((None, ROW_TILE, D_MODEL), row),
            pl.BlockSpec((None, 6, 1, D_MODEL), lambda b, i: (jnp.where(i == ctx_tile, n_batch, b), 0, 0, 0)),
            blk(), blk(), blk(),
            pl.BlockSpec((None, ROW_TILE, GROUP_W), lambda b, i: (b, i, 3)),
            vec(GROUP_W), blk(), blk(),
            pl.BlockSpec((D_MODEL, D_MODEL), lambda b, i: (0, 0)),
            vec(D_MODEL),
            pl.BlockSpec((D_MODEL, 128), lambda b, i: (0, 0)),
        ],
        out_specs=[
            pl.BlockSpec((None, ROW_TILE, D_MODEL), row),
            pl.BlockSpec((None, ROW_TILE, D_MODEL // 2), row),
            pl.BlockSpec((None, ROW_TILE, 128), row),
        ],
        out_shape=[
            jax.ShapeDtypeStruct((n_batch, n_tok, D_MODEL), F32),
            jax.ShapeDtypeStruct((n_batch, n_tok, D_MODEL // 2), jnp.uint32),
            jax.ShapeDtypeStruct((n_batch, n_tok, 128), F32),
        ],
        compiler_params=_cparams("parallel", "parallel"),
        name="out_proj_router",
    )(x, mods, o_na, o_f, o_b, pf, dn_g, o_df, o_ft, w_out, g2.reshape(1, D_MODEL), w_router)


MOE_TF = 256
GATHER_UNROLL = 8


def _gather_kernel(idx_ref, h_ref, o_ref, *, cap, n_e):
    base = (pl.program_id(0) * n_e + pl.program_id(1)) * cap

    def body(j, carry):
        for u in range(GATHER_UNROLL):
            o_ref[j, u:u + 1, :] = h_ref[pl.ds(idx_ref[base + j * GATHER_UNROLL + u], 1), :]
        return carry

    lax.fori_loop(0, cap // GATHER_UNROLL, body, 0)


def _moe_gather(hp, idx):
    n_batch, n_tok, width = hp.shape
    _, n_e, cap = idx.shape
    assert cap % GATHER_UNROLL == 0 and GATHER_UNROLL == 8
    groups = cap // GATHER_UNROLL
    out = pl.pallas_call(
        functools.partial(_gather_kernel, cap=cap, n_e=n_e),
        grid_spec=pltpu.PrefetchScalarGridSpec(
            num_scalar_prefetch=1,
            grid=(n_batch, n_e),
            in_specs=[pl.BlockSpec((None, n_tok, width), lambda b, e, idx_ref: (b, 0, 0))],
            out_specs=pl.BlockSpec((None, groups, GATHER_UNROLL, width), lambda b, e, idx_ref: (e, b, 0, 0)),
        ),
        out_shape=jax.ShapeDtypeStruct((n_e, n_batch * groups, GATHER_UNROLL, width), jnp.uint32),
        compiler_params=_cparams("parallel", "arbitrary"),
        name="moe_gather",
    )(idx.reshape(-1), hp)
    return out.reshape(n_e, n_batch * cap, width)


def _combine_kernel(idx_ref, y_ref, o_ref, *, cap, n_e):
    e = pl.program_id(1)

    @pl.when(e == 0)
    def _():
        o_ref[...] = jnp.zeros_like(o_ref)

    base = (pl.program_id(0) * n_e + e) * cap

    def body(j, carry):
        first = j * GATHER_UNROLL
        toks = [idx_ref[base + first + u] for u in range(GATHER_UNROLL)]
        rows = [o_ref[pl.ds(t, 1), :] + y_ref[j, u:u + 1, :] for u, t in enumerate(toks)]
        for t, row in zip(toks, rows):
            o_ref[pl.ds(t, 1), :] = row
        return carry

    lax.fori_loop(0, cap // GATHER_UNROLL, body, 0)


def _moe_combine(y, idx, n_tok):
    n_batch, n_e, cap = idx.shape
    d = y.shape[2]
    groups = cap // GATHER_UNROLL
    return pl.pallas_call(
        functools.partial(_combine_kernel, cap=cap, n_e=n_e),
        grid_spec=pltpu.PrefetchScalarGridSpec(
            num_scalar_prefetch=1,
            grid=(n_batch, n_e),
            in_specs=[pl.BlockSpec((None, groups, GATHER_UNROLL, d), lambda b, e, idx_ref: (e, b, 0, 0))],
            out_specs=pl.BlockSpec((None, n_tok, d), lambda b, e, idx_ref: (b, 0, 0)),
        ),
        out_shape=jax.ShapeDtypeStruct((n_batch, n_tok, d), F32),
        compiler_params=_cparams("parallel", "arbitrary"),
        name="moe_combine",
    )(idx.reshape(-1), y.reshape(n_e, n_batch * groups, GATHER_UNROLL, d))


def _moe_kernel(x_ref, wg_ref, wu_ref, wd_ref, gate_ref, o_ref, xb_ref):
    f = pl.program_id(1)

    @pl.when(f == 0)
    def _():
        packed = x_ref[...]
        xb_ref[...] = jnp.concatenate([pltpu.bitcast(packed << 16, F32),
                                       pltpu.bitcast(packed & jnp.uint32(0xFFFF0000), F32)], axis=1).astype(BF16)
        o_ref[...] = jnp.zeros_like(o_ref)

    x = xb_ref[...]
    a = _dot(x, wg_ref[...].astype(BF16))
    u = _dot(x, wu_ref[...].astype(BF16))
    hid = (_silu(a) * u).astype(BF16)
    o_ref[...] += _dot(hid, wd_ref[...].astype(BF16))

    @pl.when(f == pl.num_programs(1) - 1)
    def _():
        o_ref[...] = o_ref[...] * gate_ref[...]


def _moe_ffn(xg, w_gate, w_up, w_down, gate, layer):
    n_e, m, _ = xg.shape
    return pl.pallas_call(
        _moe_kernel,
        grid=(n_e, D_EXPERT // MOE_TF),
        in_specs=[
            pl.BlockSpec((None, m, D_MODEL // 2), lambda e, f: (e, 0, 0)),
            pl.BlockSpec((None, None, D_MODEL, MOE_TF), lambda e, f: (layer, e, 0, f)),
            pl.BlockSpec((None, None, D_MODEL, MOE_TF), lambda e, f: (layer, e, 0, f)),
            pl.BlockSpec((None, None, MOE_TF, D_MODEL), lambda e, f: (layer, e, f, 0)),
            pl.BlockSpec((None, m, 1), lambda e, f: (e, 0, 0)),
        ],
        out_specs=pl.BlockSpec((None, m, D_MODEL), lambda e, f: (e, 0, 0)),
        out_shape=jax.ShapeDtypeStruct((n_e, m, D_MODEL), F32),
        scratch_shapes=[pltpu.VMEM((m, D_MODEL), BF16)],
        compiler_params=_cparams("parallel", "arbitrary"),
        name="moe_ffn",
    )(xg, w_gate, w_up, w_down, gate)


def _final_kernel(x_ref, moe_ref, m_ref, g_ref, o_ref):
    x = x_ref[...] + m_ref[5] * moe_ref[...]
    o_ref[...] = x * lax.rsqrt(jnp.mean(x * x, axis=-1, keepdims=True) + RMS_EPS) * g_ref[...]


def _final_norm(x, moe, mods, g, n_batch, n_lat):
    row = lambda b, i: (b, i, 0)
    blk = pl.BlockSpec((None, ROW_TILE, D_MODEL), row)
    return pl.pallas_call(
        _final_kernel,
        grid=(n_batch, n_lat // ROW_TILE),
        in_specs=[blk, blk, pl.BlockSpec((None, 6, 1, D_MODEL), lambda b, i: (b, 0, 0, 0)),
                  pl.BlockSpec((1, D_MODEL), lambda b, i: (0, 0))],
        out_specs=blk,
        out_shape=jax.ShapeDtypeStruct((n_batch, n_lat, D_MODEL), F32),
        compiler_params=_cparams("parallel", "parallel"),
        name="final_norm",
    )(x, moe, mods, g.reshape(1, D_MODEL))


def _reorder_in_columns(w):
    gw = GROUP_W
    o = 6 * gw + 4 * GROUP_HEADS
    pad = jnp.zeros((w.shape[0], 128 - 4 * GROUP_HEADS), w.dtype)
    parts = [w[:, 0:3 * gw], w[:, o + gw:o + 4 * gw], w[:, 3 * gw:6 * gw], w[:, o:o + gw],
             w[:, o + 4 * gw:o + 5 * gw], w[:, 6 * gw:o], pad]
    return jnp.concatenate(parts, axis=1).astype(BF16)


@functools.lru_cache(maxsize=None)
def _rope_tables(n_lat, n_ctx):
    t = np.arange(n_lat)
    pos = np.stack([t // GRID_W, t % GRID_W], axis=-1).astype(np.float32)
    n_freq = DIFF_DIM // 4
    inv = (ROPE_BASE ** (-np.arange(n_freq, dtype=np.float32) / n_freq)).astype(np.float32)
    ang = (pos[:, :, None] * inv).reshape(n_lat, 2 * n_freq)
    lane = np.arange(GROUP_W)
    idx = (lane % DIFF_DIM) // 2
    sign = np.where(lane % 2 == 0, -1.0, 1.0)
    cos = np.concatenate([np.cos(ang)[:, idx], np.ones((n_ctx, GROUP_W))], axis=0)
    sin = np.concatenate([np.sin(ang)[:, idx] * sign, np.zeros((n_ctx, GROUP_W))], axis=0)
    return cos.astype(np.float32), sin.astype(np.float32)


def _lane_vec(v, n=128):
    v = v.reshape(-1).astype(F32)
    return jnp.zeros((1, n), F32).at[0, :v.shape[0]].set(v)


def _route(aff, cap):
    gate, idx = lax.top_k(jnp.swapaxes(aff, 1, 2), cap)
    return gate, idx


def kernel(x, c, ctx, c_ctx, w_mod, b_mod, norm1_g, w_in, na_rpb, dn_conv_w, dn_a_log, dn_dt_bias, dn_norm_g, df_lambda, df_norm_g, ft_w, w_out, norm2_g, w_router, w_gate, w_up, w_down, final_norm_g):
    n_batch, n_lat, _ = x.shape
    n_ctx = ctx.shape[1]
    n_tok = n_lat + n_ctx
    depth = w_mod.shape[0]
    assert n_batch + 1 <= 8 and n_ctx == ROW_TILE and n_lat % ROW_TILE == 0

    xs = jnp.concatenate([x, ctx], axis=1)
    cc = jnp.zeros((8, D_MODEL), F32).at[:n_batch].set(c).at[n_batch].set(c_ctx)
    m_all = _modulation(cc, w_mod, b_mod)

    cos_t, sin_t = (jnp.asarray(a) for a in _rope_tables(n_lat, n_ctx))
    n1 = n_lat // FT_N2
    f1, g_lat, cc_m, sc_m = _ft_tables(n1, FT_N2)
    _, g_ctx, _, _ = _ft_tables(1, n_ctx)
    f1h, f1l = _np_split(f1)
    glh, gll = _np_split(g_lat)
    gch, gcl = _np_split(g_ctx)
    chan = _np_split(cc_m) + _np_split(sc_m)
    cap_lat = EC_FACTOR * n_lat // N_EXPERTS
    cap_ctx = EC_FACTOR * n_ctx // N_EXPERTS

    prev = None
    for l in range(depth):
        lam_init = 0.8 - 0.6 * math.exp(-0.3 * l)
        mods = m_all[l, :n_batch + 1].reshape(n_batch + 1, 6, 1, D_MODEL)
        xs, pbf, pf, pu = _in_proj(xs, prev, norm1_g[l], mods, _reorder_in_columns(w_in[l]), cos_t, sin_t, n_batch, n_tok)

        o_na = _na_attention(pbf, _na_bias_table(na_rpb[l]), n_batch, n_lat, n_ctx)

        qkv = _dn_prep(pf, dn_conv_w[l], n_batch, n_lat, n_ctx)
        o_f, o_b = _dn_scan(qkv, pf, _lane_vec(dn_a_log[l]), _lane_vec(dn_dt_bias[l]), n_batch, n_lat, n_ctx)

        o_df = _df_attention(pbf, df_lambda[l], jnp.tile(df_norm_g[l], GROUP_HEADS).reshape(1, GROUP_W),
                             lam_init, n_batch, n_lat, n_ctx)

        ftw = ft_w[l].astype(BF16)
        y1 = _ft_stage1(pu.reshape(n_batch, n_tok // FT_N2, FT_N2 * GROUP_W), f1h, f1l, n_batch, n1)
        o_lat = _ft_stage2(y1.reshape(n_batch, 2, n1, FT_N2, GROUP_W), glh, gll, chan, ftw, n_batch, n1, FT_N2, 8)
        o_lat = o_lat.reshape(n_batch, n_lat, GROUP_W)
        u_ctx = pu[:, n_lat:]
        y1c = jnp.stack([u_ctx, jnp.zeros_like(u_ctx)], axis=1)[:, :, None]
        o_ctx = _ft_stage2(y1c, gch, gcl, chan, ftw, n_batch, 1, n_ctx, 1).reshape(n_batch, n_ctx, GROUP_W)
        o_ft = jnp.concatenate([o_lat, o_ctx], axis=1)

        wr = jnp.pad(w_router[l], ((0, 0), (0, 128 - N_EXPERTS)))
        xs, h2, aff = _out_proj(xs, mods, o_na, o_f, o_b, pf, jnp.tile(dn_norm_g[l], GROUP_HEADS).reshape(1, GROUP_W),
                                o_df, o_ft, w_out[l].astype(BF16), norm2_g[l], wr, n_batch, n_tok)

        gate_l, idx_l = _route(aff[:, :n_lat, :N_EXPERTS], cap_lat)
        gate_c, idx_c = _route(aff[:, n_lat:, :N_EXPERTS], cap_ctx)
        idx = jnp.concatenate([idx_l, idx_c + n_lat], axis=2)
        gate = jnp.concatenate([gate_l, gate_c], axis=2)
        cap = cap_lat + cap_ctx
        gate_col = jnp.swapaxes(gate, 0, 1).reshape(N_EXPERTS, n_batch * cap, 1)
        y = _moe_ffn(_moe_gather(h2, idx), w_gate, w_up, w_down, gate_col, l)
        prev = (_moe_combine(y, idx, n_tok), mods)

    return _final_norm(xs, prev[0], prev[1], final_norm_g, n_batch, n_lat)
```
